```python
import jax
import jax.numpy as jnp
from jax import lax
import numpy as np

D_MODEL = 1024
BATCH = 2
SEQ = 8192
DEPTH = 1

NSA_HEADS = 8
NSA_KV_GROUPS = 2
NSA_HEAD_DIM = 64
CMP_LEN = 32
CMP_STRIDE = 16
CMP_HIDDEN = 128
SLC_LEN = 64
N_SELECT = 16
WINDOW = 512
Q_BLOCK = 128
FORCE_SCORE = 1e4
HGRN_HEADS = 4
HGRN_KEY_DIM = 128
HGRN_VAL_DIM = 128
HGRN_CHUNK = 64
D_FF = -(-8 * D_MODEL // (3 * 256)) * 256
RMS_EPS = 1e-6

NSA_WIDTH = NSA_HEADS * NSA_HEAD_DIM
NSA_KV_WIDTH = NSA_KV_GROUPS * NSA_HEAD_DIM
HGRN_KEY_WIDTH = HGRN_HEADS * HGRN_KEY_DIM
HGRN_VAL_WIDTH = HGRN_HEADS * HGRN_VAL_DIM
MIX_WIDTH = NSA_WIDTH + HGRN_VAL_WIDTH
IN_SIZES = (NSA_WIDTH,) + (NSA_KV_WIDTH,) * 6 + (3 * NSA_HEADS, HGRN_KEY_WIDTH, HGRN_KEY_WIDTH, HGRN_VAL_WIDTH, HGRN_VAL_WIDTH)
IN_WIDTH = sum(IN_SIZES)

kernel_name = 'nsa_hgrn2_hybrid_layer'


def rms_norm(x, g):
    xf = x.astype(jnp.float32)
    y = xf * lax.rsqrt(jnp.mean(xf * xf, axis=-1, keepdims=True) + RMS_EPS)
    return (y * g.astype(jnp.float32)).astype(x.dtype)


def masked_softmax(s, mask):
    s = jnp.where(mask, s.astype(jnp.float32), -1e30)
    p = jax.nn.softmax(s, axis=-1)
    return jnp.where(mask, p, 0.0)


def compress(seq, pos, w1, w2):
    B, T, G, dh = seq.shape
    ratio = CMP_LEN // CMP_STRIDE
    nseg = T // CMP_STRIDE
    ncmp = nseg - ratio + 1
    seg = seq.reshape(B, nseg, CMP_STRIDE, G, dh)
    blocks = jnp.concatenate([seg[:, r:r + ncmp] for r in range(ratio)], axis=2)
    blocks = blocks + pos[None, None, :, None, :]
    flat = blocks.transpose(0, 1, 3, 2, 4).reshape(B, ncmp, G, CMP_LEN * dh)
    return jax.nn.gelu(flat @ w1) @ w2


def nsa_mixer(q, kc, vc, ks, vs, kw, vw, gates):
    B, T, H, dh = q.shape
    G = NSA_KV_GROUPS
    R = H // G
    ncmp = kc.shape[1]
    nslc = T // SLC_LEN
    n_sel = min(N_SELECT, nslc)
    scale = dh ** -0.5
    cmp_start = jnp.arange(ncmp) * CMP_STRIDE
    cmp_end = cmp_start + CMP_LEN - 1
    slc_start = jnp.arange(nslc) * SLC_LEN
    overlap = ((cmp_start[:, None] < slc_start[None, :] + SLC_LEN)
               & (cmp_start[:, None] + CMP_LEN > slc_start[None, :])).astype(jnp.float32)
    ks_blk = ks.reshape(B, nslc, SLC_LEN, G, dh).transpose(0, 3, 1, 2, 4)
    vs_blk = vs.reshape(B, nslc, SLC_LEN, G, dh).transpose(0, 3, 1, 2, 4)
    pad = ((0, 0), (WINDOW, 0), (0, 0), (0, 0))
    kw_pad = jnp.pad(kw, pad)
    vw_pad = jnp.pad(vw, pad)
    bi = jnp.arange(B)[:, None, None, None]
    gi = jnp.arange(G)[None, :, None, None]
    j = jnp.arange(nslc)

    def block(qb):
        s = qb * Q_BLOCK
        t = s + jnp.arange(Q_BLOCK)
        qblk = lax.dynamic_slice_in_dim(q, s, Q_BLOCK, 1).reshape(B, Q_BLOCK, G, R, dh)
        sc = jnp.einsum('bqgrd,bngd->bgrqn', qblk, kc) * scale
        p_cmp = masked_softmax(sc, cmp_end[None, :] <= t[:, None])
        o_cmp = jnp.einsum('bgrqn,bngd->bqgrd', p_cmp, vc.astype(jnp.float32))
        imp = jnp.einsum('bgrqn,nj->bgqj', p_cmp, overlap)
        jcur = t // SLC_LEN
        forced = (j[None, :] == 0) | (j[None, :] == jcur[:, None]) | (j[None, :] == jcur[:, None] - 1)
        future = j[None, :] > jcur[:, None]
        imp = jnp.where(forced, FORCE_SCORE, jnp.where(future, -1.0, imp))
        _, idx = lax.top_k(imp, n_sel)
        k_sel = ks_blk[bi, gi, idx].reshape(B, G, Q_BLOCK, n_sel * SLC_LEN, dh)
        v_sel = vs_blk[bi, gi, idx].reshape(B, G, Q_BLOCK, n_sel * SLC_LEN, dh)
        pos_sel = (idx[..., None] * SLC_LEN + jnp.arange(SLC_LEN)).reshape(B, G, Q_BLOCK, n_sel * SLC_LEN)
        ss = jnp.einsum('bqgrd,bgqkd->bgrqk', qblk, k_sel) * scale
        p_sel = masked_softmax(ss, (pos_sel <= t[None, None, :, None])[:, :, None])
        o_sel = jnp.einsum('bgrqk,bgqkd->bqgrd', p_sel, v_sel.astype(jnp.float32))
        k_win = lax.dynamic_slice_in_dim(kw_pad, s, WINDOW + Q_BLOCK, 1)
        v_win = lax.dynamic_slice_in_dim(vw_pad, s, WINDOW + Q_BLOCK, 1)
        pos_win = s - WINDOW + jnp.arange(WINDOW + Q_BLOCK)
        diff = t[:, None] - pos_win[None, :]
        win_mask = (diff >= 0) & (diff < WINDOW) & (pos_win[None, :] >= 0)
        sw = jnp.einsum('bqgrd,bkgd->bgrqk', qblk, k_win) * scale
        p_win = masked_softmax(sw, win_mask)
        o_win = jnp.einsum('bgrqk,bkgd->bqgrd', p_win, v_win.astype(jnp.float32))
        g = lax.dynamic_slice_in_dim(gates, s, Q_BLOCK, 1).reshape(B, Q_BLOCK, G, R, 3)
        o = g[..., 0:1] * o_cmp + g[..., 1:2] * o_sel + g[..., 2:3] * o_win
        return o.reshape(B, Q_BLOCK, H * dh)

    out = lax.map(block, jnp.arange(T // Q_BLOCK))
    return out.transpose(1, 0, 2, 3).reshape(B, T, H * dh)


def hgrn2_mixer(hq, hf, hi, hg, lb, o_gain):
    B, T, _ = hq.shape
    H, dk, dv, C = HGRN_HEADS, HGRN_KEY_DIM, HGRN_VAL_DIM, HGRN_CHUNK
    f32 = jnp.float32
    q = jax.nn.silu(hq.astype(f32)).reshape(B, T, H, dk)
    f = lb + (1.0 - lb) * jax.nn.sigmoid(hf.astype(f32))
    logf = jnp.log(f).reshape(B, T, H, dk)
    k = (1.0 - f).reshape(B, T, H, dk)
    v = hi.astype(f32).reshape(B, T, H, dv)
    nc = T // C

    def to_chunks(a):
        return a.reshape(B, nc, C, H, a.shape[-1]).transpose(1, 0, 3, 2, 4)

    causal = jnp.tril(jnp.ones((C, C), dtype=bool))

    def step(S, xs):
        qc, kc, vc, gc = xs
        b = jnp.cumsum(gc, axis=-2)
        b_last = b[:, :, -1:, :]
        o_inter = jnp.einsum('bhcd,bhde->bhce', qc * jnp.exp(b), S)
        decay = jnp.exp(jnp.where(causal[:, :, None], b[:, :, :, None, :] - b[:, :, None, :, :], -jnp.inf))
        attn = jnp.einsum('bhtd,bhsd,bhtsd->bhts', qc, kc, decay)
        o = o_inter + jnp.einsum('bhts,bhse->bhte', attn, vc)
        S = jnp.exp(b_last[:, :, 0, :])[..., None] * S + jnp.einsum('bhsd,bhse->bhde', kc * jnp.exp(b_last - b), vc)
        return S, o

    S0 = jnp.zeros((B, H, dk, dv), f32)
    _, o = lax.scan(step, S0, (to_chunks(q), to_chunks(k), to_chunks(v), to_chunks(logf)))
    o = o.transpose(1, 0, 3, 2, 4).reshape(B, T, H, dv)
    o = rms_norm(o, o_gain) * jax.nn.silu(hg.astype(f32)).reshape(B, T, H, dv)
    return o.reshape(B, T, H * dv)


def setup_inputs(seed: int = 0) -> dict:
    key = jax.random.key(seed)
    k = jax.random.split(key, 17)
    f32 = jnp.float32
    dh = NSA_HEAD_DIM

    def nrm(kk, shape, scale):
        return jax.random.normal(kk, shape, f32) * scale

    return {
        'x': nrm(k[0], (BATCH, SEQ, D_MODEL), 1.0),
        'norm_mix': 1.0 + nrm(k[1], (DEPTH, D_MODEL), 0.02),
        'w_in': nrm(k[2], (DEPTH, D_MODEL, IN_WIDTH), D_MODEL ** -0.5),
        'q_norm': 1.0 + nrm(k[3], (DEPTH, dh), 0.02),
        'k_norm': 1.0 + nrm(k[4], (DEPTH, 3, dh), 0.02),
        'cmp_pos_k': nrm(k[5], (DEPTH, CMP_LEN, dh), 0.5),
        'cmp_pos_v': nrm(k[6], (DEPTH, CMP_LEN, dh), 0.5),
        'cmp_k_w1': nrm(k[7], (DEPTH, CMP_LEN * dh, CMP_HIDDEN), (CMP_LEN * dh) ** -0.5),
        'cmp_k_w2': nrm(k[8], (DEPTH, CMP_HIDDEN, dh), CMP_HIDDEN ** -0.5),
        'cmp_v_w1': nrm(k[9], (DEPTH, CMP_LEN * dh, CMP_HIDDEN), (CMP_LEN * dh) ** -0.5),
        'cmp_v_w2': nrm(k[10], (DEPTH, CMP_HIDDEN, dh), CMP_HIDDEN ** -0.5),
        'hgrn_lb_logits': nrm(k[11], (DEPTH + 1, HGRN_KEY_WIDTH), 0.5),
        'hgrn_o_norm': 1.0 + nrm(k[12], (DEPTH, HGRN_VAL_DIM), 0.02),
        'w_out': nrm(k[13], (DEPTH, MIX_WIDTH, D_MODEL), MIX_WIDTH ** -0.5),
        'norm_ffn': 1.0 + nrm(k[14], (DEPTH, D_MODEL), 0.02),
        'w_gate_up': nrm(k[15], (DEPTH, D_MODEL, 2 * D_FF), D_MODEL ** -0.5),
        'w_down': nrm(k[16], (DEPTH, D_FF, D_MODEL), D_FF ** -0.5),
    }


def reference(x, norm_mix, w_in, q_norm, k_norm, cmp_pos_k, cmp_pos_v, cmp_k_w1, cmp_k_w2, cmp_v_w1, cmp_v_w2,
              hgrn_lb_logits, hgrn_o_norm, w_out, norm_ffn, w_gate_up, w_down):
    B, T, _ = x.shape
    H, G, dh = NSA_HEADS, NSA_KV_GROUPS, NSA_HEAD_DIM
    split_points = [int(v) for v in np.cumsum(IN_SIZES)[:-1]]
    lower_bounds = jnp.cumsum(jax.nn.softmax(hgrn_lb_logits.astype(jnp.float32), axis=0), axis=0)
    for l in range(DEPTH):
        h = rms_norm(x, norm_mix[l])
        proj = h @ w_in[l]
        (q, kc_raw, vc_raw, ks, vs, kw, vw, gate_logits, hq, hf, hi, hg) = jnp.split(proj, split_points, axis=-1)
        q = rms_norm(q.reshape(B, T, H, dh), q_norm[l])
        kc = rms_norm(compress(kc_raw.reshape(B, T, G, dh), cmp_pos_k[l], cmp_k_w1[l], cmp_k_w2[l]), k_norm[l, 0])
        vc = compress(vc_raw.reshape(B, T, G, dh), cmp_pos_v[l], cmp_v_w1[l], cmp_v_w2[l])
        ks = rms_norm(ks.reshape(B, T, G, dh), k_norm[l, 1])
        kw = rms_norm(kw.reshape(B, T, G, dh), k_norm[l, 2])
        gates = jax.nn.sigmoid(gate_logits.astype(jnp.float32)).reshape(B, T, H, 3)
        o_nsa = nsa_mixer(q, kc, vc, ks, vs.reshape(B, T, G, dh), kw, vw.reshape(B, T, G, dh), gates)
        o_hgrn = hgrn2_mixer(hq, hf, hi, hg, lower_bounds[l], hgrn_o_norm[l])
        mix = jnp.concatenate([o_nsa.astype(x.dtype), o_hgrn.astype(x.dtype)], axis=-1)
        x = x + mix @ w_out[l]
        h = rms_norm(x, norm_ffn[l])
        gate, up = jnp.split(h @ w_gate_up[l], 2, axis=-1)
        x = x + (jax.nn.silu(gate) * up) @ w_down[l]
    return x
```

```python
import functools

import numpy as np
import jax
import jax.numpy as jnp
from jax import lax
from jax.experimental import pallas as pl
from jax.experimental.pallas import tpu as pltpu

F32 = jnp.float32
BF16 = jnp.bfloat16

LANE = 128

NSA_HEADS = 8
NSA_GROUPS = 2
NSA_REP = NSA_HEADS // NSA_GROUPS
HEAD_DIM = 64
CMP_LEN = 32
CMP_STRIDE = 16
CMP_HIDDEN = 128
SLC_LEN = 64
N_SELECT = 16
WINDOW = 512
Q_BLOCK = 128
FORCE_SCORE = 1e4
HGRN_HEADS = 4
HGRN_DIM = 128
HGRN_CHUNK = 64
HGRN_SUB = 16
RMS_EPS = 1e-6
NEG = -1e30

NSA_WIDTH = NSA_HEADS * HEAD_DIM
KV_WIDTH = NSA_GROUPS * HEAD_DIM
HG_WIDTH = HGRN_HEADS * HGRN_DIM
N_GATES = 3 * NSA_HEADS

KEY_CHUNK = 512
VMEM_LIMIT = 56 * 1024 * 1024


def _dot(a, b):
    return jnp.dot(a, b, preferred_element_type=F32)


def _dot_t(a, b):
    return lax.dot_general(a, b, (((1,), (1,)), ((), ())), preferred_element_type=F32)


def _split2(x):
    hi = x.astype(BF16)
    lo = (x - hi.astype(F32)).astype(BF16)
    return hi, lo


def _group_sumsq(y, bd):
    hi, lo = _split2(y * y)
    return _dot(hi, bd) + _dot(lo, bd)


def _group_rms(y, bd, gain):
    ss = _group_sumsq(y, bd)
    return y * lax.rsqrt(ss * (1.0 / HEAD_DIM) + RMS_EPS) * gain


def _in_proj_kernel(x_ref, g_ref, w_ref, bd_ref, qg_ref, ksg_ref, kwg_ref,
                    q_ref, kc_ref, vc_ref, ks_ref, vs_ref, kw_ref, vw_ref, gate_ref, ph_ref):
    x = x_ref[...]
    ms = jnp.mean(x * x, axis=-1, keepdims=True)
    h = (x * lax.rsqrt(ms + RMS_EPS) * g_ref[...]).astype(BF16)
    y = _dot(h, w_ref[...])
    bd = bd_ref[...]
    for r in range(NSA_REP):
        sl = slice(r * LANE, (r + 1) * LANE)
        q_ref[:, sl] = _group_rms(y[:, sl], bd, qg_ref[...]).astype(BF16)
    o = NSA_WIDTH
    kc_ref[...] = y[:, o:o + LANE]
    vc_ref[...] = y[:, o + LANE:o + 2 * LANE]
    ks_ref[...] = _group_rms(y[:, o + 2 * LANE:o + 3 * LANE], bd, ksg_ref[...]).astype(BF16)
    vs_ref[...] = y[:, o + 3 * LANE:o + 4 * LANE].astype(BF16)
    kw_ref[...] = _group_rms(y[:, o + 4 * LANE:o + 5 * LANE], bd, kwg_ref[...]).astype(BF16)
    vw_ref[...] = y[:, o + 5 * LANE:o + 6 * LANE].astype(BF16)
    o = NSA_WIDTH + 6 * LANE
    ph_ref[...] = y[:, o:o + 4 * HG_WIDTH]
    o = o + 4 * HG_WIDTH
    gate_ref[...] = jax.nn.sigmoid(y[:, o:o + LANE])


def _in_proj(x2, g, w, bd, qg, ksg, kwg, tm=256):
    n, d = x2.shape
    nw = w.shape[1]
    row = lambda i: (i, 0)
    const = lambda i: (0, 0)
    outs = [
        jax.ShapeDtypeStruct((n, NSA_WIDTH), BF16),
        jax.ShapeDtypeStruct((n, LANE), F32),
        jax.ShapeDtypeStruct((n, LANE), F32),
        jax.ShapeDtypeStruct((n, LANE), BF16),
        jax.ShapeDtypeStruct((n, LANE), BF16),
        jax.ShapeDtypeStruct((n, LANE), BF16),
        jax.ShapeDtypeStruct((n, LANE), BF16),
        jax.ShapeDtypeStruct((n, LANE), F32),
        jax.ShapeDtypeStruct((n, 4 * HG_WIDTH), F32),
    ]
    out_specs = [pl.BlockSpec((tm, s.shape[1]), row) for s in outs]
    return pl.pallas_call(
        _in_proj_kernel,
        grid=(n // tm,),
        in_specs=[
            pl.BlockSpec((tm, d), row),
            pl.BlockSpec((1, d), const),
            pl.BlockSpec((d, nw), const),
            pl.BlockSpec((LANE, LANE), const),
            pl.BlockSpec((1, LANE), const),
            pl.BlockSpec((1, LANE), const),
            pl.BlockSpec((1, LANE), const),
        ],
        out_specs=out_specs,
        out_shape=outs,
        compiler_params=pltpu.CompilerParams(
            dimension_semantics=("arbitrary",), vmem_limit_bytes=VMEM_LIMIT),
        name="in_proj",
    )(x2, g, w, bd, qg, ksg, kwg)


def _compress_kernel(xk_ref, xv_ref, wklo_ref, wkhi_ref, wk2_ref, pk_ref,
                     wvlo_ref, wvhi_ref, wv2_ref, pv_ref, bd_ref, kg_ref, kc_ref, vc_ref):
    ns = xk_ref.shape[1]

    def mlp(x_ref, wlo_ref, whi_ref, w2_ref, p_ref):
        xb = x_ref[0].astype(BF16)
        a = _dot(xb, wlo_ref[...])
        b = _dot(xb, whi_ref[...])
        h = a + pltpu.roll(b, ns - 1, axis=0) + p_ref[...]
        return _dot(jax.nn.gelu(h).astype(BF16), w2_ref[...])

    kc = mlp(xk_ref, wklo_ref, wkhi_ref, wk2_ref, pk_ref)
    kc_ref[0] = _group_rms(kc, bd_ref[...], kg_ref[...]).astype(BF16)
    vc_ref[0] = mlp(xv_ref, wvlo_ref, wvhi_ref, wv2_ref, pv_ref).astype(BF16)


def _compress(xk, xv, wk, wv, bd, kg):
    b, ns, wd = xk.shape
    const2 = lambda i: (0, 0)
    bat = lambda i: (i, 0, 0)
    wspecs = [pl.BlockSpec(w.shape, const2) for w in wk] + [pl.BlockSpec(w.shape, const2) for w in wv]
    return pl.pallas_call(
        _compress_kernel,
        grid=(b,),
        in_specs=[pl.BlockSpec((1, ns, wd), bat), pl.BlockSpec((1, ns, wd), bat)] + wspecs + [
            pl.BlockSpec((LANE, LANE), const2), pl.BlockSpec((1, LANE), const2)],
        out_specs=[pl.BlockSpec((1, ns, LANE), bat), pl.BlockSpec((1, ns, LANE), bat)],
        out_shape=[jax.ShapeDtypeStruct((b, ns, LANE), BF16), jax.ShapeDtypeStruct((b, ns, LANE), BF16)],
        compiler_params=pltpu.CompilerParams(
            dimension_semantics=("arbitrary",), vmem_limit_bytes=VMEM_LIMIT),
        name="compress",
    )(xk, xv, *wk, *wv, bd, kg)


def _softmax_rows(s, valid):
    s = jnp.where(valid, s, NEG)
    m = jnp.max(s, axis=-1, keepdims=True)
    e = jnp.where(valid, jnp.exp(s - m), 0.0)
    l = jnp.sum(e, axis=-1, keepdims=True)
    return e / jnp.where(l > 0.0, l, 1.0)


def _topk_bias_t(imp_t, s0):
    nj, nq = imp_t.shape
    jidx = lax.broadcasted_iota(jnp.int32, (nj, nq), 0)
    tq = s0 + lax.broadcasted_iota(jnp.int32, (nj, nq), 1)
    jcur = lax.shift_right_logical(tq, 6)
    forced = (jidx == 0) | (jidx == jcur) | (jidx == jcur - 1)
    future = jidx > jcur
    v = jnp.where(forced, FORCE_SCORE, jnp.where(future, -1.0, imp_t))
    jf = jidx.astype(F32)
    sel = jnp.zeros((nj, nq), F32)
    for _ in range(N_SELECT):
        m = jnp.max(v, axis=0, keepdims=True)
        jm = jnp.min(jnp.where(v == m, jf, float(nj)), axis=0, keepdims=True)
        hit = jf == jm
        sel = jnp.where(hit, 1.0, sel)
        v = jnp.where(hit, -2.0, v)
    return jnp.where((sel > 0.5) & jnp.logical_not(future), 0.0, NEG)


def _nsa_kernel(q_ref, ks_ref, vs_ref, kw_ref, vw_ref, kc_ref, vc_ref, gate_ref, ov_ref, ex_ref, o_ref):
    qb = pl.program_id(1)
    s0 = qb * Q_BLOCK
    nq = Q_BLOCK
    rows = NSA_REP * nq
    nc = kc_ref.shape[1]
    q = q_ref[0]
    half = lax.broadcasted_iota(jnp.int32, (nq, LANE), 1) >= HEAD_DIM
    t_row = s0 + (lax.broadcasted_iota(jnp.int32, (rows, 1), 0) & (nq - 1))
    kc = kc_ref[0]
    vc = vc_ref[0]

    def qpad(g):
        keep = half if g == 1 else jnp.logical_not(half)
        zero = jnp.zeros((nq, LANE), BF16)
        return jnp.concatenate(
            [jnp.where(keep, q[:, r * LANE:(r + 1) * LANE], zero) for r in range(NSA_REP)], axis=0)

    qp = [qpad(g) for g in range(NSA_GROUPS)]

    o_cmp = []
    bias = []
    ncol = lax.broadcasted_iota(jnp.int32, (1, nc), 1)
    cmp_valid = (ncol * CMP_STRIDE + (CMP_LEN - 1)) <= t_row
    for g in range(NSA_GROUPS):
        p = _softmax_rows(_dot_t(qp[g], kc), cmp_valid)
        o_cmp.append(_dot(p.astype(BF16), vc))
        psum = p[0:nq] + p[nq:2 * nq] + p[2 * nq:3 * nq] + p[3 * nq:4 * nq]
        hi, lo = _split2(psum)
        imp = _dot(hi, ov_ref[...]) + _dot(lo, ov_ref[...])
        bias.append(_topk_bias_t(imp.T, s0).T.astype(BF16))

    def sel_chunk(c, carry, diagonal):
        k0 = pl.multiple_of(c * KEY_CHUNK, KEY_CHUNK)
        kk = ks_ref[0, pl.ds(k0, KEY_CHUNK), :]
        vv = vs_ref[0, pl.ds(k0, KEY_CHUNK), :]
        ex = ex_ref[:, pl.ds(k0, KEY_CHUNK)]
        out = []
        for g in range(NSA_GROUPS):
            m, l, acc = carry[g]
            s = _dot_t(qp[g], kk)
            b = _dot(bias[g], ex)
            s = (s.reshape(NSA_REP, nq, KEY_CHUNK) + b[None]).reshape(rows, KEY_CHUNK)
            if diagonal:
                kpos = k0 + lax.broadcasted_iota(jnp.int32, (1, KEY_CHUNK), 1)
                s = jnp.where(kpos <= t_row, s, NEG)
            m_new = jnp.maximum(m, jnp.max(s, axis=-1, keepdims=True))
            alpha = jnp.exp(m - m_new)
            p = jnp.exp(s - m_new)
            l = alpha * l + jnp.sum(p, axis=-1, keepdims=True)
            acc = alpha * acc + _dot(p.astype(BF16), vv)
            out.append((m_new, l, acc))
        return tuple(out)

    init = tuple((jnp.full((rows, 1), NEG, F32), jnp.zeros((rows, 1), F32), jnp.zeros((rows, LANE), F32))
                 for _ in range(NSA_GROUPS))
    n_full = s0 // KEY_CHUNK
    carry = lax.fori_loop(0, n_full, lambda c, cr: sel_chunk(c, cr, False), init)
    carry = sel_chunk(n_full, carry, True)
    o_sel = [carry[g][2] / carry[g][1] for g in range(NSA_GROUPS)]

    wlen = WINDOW + Q_BLOCK
    w0 = pl.multiple_of(jnp.maximum(s0 - WINDOW, 0), Q_BLOCK)
    kwin = kw_ref[0, pl.ds(w0, wlen), :]
    vwin = vw_ref[0, pl.ds(w0, wlen), :]
    diff = t_row - (w0 + lax.broadcasted_iota(jnp.int32, (1, wlen), 1))
    win_valid = (diff >= 0) & (diff < WINDOW)
    o_win = [_dot(_softmax_rows(_dot_t(qp[g], kwin), win_valid).astype(BF16), vwin) for g in range(NSA_GROUPS)]

    gates = gate_ref[0]
    for r in range(NSA_REP):
        per_group = []
        for g in range(NSA_GROUPS):
            c0 = 3 * (g * NSA_REP + r)
            rs = slice(r * nq, (r + 1) * nq)
            per_group.append(gates[:, c0:c0 + 1] * o_cmp[g][rs]
                             + gates[:, c0 + 1:c0 + 2] * o_sel[g][rs]
                             + gates[:, c0 + 2:c0 + 3] * o_win[g][rs])
        o_ref[0, :, r * LANE:(r + 1) * LANE] = jnp.where(half, per_group[1], per_group[0]).astype(BF16)


def _nsa(q, ks, vs, kw, vw, kc, vc, gates, ov, ex):
    b, t, _ = q.shape
    nc = kc.shape[1]
    blk = lambda i, j: (i, j, 0)
    full = lambda i, j: (i, 0, 0)
    const = lambda i, j: (0, 0)
    return pl.pallas_call(
        _nsa_kernel,
        grid=(b, t // Q_BLOCK),
        in_specs=[
            pl.BlockSpec((1, Q_BLOCK, NSA_WIDTH), blk),
            pl.BlockSpec((1, t, LANE), full),
            pl.BlockSpec((1, t, LANE), full),
            pl.BlockSpec((1, t, LANE), full),
            pl.BlockSpec((1, t, LANE), full),
            pl.BlockSpec((1, nc, LANE), full),
            pl.BlockSpec((1, nc, LANE), full),
            pl.BlockSpec((1, Q_BLOCK, LANE), blk),
            pl.BlockSpec(ov.shape, const),
            pl.BlockSpec(ex.shape, const),
        ],
        out_specs=pl.BlockSpec((1, Q_BLOCK, NSA_WIDTH), blk),
        out_shape=jax.ShapeDtypeStruct((b, t, NSA_WIDTH), BF16),
        compiler_params=pltpu.CompilerParams(
            dimension_semantics=("arbitrary", "arbitrary"), vmem_limit_bytes=VMEM_LIMIT),
        name="nsa",
    )(q, ks, vs, kw, vw, kc, vc, gates, ov, ex)


def _hgrn_kernel(ph_ref, lb_ref, og_ref, o_ref, st_ref, *, chunks):
    c_len = HGRN_CHUNK
    n_sub = c_len // HGRN_SUB

    @pl.when(pl.program_id(1) == 0)
    def _():
        st_ref[...] = jnp.zeros_like(st_ref)

    lg = lb_ref[...]
    lmax = jnp.max(lg, axis=0, keepdims=True)
    le = jnp.exp(lg - lmax)
    lb_all = le[0:1] / jnp.sum(le, axis=0, keepdims=True)

    ri = lax.broadcasted_iota(jnp.int32, (c_len, c_len), 0)
    ci = lax.broadcasted_iota(jnp.int32, (c_len, c_len), 1)
    tri = (ci <= ri).astype(BF16)
    causal = ci <= ri
    rowi = lax.broadcasted_iota(jnp.int32, (c_len, HGRN_DIM), 0)

    for h in range(HGRN_HEADS):
        hs = slice(h * HGRN_DIM, (h + 1) * HGRN_DIM)
        lb = lb_all[:, hs]
        gain = og_ref[...]
        st = st_ref[h]
        for c in range(chunks):
            rs = slice(c * c_len, (c + 1) * c_len)
            hq = ph_ref[0, rs, h * HGRN_DIM:(h + 1) * HGRN_DIM]
            hf = ph_ref[0, rs, HG_WIDTH + h * HGRN_DIM:HG_WIDTH + (h + 1) * HGRN_DIM]
            hi = ph_ref[0, rs, 2 * HG_WIDTH + h * HGRN_DIM:2 * HG_WIDTH + (h + 1) * HGRN_DIM]
            hg = ph_ref[0, rs, 3 * HG_WIDTH + h * HGRN_DIM:3 * HG_WIDTH + (h + 1) * HGRN_DIM]
            qv = jax.nn.silu(hq)
            f = lb + (1.0 - lb) * jax.nn.sigmoid(hf)
            logf = jnp.log(f)
            kv = 1.0 - f
            vb = hi.astype(BF16)
            g1 = logf.astype(BF16)
            r1 = logf - g1.astype(F32)
            g2 = r1.astype(BF16)
            g3 = (r1 - g2.astype(F32)).astype(BF16)
            bcum = _dot(tri, g1) + _dot(tri, g2) + _dot(tri, g3)
            b_last = bcum[c_len - 1:c_len]
            refs = [jnp.zeros((1, HGRN_DIM), F32)] + [bcum[i * HGRN_SUB - 1:i * HGRN_SUB] for i in range(1, n_sub)]
            rfull = jnp.concatenate([jnp.broadcast_to(r, (HGRN_SUB, HGRN_DIM)) for r in refs], axis=0)
            qd = (qv * jnp.exp(bcum - rfull)).astype(BF16)
            parts = []
            for i in range(n_sub):
                e = jnp.where(rowi < (i + 1) * HGRN_SUB, refs[i] - bcum, 0.0)
                kd = (kv * jnp.exp(e)).astype(BF16)
                parts.append(_dot_t(qd[i * HGRN_SUB:(i + 1) * HGRN_SUB], kd))
            attn = jnp.where(causal, jnp.concatenate(parts, axis=0), 0.0)
            o = _dot(attn.astype(BF16), vb) + _dot_t((qv * jnp.exp(bcum)).astype(BF16), st.astype(BF16))
            kdec = (kv * jnp.exp(b_last - bcum)).astype(BF16)
            st = st * jnp.exp(b_last) + _dot(hi.T.astype(BF16), kdec)
            ms = jnp.mean(o * o, axis=-1, keepdims=True)
            o = o * lax.rsqrt(ms + RMS_EPS) * gain * jax.nn.silu(hg)
            o_ref[0, rs, hs] = o.astype(BF16)
        st_ref[h] = st


def _hgrn(ph, lb_logits, o_gain, tt=256):
    b, t, _ = ph.shape
    blk = lambda i, j: (i, j, 0)
    const = lambda i, j: (0, 0)
    return pl.pallas_call(
        functools.partial(_hgrn_kernel, chunks=tt // HGRN_CHUNK),
        grid=(b, t // tt),
        in_specs=[
            pl.BlockSpec((1, tt, 4 * HG_WIDTH), blk),
            pl.BlockSpec(lb_logits.shape, const),
            pl.BlockSpec((1, HGRN_DIM), const),
        ],
        out_specs=pl.BlockSpec((1, tt, HG_WIDTH), blk),
        out_shape=jax.ShapeDtypeStruct((b, t, HG_WIDTH), BF16),
        scratch_shapes=[pltpu.VMEM((HGRN_HEADS, HGRN_DIM, HGRN_DIM), F32)],
        compiler_params=pltpu.CompilerParams(
            dimension_semantics=("arbitrary", "arbitrary"), vmem_limit_bytes=VMEM_LIMIT),
        name="hgrn2",
    )(ph, lb_logits, o_gain)


def _ff_chunks(d_ff, width):
    return tuple((s, min(width, d_ff - s)) for s in range(0, d_ff, width))


def _out_ffn_kernel(x_ref, on_ref, oh_ref, won_ref, woh_ref, g_ref, wgu_ref, wd_ref, o_ref, *, chunks):
    d_ff = wd_ref.shape[0]
    x1 = x_ref[...] + _dot(on_ref[...], won_ref[...]) + _dot(oh_ref[...], woh_ref[...])
    ms = jnp.mean(x1 * x1, axis=-1, keepdims=True)
    h = (x1 * lax.rsqrt(ms + RMS_EPS) * g_ref[...]).astype(BF16)
    o_ref[...] = x1
    for s, n in chunks:
        gate = _dot(h, wgu_ref[:, s:s + n])
        up = _dot(h, wgu_ref[:, d_ff + s:d_ff + s + n])
        act = (jax.nn.silu(gate) * up).astype(BF16)
        o_ref[...] += _dot(act, wd_ref[s:s + n, :])


def _out_ffn(x2, o_nsa, o_hg, won, woh, g, wgu, wd, tm=256, ff_width=1024):
    n, d = x2.shape
    d_ff = wd.shape[0]
    row = lambda i: (i, 0)
    const = lambda i: (0, 0)
    return pl.pallas_call(
        functools.partial(_out_ffn_kernel, chunks=_ff_chunks(d_ff, ff_width)),
        grid=(n // tm,),
        in_specs=[
            pl.BlockSpec((tm, d), row),
            pl.BlockSpec((tm, NSA_WIDTH), row),
            pl.BlockSpec((tm, HG_WIDTH), row),
            pl.BlockSpec(won.shape, const),
            pl.BlockSpec(woh.shape, const),
            pl.BlockSpec((1, d), const),
            pl.BlockSpec(wgu.shape, const),
            pl.BlockSpec(wd.shape, const),
        ],
        out_specs=pl.BlockSpec((tm, d), row),
        out_shape=jax.ShapeDtypeStruct((n, d), F32),
        compiler_params=pltpu.CompilerParams(
            dimension_semantics=("arbitrary",), vmem_limit_bytes=VMEM_LIMIT),
        name="out_ffn",
    )(x2, o_nsa, o_hg, won, woh, g, wgu, wd)


def _head_pair_perm():
    cols = []
    for r in range(NSA_REP):
        for g in range(NSA_GROUPS):
            h = g * NSA_REP + r
            cols.extend(range(h * HEAD_DIM, (h + 1) * HEAD_DIM))
    return np.asarray(cols, dtype=np.int32)


def _expand_cmp_weights(pos, w1, w2):
    eye = jnp.eye(NSA_GROUPS, dtype=F32)
    w1r = w1.reshape(CMP_LEN, HEAD_DIM, CMP_HIDDEN)

    def lift(wpart):
        return jnp.einsum('ldc,gk->lgdkc', wpart, eye).reshape(
            CMP_STRIDE * NSA_GROUPS * HEAD_DIM, NSA_GROUPS * CMP_HIDDEN)

    wlo = lift(w1r[:CMP_STRIDE]).astype(BF16)
    whi = lift(w1r[CMP_STRIDE:]).astype(BF16)
    w2x = jnp.einsum('cd,gk->gckd', w2, eye).reshape(NSA_GROUPS * CMP_HIDDEN, NSA_GROUPS * HEAD_DIM).astype(BF16)
    pbias = jnp.tile(pos.reshape(1, CMP_LEN * HEAD_DIM) @ w1, (1, NSA_GROUPS))
    return wlo, whi, w2x, pbias


def _mixers(x, norm_mix, w_in, q_norm, k_norm, cmp_pos_k, cmp_pos_v, cmp_k_w1, cmp_k_w2, cmp_v_w1, cmp_v_w2,
            hgrn_lb_logits, hgrn_o_norm):
    b, t, d = x.shape
    depth = norm_mix.shape[0]
    assert depth == 1 and hgrn_lb_logits.shape[0] == 2
    assert t % KEY_CHUNK == 0 and t >= WINDOW + Q_BLOCK and t // SLC_LEN <= LANE
    l = 0
    perm = _head_pair_perm()

    o_q, o_kv, o_g, o_h = 0, NSA_WIDTH, NSA_WIDTH + 6 * KV_WIDTH, NSA_WIDTH + 6 * KV_WIDTH + N_GATES
    w = w_in[l]
    w_perm = jnp.concatenate([
        w[:, o_q:o_kv][:, perm], w[:, o_kv:o_g], w[:, o_h:], w[:, o_g:o_h],
        jnp.zeros((d, LANE - N_GATES), w.dtype)], axis=1).astype(BF16)

    bd = jnp.asarray(np.kron(np.eye(LANE // HEAD_DIM), np.ones((HEAD_DIM, HEAD_DIM))), BF16)
    tile2 = lambda v: jnp.tile(v.reshape(1, HEAD_DIM), (1, LANE // HEAD_DIM)).astype(F32)
    qg = tile2(q_norm[l]) * (HEAD_DIM ** -0.5)

    x2 = x.reshape(b * t, d)
    q, kc_raw, vc_raw, ks, vs, kw, vw, gates, ph = _in_proj(
        x2, norm_mix[l].reshape(1, d), w_perm, bd, qg, tile2(k_norm[l, 1]), tile2(k_norm[l, 2]))

    ns = t // CMP_STRIDE
    seg_w = CMP_STRIDE * KV_WIDTH
    kc, vc = _compress(
        kc_raw.reshape(b, ns, seg_w), vc_raw.reshape(b, ns, seg_w),
        _expand_cmp_weights(cmp_pos_k[l], cmp_k_w1[l], cmp_k_w2[l]),
        _expand_cmp_weights(cmp_pos_v[l], cmp_v_w1[l], cmp_v_w2[l]),
        bd, tile2(k_norm[l, 0]))

    cs = np.arange(ns)[:, None] * CMP_STRIDE
    ss = np.arange(LANE)[None, :] * SLC_LEN
    ov = jnp.asarray(((cs < ss + SLC_LEN) & (cs + CMP_LEN > ss)).astype(np.float32), BF16)
    ex = jnp.asarray((np.arange(t)[None, :] // SLC_LEN == np.arange(LANE)[:, None]).astype(np.float32), BF16)

    r3 = lambda a: a.reshape(b, t, a.shape[-1])
    o_nsa = _nsa(r3(q), r3(ks), r3(vs), r3(kw), r3(vw), kc, vc, r3(gates), ov, ex)
    o_hg = _hgrn(r3(ph), hgrn_lb_logits, hgrn_o_norm[l].reshape(1, HGRN_DIM))
    return o_nsa, o_hg


def kernel(x, norm_mix, w_in, q_norm, k_norm, cmp_pos_k, cmp_pos_v, cmp_k_w1, cmp_k_w2, cmp_v_w1, cmp_v_w2,
           hgrn_lb_logits, hgrn_o_norm, w_out, norm_ffn, w_gate_up, w_down):
    b, t, d = x.shape
    l = 0
    perm = _head_pair_perm()
    x2 = x.reshape(b * t, d)
    o_nsa, o_hg = _mixers(x, norm_mix, w_in, q_norm, k_norm, cmp_pos_k, cmp_pos_v, cmp_k_w1, cmp_k_w2,
                          cmp_v_w1, cmp_v_w2, hgrn_lb_logits, hgrn_o_norm)
    wo = w_out[l]
    out = _out_ffn(
        x2, o_nsa.reshape(b * t, NSA_WIDTH), o_hg.reshape(b * t, HG_WIDTH),
        wo[:NSA_WIDTH][perm].astype(BF16), wo[NSA_WIDTH:].astype(BF16),
        norm_ffn[l].reshape(1, d), w_gate_up[l].astype(BF16), w_down[l].astype(BF16))
    return out.reshape(b, t, d)
```

```python
import functools
import math

import numpy as np
import jax
import jax.numpy as jnp
from jax import lax
from jax.experimental import pallas as pl
from jax.experimental.pallas import tpu as pltpu

F32 = jnp.float32
BF16 = jnp.bfloat16

LANE = 128
MXU_WIDTH = 256

NSA_HEADS = 8
NSA_GROUPS = 2
NSA_REP = NSA_HEADS // NSA_GROUPS
HEAD_DIM = 64
CMP_LEN = 32
CMP_STRIDE = 16
CMP_HIDDEN = 128
SLC_LEN = 64
N_SELECT = 16
WINDOW = 512
Q_BLOCK = 128
FORCE_SCORE = 1e4
HGRN_HEADS = 4
HGRN_DIM = 128
HGRN_CHUNK = 64
HGRN_SUB = 16
RMS_EPS = 1e-6
NEG = -1e30

NSA_WIDTH = NSA_HEADS * HEAD_DIM
KV_WIDTH = NSA_GROUPS * HEAD_DIM
HG_WIDTH = HGRN_HEADS * HGRN_DIM
N_GATES = 3 * NSA_HEADS

KEY_TILE = 128
KEY_UNIT = 2 * KEY_TILE
N_PAIRS = NSA_HEADS // 2
VMEM_LIMIT = 56 * 1024 * 1024


def _dot(a, b):
    return jnp.dot(a, b, preferred_element_type=F32)


def _dot_t(a, b):
    return lax.dot_general(a, b, (((1,), (1,)), ((), ())), preferred_element_type=F32)


def _split2(x):
    hi = x.astype(BF16)
    lo = (x - hi.astype(F32)).astype(BF16)
    return hi, lo


def _group_sumsq(y, bd):
    hi, lo = _split2(y * y)
    return _dot(hi, bd) + _dot(lo, bd)


def _group_rms(y, bd, gain):
    ss = _group_sumsq(y, bd)
    return y * lax.rsqrt(ss * (1.0 / HEAD_DIM) + RMS_EPS) * gain


def _in_proj_kernel(x_ref, g_ref, w_ref, bd_ref, qg_ref, ksg_ref, kwg_ref,
                    qt_ref, kc_ref, vc_ref, ksa_ref, vst_ref, kw_ref, vwt_ref, gt_ref, ph_ref):
    tm = x_ref.shape[1]
    x = x_ref[0]
    ms = jnp.mean(x * x, axis=-1, keepdims=True)
    h = (x * lax.rsqrt(ms + RMS_EPS) * g_ref[...]).astype(BF16)
    y = _dot(h, w_ref[...])
    bd = bd_ref[...]
    for r in range(NSA_WIDTH // LANE):
        sl = slice(r * LANE, (r + 1) * LANE)
        qt_ref[0, sl, :] = _group_rms(y[:, sl], bd, qg_ref[...]).T.astype(BF16)
    o = NSA_WIDTH
    kc_ref[0] = y[:, o:o + LANE]
    vc_ref[0] = y[:, o + LANE:o + 2 * LANE]
    ksa_ref[0, :, 0:LANE] = _group_rms(y[:, o + 2 * LANE:o + 3 * LANE], bd, ksg_ref[...]).astype(BF16)
    key = pl.program_id(1) * tm + lax.broadcasted_iota(jnp.int32, (tm, LANE), 0)
    blk = lax.broadcasted_iota(jnp.int32, (tm, LANE), 1)
    ksa_ref[0, :, LANE:2 * LANE] = jnp.where(lax.shift_right_logical(key, 6) == blk, 1.0, 0.0).astype(BF16)
    vst_ref[0] = y[:, o + 3 * LANE:o + 4 * LANE].T.astype(BF16)
    kw_ref[0] = _group_rms(y[:, o + 4 * LANE:o + 5 * LANE], bd, kwg_ref[...]).astype(BF16)
    vwt_ref[0] = y[:, o + 5 * LANE:o + 6 * LANE].T.astype(BF16)
    o = NSA_WIDTH + 6 * LANE
    ph_ref[0] = y[:, o:o + 4 * HG_WIDTH]
    o = o + 4 * HG_WIDTH
    gt_ref[0] = jax.nn.sigmoid(y[:, o:o + LANE]).T


def _in_proj(x, g, w, bd, qg, ksg, kwg, tm=256):
    b, t, d = x.shape
    nw = w.shape[1]
    rows = lambda i, j: (i, j, 0)
    cols = lambda i, j: (i, 0, j)
    const = lambda i, j: (0, 0)
    outs = [
        (jax.ShapeDtypeStruct((b, NSA_WIDTH, t), BF16), pl.BlockSpec((1, NSA_WIDTH, tm), cols)),
        (jax.ShapeDtypeStruct((b, t, LANE), F32), pl.BlockSpec((1, tm, LANE), rows)),
        (jax.ShapeDtypeStruct((b, t, LANE), F32), pl.BlockSpec((1, tm, LANE), rows)),
        (jax.ShapeDtypeStruct((b, t, 2 * LANE), BF16), pl.BlockSpec((1, tm, 2 * LANE), rows)),
        (jax.ShapeDtypeStruct((b, LANE, t), BF16), pl.BlockSpec((1, LANE, tm), cols)),
        (jax.ShapeDtypeStruct((b, t, LANE), BF16), pl.BlockSpec((1, tm, LANE), rows)),
        (jax.ShapeDtypeStruct((b, LANE, t), BF16), pl.BlockSpec((1, LANE, tm), cols)),
        (jax.ShapeDtypeStruct((b, LANE, t), F32), pl.BlockSpec((1, LANE, tm), cols)),
        (jax.ShapeDtypeStruct((b, t, 4 * HG_WIDTH), F32), pl.BlockSpec((1, tm, 4 * HG_WIDTH), rows)),
    ]
    return pl.pallas_call(
        _in_proj_kernel,
        grid=(b, t // tm),
        in_specs=[
            pl.BlockSpec((1, tm, d), rows),
            pl.BlockSpec((1, d), const),
            pl.BlockSpec((d, nw), const),
            pl.BlockSpec((LANE, LANE), const),
            pl.BlockSpec((1, LANE), const),
            pl.BlockSpec((1, LANE), const),
            pl.BlockSpec((1, LANE), const),
        ],
        out_specs=[s for _, s in outs],
        out_shape=[s for s, _ in outs],
        compiler_params=pltpu.CompilerParams(
            dimension_semantics=("arbitrary", "arbitrary"), vmem_limit_bytes=VMEM_LIMIT),
        name="in_proj",
    )(x, g, w, bd, qg, ksg, kwg)


def _compress_kernel(xk_ref, xv_ref, wklo_ref, wkhi_ref, wk2_ref, pk_ref,
                     wvlo_ref, wvhi_ref, wv2_ref, pv_ref, bd_ref, kg_ref, kc_ref, vct_ref):
    ns = xk_ref.shape[1]

    def mlp(x_ref, wlo_ref, whi_ref, w2_ref, p_ref):
        xb = x_ref[0].astype(BF16)
        a = _dot(xb, wlo_ref[...])
        b = _dot(xb, whi_ref[...])
        h = a + pltpu.roll(b, ns - 1, axis=0) + p_ref[...]
        return _dot(jax.nn.gelu(h).astype(BF16), w2_ref[...])

    kc = mlp(xk_ref, wklo_ref, wkhi_ref, wk2_ref, pk_ref)
    kc_ref[0] = _group_rms(kc, bd_ref[...], kg_ref[...]).astype(BF16)
    vct_ref[0] = mlp(xv_ref, wvlo_ref, wvhi_ref, wv2_ref, pv_ref).T.astype(BF16)


def _compress(xk, xv, wk, wv, bd, kg):
    b, ns, wd = xk.shape
    const2 = lambda i: (0, 0)
    bat = lambda i: (i, 0, 0)
    wspecs = [pl.BlockSpec(w.shape, const2) for w in wk] + [pl.BlockSpec(w.shape, const2) for w in wv]
    return pl.pallas_call(
        _compress_kernel,
        grid=(b,),
        in_specs=[pl.BlockSpec((1, ns, wd), bat), pl.BlockSpec((1, ns, wd), bat)] + wspecs + [
            pl.BlockSpec((LANE, LANE), const2), pl.BlockSpec((1, LANE), const2)],
        out_specs=[pl.BlockSpec((1, ns, LANE), bat), pl.BlockSpec((1, LANE, ns), bat)],
        out_shape=[jax.ShapeDtypeStruct((b, ns, LANE), BF16), jax.ShapeDtypeStruct((b, LANE, ns), BF16)],
        compiler_params=pltpu.CompilerParams(
            dimension_semantics=("arbitrary",), vmem_limit_bytes=VMEM_LIMIT),
        name="compress",
    )(xk, xv, *wk, *wv, bd, kg)


def _topk_bias_t(imp_t, s0):
    nj, nq = imp_t.shape
    jidx = lax.broadcasted_iota(jnp.int32, (nj, nq), 0)
    tq = s0 + lax.broadcasted_iota(jnp.int32, (nj, nq), 1)
    jcur = lax.shift_right_logical(tq, 6)
    forced = (jidx == 0) | (jidx == jcur) | (jidx == jcur - 1)
    future = jidx > jcur
    v = jnp.where(forced, FORCE_SCORE, jnp.where(future, -1.0, imp_t))
    jf = jidx.astype(F32)
    sel = jnp.zeros((nj, nq), F32)
    for _ in range(N_SELECT):
        m = jnp.max(v, axis=0, keepdims=True)
        jm = jnp.min(jnp.where(v == m, jf, float(nj)), axis=0, keepdims=True)
        hit = jf == jm
        sel = jnp.where(hit, 1.0, sel)
        v = jnp.where(hit, -2.0, v)
    return jnp.where((sel > 0.5) & jnp.logical_not(future), 0.0, NEG)


def _nsa_kernel(qt_ref, ksa_ref, vst_ref, kw_ref, vwt_ref, kc_ref, vct_ref, gt_ref, ovt_ref, o_ref,
                w_ref, sc_ref, sw_ref, ss_ref, e_ref, lo_ref, m_ref, l_ref, acc_ref, out_ref):
    qb = pl.program_id(1)
    s0 = qb * Q_BLOCK
    nc = kc_ref.shape[1]
    two = 2 * Q_BLOCK
    t_lane = s0 + (lax.broadcasted_iota(jnp.int32, (1, two), 1) & (Q_BLOCK - 1))
    row_i = lax.broadcasted_iota(jnp.int32, (KEY_TILE, two), 0)
    gt = gt_ref[0]

    zero = jnp.zeros((HEAD_DIM, Q_BLOCK), BF16)
    for p in range(N_PAIRS):
        g = (2 * p) // NSA_REP
        halves = []
        for h in (2 * p, 2 * p + 1):
            qh = qt_ref[0, h * HEAD_DIM:(h + 1) * HEAD_DIM, :]
            halves.append(jnp.concatenate([qh, zero] if g == 0 else [zero, qh], axis=0))
        w_ref[p, 0:LANE, :] = jnp.concatenate(halves, axis=1)

    def softmax_tiles(p, load, n_tiles, valid_fn, m0, with_lo=False):
        def masked(c):
            s = load(c)
            valid = None if valid_fn is None else valid_fn(c)
            return s if valid is None else jnp.where(valid, s, NEG)

        m = m0
        for c in range(n_tiles):
            m = jnp.maximum(m, jnp.max(masked(c), axis=0, keepdims=True))
        l = jnp.zeros((1, two), F32)
        for c in range(n_tiles):
            e = jnp.exp2(masked(c) - m)
            l = l + jnp.sum(e, axis=0, keepdims=True)
            eb = e.astype(BF16)
            e_ref[p, c * KEY_TILE:(c + 1) * KEY_TILE, :] = eb
            if with_lo:
                lo_ref[p, c * KEY_TILE:(c + 1) * KEY_TILE, :] = (e - eb.astype(F32)).astype(BF16)
        return m, l

    def emit(p, branch, o_t, inv, first):
        for hh in range(2):
            h = 2 * p + hh
            ls = slice(hh * Q_BLOCK, (hh + 1) * Q_BLOCK)
            o = o_t[:, ls] * (inv[:, ls] * gt[3 * h + branch:3 * h + branch + 1, :])
            rs = slice(h * HEAD_DIM, (h + 1) * HEAD_DIM)
            if first:
                out_ref[rs, :] = o
            else:
                out_ref[rs, :] += o

    neg_row = jnp.full((1, two), NEG, F32)

    kc = kc_ref[0]
    for p in range(N_PAIRS):
        sc_ref[p] = _dot(kc, w_ref[p, 0:LANE, :])
    w0 = pl.multiple_of(jnp.maximum(s0 - WINDOW, 0), KEY_TILE)
    wlen = WINDOW + Q_BLOCK
    kwin = kw_ref[0, pl.ds(w0, wlen), :]
    for p in range(N_PAIRS):
        sw_ref[p] = _dot(kwin, w_ref[p, 0:LANE, :])

    imp_t = [jnp.zeros((LANE, Q_BLOCK), F32) for _ in range(NSA_GROUPS)]
    for p in range(N_PAIRS):
        g = (2 * p) // NSA_REP
        m, l = softmax_tiles(
            p, lambda c: sc_ref[p, c * KEY_TILE:(c + 1) * KEY_TILE, :], nc // KEY_TILE,
            lambda c: ((c * KEY_TILE + row_i) * CMP_STRIDE + (CMP_LEN - 1)) <= t_lane, neg_row, with_lo=True)
        inv = jnp.where(m > 0.5 * NEG, 1.0 / jnp.where(l > 0.0, l, 1.0), 0.0)
        emit(p, 0, _dot(vct_ref[0, g * HEAD_DIM:(g + 1) * HEAD_DIM, :], e_ref[p, 0:nc, :]), inv, True)
        imp_p = (_dot(ovt_ref[...], e_ref[p, 0:nc, :]) + _dot(ovt_ref[...], lo_ref[p, 0:nc, :])) * inv
        imp_t[g] = imp_t[g] + imp_p[:, 0:Q_BLOCK] + imp_p[:, Q_BLOCK:two]
    for g in range(NSA_GROUPS):
        bias_t = _topk_bias_t(imp_t[g], s0).astype(BF16)
        for p in range(g * 2, g * 2 + 2):
            w_ref[p, LANE:2 * LANE, :] = jnp.concatenate([bias_t, bias_t], axis=1)

    m_ref[...] = jnp.full(m_ref.shape, NEG, F32)
    l_ref[...] = jnp.zeros(l_ref.shape, F32)
    acc_ref[...] = jnp.zeros(acc_ref.shape, F32)
    tiles_per_unit = KEY_UNIT // KEY_TILE

    def sel_scores(u, buf):
        k0 = pl.multiple_of(u * KEY_UNIT, KEY_UNIT)
        ku = ksa_ref[0, pl.ds(k0, KEY_UNIT), :]
        for p in range(N_PAIRS):
            ss_ref[buf, p] = _dot(ku, w_ref[p])

    def sel_update(u, buf, causal):
        k0 = pl.multiple_of(u * KEY_UNIT, KEY_UNIT)
        valid_fn = (lambda c: (k0 + c * KEY_TILE + row_i) <= t_lane) if causal else None
        for p in range(N_PAIRS):
            g = (2 * p) // NSA_REP
            m_old = m_ref[p]
            m, l = softmax_tiles(
                p, lambda c: ss_ref[buf, p, c * KEY_TILE:(c + 1) * KEY_TILE, :], tiles_per_unit, valid_fn, m_old)
            alpha = jnp.exp2(m_old - m)
            m_ref[p] = m
            l_ref[p] = alpha * l_ref[p] + l
            vt = vst_ref[0, g * HEAD_DIM:(g + 1) * HEAD_DIM, pl.ds(k0, KEY_UNIT)]
            acc_ref[p] = alpha * acc_ref[p] + _dot(vt, e_ref[p, 0:KEY_UNIT, :])

    n_before = qb // 2
    n_trips = n_before // 2
    sel_scores(0, 0)

    def body(i, carry):
        u = 2 * i
        sel_scores(u + 1, 1)
        sel_update(u, 0, False)
        sel_scores(u + 2, 0)
        sel_update(u + 1, 1, False)
        return carry

    lax.fori_loop(0, n_trips, body, 0)
    rest = 2 * n_trips

    @pl.when(n_before > rest)
    def _():
        sel_scores(rest + 1, 1)
        sel_update(rest, 0, False)
        sel_update(rest + 1, 1, True)

    @pl.when(n_before == rest)
    def _():
        sel_update(rest, 0, True)

    for p in range(N_PAIRS):
        emit(p, 1, acc_ref[p], 1.0 / l_ref[p], False)

    def win_valid(c):
        diff = t_lane - (w0 + c * KEY_TILE + row_i)
        return (diff >= 0) & (diff < WINDOW)

    for p in range(N_PAIRS):
        g = (2 * p) // NSA_REP
        m, l = softmax_tiles(
            p, lambda c: sw_ref[p, c * KEY_TILE:(c + 1) * KEY_TILE, :], wlen // KEY_TILE, win_valid, neg_row)
        vt = vwt_ref[0, g * HEAD_DIM:(g + 1) * HEAD_DIM, pl.ds(w0, wlen)]
        emit(p, 2, _dot(vt, e_ref[p, 0:wlen, :]), 1.0 / l, False)

    o_ref[0] = out_ref[...].T.astype(BF16)


def _nsa(qt, ksa, vst, kw, vwt, kc, vct, gt, ovt):
    b, _, t = qt.shape
    nc = kc.shape[1]
    qcol = lambda i, j: (i, 0, j)
    full = lambda i, j: (i, 0, 0)
    const = lambda i, j: (0, 0)
    return pl.pallas_call(
        _nsa_kernel,
        grid=(b, t // Q_BLOCK),
        in_specs=[
            pl.BlockSpec((1, NSA_WIDTH, Q_BLOCK), qcol),
            pl.BlockSpec((1, t, 2 * LANE), full),
            pl.BlockSpec((1, LANE, t), full),
            pl.BlockSpec((1, t, LANE), full),
            pl.BlockSpec((1, LANE, t), full),
            pl.BlockSpec((1, nc, LANE), full),
            pl.BlockSpec((1, LANE, nc), full),
            pl.BlockSpec((1, LANE, Q_BLOCK), qcol),
            pl.BlockSpec(ovt.shape, const),
        ],
        out_specs=pl.BlockSpec((1, Q_BLOCK, NSA_WIDTH), lambda i, j: (i, j, 0)),
        out_shape=jax.ShapeDtypeStruct((b, t, NSA_WIDTH), BF16),
        scratch_shapes=[
            pltpu.VMEM((N_PAIRS, 2 * LANE, 2 * Q_BLOCK), BF16),
            pltpu.VMEM((N_PAIRS, nc, 2 * Q_BLOCK), F32),
            pltpu.VMEM((N_PAIRS, WINDOW + Q_BLOCK, 2 * Q_BLOCK), F32),
            pltpu.VMEM((2, N_PAIRS, KEY_UNIT, 2 * Q_BLOCK), F32),
            pltpu.VMEM((N_PAIRS, max(nc, WINDOW + Q_BLOCK), 2 * Q_BLOCK), BF16),
            pltpu.VMEM((N_PAIRS, nc, 2 * Q_BLOCK), BF16),
            pltpu.VMEM((N_PAIRS, 1, 2 * Q_BLOCK), F32),
            pltpu.VMEM((N_PAIRS, 1, 2 * Q_BLOCK), F32),
            pltpu.VMEM((N_PAIRS, HEAD_DIM, 2 * Q_BLOCK), F32),
            pltpu.VMEM((NSA_WIDTH, Q_BLOCK), F32),
        ],
        compiler_params=pltpu.CompilerParams(
            dimension_semantics=("arbitrary", "arbitrary"), vmem_limit_bytes=VMEM_LIMIT),
        name="nsa",
    )(qt, ksa, vst, kw, vwt, kc, vct, gt, ovt)


def _hgrn_kernel(ph_ref, lb_ref, og_ref, o_ref, st_ref, *, chunks):
    c_len = HGRN_CHUNK
    n_sub = c_len // HGRN_SUB

    @pl.when(pl.program_id(1) == 0)
    def _():
        st_ref[...] = jnp.zeros_like(st_ref)

    lg = lb_ref[...]
    lmax = jnp.max(lg, axis=0, keepdims=True)
    le = jnp.exp(lg - lmax)
    lb_all = le[0:1] / jnp.sum(le, axis=0, keepdims=True)

    ri = lax.broadcasted_iota(jnp.int32, (c_len, c_len), 0)
    ci = lax.broadcasted_iota(jnp.int32, (c_len, c_len), 1)
    tri = (ci <= ri).astype(BF16)
    causal = ci <= ri
    rowi = lax.broadcasted_iota(jnp.int32, (c_len, HGRN_DIM), 0)

    for h in range(HGRN_HEADS):
        hs = slice(h * HGRN_DIM, (h + 1) * HGRN_DIM)
        lb = lb_all[:, hs]
        gain = og_ref[...]
        st = st_ref[h]
        for c in range(chunks):
            rs = slice(c * c_len, (c + 1) * c_len)
            hq = ph_ref[0, rs, h * HGRN_DIM:(h + 1) * HGRN_DIM]
            hf = ph_ref[0, rs, HG_WIDTH + h * HGRN_DIM:HG_WIDTH + (h + 1) * HGRN_DIM]
            hi = ph_ref[0, rs, 2 * HG_WIDTH + h * HGRN_DIM:2 * HG_WIDTH + (h + 1) * HGRN_DIM]
            hg = ph_ref[0, rs, 3 * HG_WIDTH + h * HGRN_DIM:3 * HG_WIDTH + (h + 1) * HGRN_DIM]
            qv = jax.nn.silu(hq)
            f = lb + (1.0 - lb) * jax.nn.sigmoid(hf)
            logf = jnp.log(f)
            kv = 1.0 - f
            vb = hi.astype(BF16)
            g1 = logf.astype(BF16)
            r1 = logf - g1.astype(F32)
            g2 = r1.astype(BF16)
            g3 = (r1 - g2.astype(F32)).astype(BF16)
            bcum = _dot(tri, g1) + _dot(tri, g2) + _dot(tri, g3)
            b_last = bcum[c_len - 1:c_len]
            refs = [jnp.zeros((1, HGRN_DIM), F32)] + [bcum[i * HGRN_SUB - 1:i * HGRN_SUB] for i in range(1, n_sub)]
            rfull = jnp.concatenate([jnp.broadcast_to(r, (HGRN_SUB, HGRN_DIM)) for r in refs], axis=0)
            qd = (qv * jnp.exp(bcum - rfull)).astype(BF16)
            parts = []
            for i in range(n_sub):
                e = jnp.where(rowi < (i + 1) * HGRN_SUB, refs[i] - bcum, 0.0)
                kd = (kv * jnp.exp(e)).astype(BF16)
                parts.append(_dot_t(qd[i * HGRN_SUB:(i + 1) * HGRN_SUB], kd))
            attn = jnp.where(causal, jnp.concatenate(parts, axis=0), 0.0)
            o = _dot(attn.astype(BF16), vb) + _dot_t((qv * jnp.exp(bcum)).astype(BF16), st.astype(BF16))
            kdec = (kv * jnp.exp(b_last - bcum)).astype(BF16)
            st = st * jnp.exp(b_last) + _dot(hi.T.astype(BF16), kdec)
            ms = jnp.mean(o * o, axis=-1, keepdims=True)
            o = o * lax.rsqrt(ms + RMS_EPS) * gain * jax.nn.silu(hg)
            o_ref[0, rs, hs] = o.astype(BF16)
        st_ref[h] = st


def _hgrn(ph, lb_logits, o_gain, tt=256):
    b, t, _ = ph.shape
    blk = lambda i, j: (i, j, 0)
    const = lambda i, j: (0, 0)
    return pl.pallas_call(
        functools.partial(_hgrn_kernel, chunks=tt // HGRN_CHUNK),
        grid=(b, t // tt),
        in_specs=[
            pl.BlockSpec((1, tt, 4 * HG_WIDTH), blk),
            pl.BlockSpec(lb_logits.shape, const),
            pl.BlockSpec((1, HGRN_DIM), const),
        ],
        out_specs=pl.BlockSpec((1, tt, HG_WIDTH), blk),
        out_shape=jax.ShapeDtypeStruct((b, t, HG_WIDTH), BF16),
        scratch_shapes=[pltpu.VMEM((HGRN_HEADS, HGRN_DIM, HGRN_DIM), F32)],
        compiler_params=pltpu.CompilerParams(
            dimension_semantics=("arbitrary", "arbitrary"), vmem_limit_bytes=VMEM_LIMIT),
        name="hgrn2",
    )(ph, lb_logits, o_gain)


def _ff_chunks(d_ff, width):
    return tuple((s, min(width, d_ff - s)) for s in range(0, d_ff, width))


def _out_ffn_kernel(x_ref, on_ref, oh_ref, won_ref, woh_ref, g_ref, wgu_ref, wd_ref, o_ref, *, chunks):
    d_ff = wd_ref.shape[0]
    x1 = x_ref[...] + _dot(on_ref[...], won_ref[...]) + _dot(oh_ref[...], woh_ref[...])
    ms = jnp.mean(x1 * x1, axis=-1, keepdims=True)
    h = (x1 * lax.rsqrt(ms + RMS_EPS) * g_ref[...]).astype(BF16)
    o_ref[...] = x1
    for s, n in chunks:
        gate = _dot(h, wgu_ref[:, s:s + n])
        up = _dot(h, wgu_ref[:, d_ff + s:d_ff + s + n])
        act = (jax.nn.silu(gate) * up).astype(BF16)
        o_ref[...] += _dot(act, wd_ref[s:s + n, :])


def _out_ffn(x2, o_nsa, o_hg, won, woh, g, wgu, wd, tm=256, ff_width=4 * MXU_WIDTH):
    n, d = x2.shape
    d_ff = wd.shape[0]
    row = lambda i: (i, 0)
    const = lambda i: (0, 0)
    return pl.pallas_call(
        functools.partial(_out_ffn_kernel, chunks=_ff_chunks(d_ff, ff_width)),
        grid=(n // tm,),
        in_specs=[
            pl.BlockSpec((tm, d), row),
            pl.BlockSpec((tm, NSA_WIDTH), row),
            pl.BlockSpec((tm, HG_WIDTH), row),
            pl.BlockSpec(won.shape, const),
            pl.BlockSpec(woh.shape, const),
            pl.BlockSpec((1, d), const),
            pl.BlockSpec(wgu.shape, const),
            pl.BlockSpec(wd.shape, const),
        ],
        out_specs=pl.BlockSpec((tm, d), row),
        out_shape=jax.ShapeDtypeStruct((n, d), F32),
        compiler_params=pltpu.CompilerParams(
            dimension_semantics=("arbitrary",), vmem_limit_bytes=VMEM_LIMIT),
        name="out_ffn",
    )(x2, o_nsa, o_hg, won, woh, g, wgu, wd)


def _expand_cmp_weights(pos, w1, w2):
    eye = jnp.eye(NSA_GROUPS, dtype=F32)
    w1r = w1.reshape(CMP_LEN, HEAD_DIM, CMP_HIDDEN)

    def lift(wpart):
        return jnp.einsum('ldc,gk->lgdkc', wpart, eye).reshape(
            CMP_STRIDE * NSA_GROUPS * HEAD_DIM, NSA_GROUPS * CMP_HIDDEN)

    wlo = lift(w1r[:CMP_STRIDE]).astype(BF16)
    whi = lift(w1r[CMP_STRIDE:]).astype(BF16)
    w2x = jnp.einsum('cd,gk->gckd', w2, eye).reshape(NSA_GROUPS * CMP_HIDDEN, NSA_GROUPS * HEAD_DIM).astype(BF16)
    pbias = jnp.tile(pos.reshape(1, CMP_LEN * HEAD_DIM) @ w1, (1, NSA_GROUPS))
    return wlo, whi, w2x, pbias


def _mixers(x, norm_mix, w_in, q_norm, k_norm, cmp_pos_k, cmp_pos_v, cmp_k_w1, cmp_k_w2, cmp_v_w1, cmp_v_w2,
            hgrn_lb_logits, hgrn_o_norm):
    b, t, d = x.shape
    depth = norm_mix.shape[0]
    assert depth == 1 and hgrn_lb_logits.shape[0] == 2
    assert t % KEY_UNIT == 0 and t >= WINDOW + Q_BLOCK and t // SLC_LEN <= LANE
    assert (t // CMP_STRIDE) % KEY_TILE == 0
    l = 0

    o_g, o_h = NSA_WIDTH + 6 * KV_WIDTH, NSA_WIDTH + 6 * KV_WIDTH + N_GATES
    w = w_in[l]
    w_perm = jnp.concatenate([
        w[:, :o_g], w[:, o_h:], w[:, o_g:o_h], jnp.zeros((d, LANE - N_GATES), w.dtype)], axis=1).astype(BF16)

    bd = jnp.asarray(np.kron(np.eye(LANE // HEAD_DIM), np.ones((HEAD_DIM, HEAD_DIM))), BF16)
    tile2 = lambda v: jnp.tile(v.reshape(1, HEAD_DIM), (1, LANE // HEAD_DIM)).astype(F32)
    qg = tile2(q_norm[l]) * (HEAD_DIM ** -0.5 * math.log2(math.e))

    qt, kc_raw, vc_raw, ksa, vst, kw, vwt, gt, ph = _in_proj(
        x, norm_mix[l].reshape(1, d), w_perm, bd, qg, tile2(k_norm[l, 1]), tile2(k_norm[l, 2]))

    ns = t // CMP_STRIDE
    seg_w = CMP_STRIDE * KV_WIDTH
    kc, vct = _compress(
        kc_raw.reshape(b, ns, seg_w), vc_raw.reshape(b, ns, seg_w),
        _expand_cmp_weights(cmp_pos_k[l], cmp_k_w1[l], cmp_k_w2[l]),
        _expand_cmp_weights(cmp_pos_v[l], cmp_v_w1[l], cmp_v_w2[l]),
        bd, tile2(k_norm[l, 0]))

    cs = np.arange(ns)[None, :] * CMP_STRIDE
    ss = np.arange(LANE)[:, None] * SLC_LEN
    ovt = jnp.asarray(((cs < ss + SLC_LEN) & (cs + CMP_LEN > ss)).astype(np.float32), BF16)

    o_nsa = _nsa(qt, ksa, vst, kw, vwt, kc, vct, gt, ovt)
    o_hg = _hgrn(ph, hgrn_lb_logits, hgrn_o_norm[l].reshape(1, HGRN_DIM))
    return o_nsa, o_hg


def kernel(x, norm_mix, w_in, q_norm, k_norm, cmp_pos_k, cmp_pos_v, cmp_k_w1, cmp_k_w2, cmp_v_w1, cmp_v_w2,
           hgrn_lb_logits, hgrn_o_norm, w_out, norm_ffn, w_gate_up, w_down):
    b, t, d = x.shape
    l = 0
    o_nsa, o_hg = _mixers(x, norm_mix, w_in, q_norm, k_norm, cmp_pos_k, cmp_pos_v, cmp_k_w1, cmp_k_w2,
                          cmp_v_w1, cmp_v_w2, hgrn_lb_logits, hgrn_o_norm)
    wo = w_out[l]
    out = _out_ffn(
        x.reshape(b * t, d), o_nsa.reshape(b * t, NSA_WIDTH), o_hg.reshape(b * t, HG_WIDTH),
        wo[:NSA_WIDTH].astype(BF16), wo[NSA_WIDTH:].astype(BF16),
        norm_ffn[l].reshape(1, d), w_gate_up[l].astype(BF16), w_down[l].astype(BF16))
    return out.reshape(b, t, d)
```

```python
import functools
import math

import numpy as np
import jax
import jax.numpy as jnp
from jax import lax
from jax.experimental import pallas as pl
from jax.experimental.pallas import tpu as pltpu

F32 = jnp.float32
BF16 = jnp.bfloat16

LANE = 128
MXU_WIDTH = 256

NSA_HEADS = 8
NSA_GROUPS = 2
NSA_REP = NSA_HEADS // NSA_GROUPS
HEAD_DIM = 64
CMP_LEN = 32
CMP_STRIDE = 16
CMP_HIDDEN = 128
SLC_LEN = 64
N_SELECT = 16
WINDOW = 512
Q_BLOCK = 128
FORCE_SCORE = 1e4
HGRN_HEADS = 4
HGRN_DIM = 128
HGRN_CHUNK = 64
HGRN_SUB = 16
RMS_EPS = 1e-6
NEG = -1e30

NSA_WIDTH = NSA_HEADS * HEAD_DIM
KV_WIDTH = NSA_GROUPS * HEAD_DIM
HG_WIDTH = HGRN_HEADS * HGRN_DIM
N_GATES = 3 * NSA_HEADS

KEY_TILE = 128
KEY_UNIT = 4 * KEY_TILE
N_PAIRS = NSA_HEADS // 2
V_AUG = HEAD_DIM + 16
VMEM_LIMIT = 56 * 1024 * 1024


def _dot(a, b):
    return jnp.dot(a, b, preferred_element_type=F32)


def _dot_t(a, b):
    return lax.dot_general(a, b, (((1,), (1,)), ((), ())), preferred_element_type=F32)


def _split2(x):
    hi = x.astype(BF16)
    lo = (x - hi.astype(F32)).astype(BF16)
    return hi, lo


def _group_sumsq(y, bd):
    hi, lo = _split2(y * y)
    return _dot(hi, bd) + _dot(lo, bd)


def _group_rms(y, bd, gain):
    ss = _group_sumsq(y, bd)
    return y * lax.rsqrt(ss * (1.0 / HEAD_DIM) + RMS_EPS) * gain


def _values_t_aug(v):
    vt = v.T
    ones = jnp.ones((V_AUG - HEAD_DIM, v.shape[0]), F32)
    return jnp.concatenate([vt[0:HEAD_DIM], ones, vt[HEAD_DIM:2 * HEAD_DIM], ones], axis=0).astype(BF16)


def _in_proj_kernel(x_ref, g_ref, w_ref, bd_ref, qg_ref, ksg_ref, kwg_ref,
                    qt_ref, kc_ref, vc_ref, ksa_ref, vst_ref, kw_ref, vwt_ref, gt_ref, ph_ref):
    tm = x_ref.shape[1]
    x = x_ref[0]
    ms = jnp.mean(x * x, axis=-1, keepdims=True)
    h = (x * lax.rsqrt(ms + RMS_EPS) * g_ref[...]).astype(BF16)
    y = _dot(h, w_ref[...])
    bd = bd_ref[...]
    for r in range(NSA_WIDTH // LANE):
        sl = slice(r * LANE, (r + 1) * LANE)
        qt_ref[0, sl, :] = _group_rms(y[:, sl], bd, qg_ref[...]).T.astype(BF16)
    o = NSA_WIDTH
    kc_ref[0] = y[:, o:o + LANE]
    vc_ref[0] = y[:, o + LANE:o + 2 * LANE]
    ksa_ref[0, :, 0:LANE] = _group_rms(y[:, o + 2 * LANE:o + 3 * LANE], bd, ksg_ref[...]).astype(BF16)
    key = pl.program_id(1) * tm + lax.broadcasted_iota(jnp.int32, (tm, LANE), 0)
    blk = lax.broadcasted_iota(jnp.int32, (tm, LANE), 1)
    ksa_ref[0, :, LANE:2 * LANE] = jnp.where(lax.shift_right_logical(key, 6) == blk, 1.0, 0.0).astype(BF16)
    vst_ref[0] = _values_t_aug(y[:, o + 3 * LANE:o + 4 * LANE])
    kw_ref[0] = _group_rms(y[:, o + 4 * LANE:o + 5 * LANE], bd, kwg_ref[...]).astype(BF16)
    vwt_ref[0] = _values_t_aug(y[:, o + 5 * LANE:o + 6 * LANE])
    o = NSA_WIDTH + 6 * LANE
    ph_ref[0] = y[:, o:o + 4 * HG_WIDTH]
    o = o + 4 * HG_WIDTH
    gt_ref[0] = jax.nn.sigmoid(y[:, o:o + LANE]).T


def _in_proj(x, g, w, bd, qg, ksg, kwg, tm=256):
    b, t, d = x.shape
    nw = w.shape[1]
    rows = lambda i, j: (i, j, 0)
    cols = lambda i, j: (i, 0, j)
    const = lambda i, j: (0, 0)
    outs = [
        (jax.ShapeDtypeStruct((b, NSA_WIDTH, t), BF16), pl.BlockSpec((1, NSA_WIDTH, tm), cols)),
        (jax.ShapeDtypeStruct((b, t, LANE), F32), pl.BlockSpec((1, tm, LANE), rows)),
        (jax.ShapeDtypeStruct((b, t, LANE), F32), pl.BlockSpec((1, tm, LANE), rows)),
        (jax.ShapeDtypeStruct((b, t, 2 * LANE), BF16), pl.BlockSpec((1, tm, 2 * LANE), rows)),
        (jax.ShapeDtypeStruct((b, 2 * V_AUG, t), BF16), pl.BlockSpec((1, 2 * V_AUG, tm), cols)),
        (jax.ShapeDtypeStruct((b, t, LANE), BF16), pl.BlockSpec((1, tm, LANE), rows)),
        (jax.ShapeDtypeStruct((b, 2 * V_AUG, t), BF16), pl.BlockSpec((1, 2 * V_AUG, tm), cols)),
        (jax.ShapeDtypeStruct((b, LANE, t), F32), pl.BlockSpec((1, LANE, tm), cols)),
        (jax.ShapeDtypeStruct((b, t, 4 * HG_WIDTH), F32), pl.BlockSpec((1, tm, 4 * HG_WIDTH), rows)),
    ]
    return pl.pallas_call(
        _in_proj_kernel,
        grid=(b, t // tm),
        in_specs=[
            pl.BlockSpec((1, tm, d), rows),
            pl.BlockSpec((1, d), const),
            pl.BlockSpec((d, nw), const),
            pl.BlockSpec((LANE, LANE), const),
            pl.BlockSpec((1, LANE), const),
            pl.BlockSpec((1, LANE), const),
            pl.BlockSpec((1, LANE), const),
        ],
        out_specs=[s for _, s in outs],
        out_shape=[s for s, _ in outs],
        compiler_params=pltpu.CompilerParams(
            dimension_semantics=("arbitrary", "arbitrary"), vmem_limit_bytes=VMEM_LIMIT),
        name="in_proj",
    )(x, g, w, bd, qg, ksg, kwg)


def _compress_kernel(xk_ref, xv_ref, wklo_ref, wkhi_ref, wk2_ref, pk_ref,
                     wvlo_ref, wvhi_ref, wv2_ref, pv_ref, bd_ref, kg_ref, kc_ref, vct_ref):
    ns = xk_ref.shape[1]

    def mlp(x_ref, wlo_ref, whi_ref, w2_ref, p_ref):
        xb = x_ref[0].astype(BF16)
        a = _dot(xb, wlo_ref[...])
        b = _dot(xb, whi_ref[...])
        h = a + pltpu.roll(b, ns - 1, axis=0) + p_ref[...]
        return _dot(jax.nn.gelu(h).astype(BF16), w2_ref[...])

    kc = mlp(xk_ref, wklo_ref, wkhi_ref, wk2_ref, pk_ref)
    kc_ref[0] = _group_rms(kc, bd_ref[...], kg_ref[...]).astype(BF16)
    vct_ref[0] = _values_t_aug(mlp(xv_ref, wvlo_ref, wvhi_ref, wv2_ref, pv_ref))


def _compress(xk, xv, wk, wv, bd, kg):
    b, ns, wd = xk.shape
    const2 = lambda i: (0, 0)
    bat = lambda i: (i, 0, 0)
    wspecs = [pl.BlockSpec(w.shape, const2) for w in wk] + [pl.BlockSpec(w.shape, const2) for w in wv]
    return pl.pallas_call(
        _compress_kernel,
        grid=(b,),
        in_specs=[pl.BlockSpec((1, ns, wd), bat), pl.BlockSpec((1, ns, wd), bat)] + wspecs + [
            pl.BlockSpec((LANE, LANE), const2), pl.BlockSpec((1, LANE), const2)],
        out_specs=[pl.BlockSpec((1, ns, LANE), bat), pl.BlockSpec((1, 2 * V_AUG, ns), bat)],
        out_shape=[jax.ShapeDtypeStruct((b, ns, LANE), BF16), jax.ShapeDtypeStruct((b, 2 * V_AUG, ns), BF16)],
        compiler_params=pltpu.CompilerParams(
            dimension_semantics=("arbitrary",), vmem_limit_bytes=VMEM_LIMIT),
        name="compress",
    )(xk, xv, *wk, *wv, bd, kg)


def _topk_bias_t(imp_t, s0):
    nj, nq = imp_t.shape
    jidx = lax.broadcasted_iota(jnp.int32, (nj, nq), 0)
    tq = s0 + lax.broadcasted_iota(jnp.int32, (nj, nq), 1)
    jcur = lax.shift_right_logical(tq, 6)
    forced = (jidx == 0) | (jidx == jcur) | (jidx == jcur - 1)
    future = jidx > jcur
    v = jnp.where(forced, FORCE_SCORE, jnp.where(future, -1.0, imp_t))
    jf = jidx.astype(F32)
    sel = jnp.zeros((nj, nq), F32)
    for _ in range(N_SELECT):
        m = jnp.max(v, axis=0, keepdims=True)
        jm = jnp.min(jnp.where(v == m, jf, float(nj)), axis=0, keepdims=True)
        hit = jf == jm
        sel = jnp.where(hit, 1.0, sel)
        v = jnp.where(hit, -2.0, v)
    return jnp.where((sel > 0.5) & jnp.logical_not(future), 0.0, NEG)


def _nsa_kernel(qt_ref, ksa_ref, vst_ref, kw_ref, vwt_ref, kc_ref, vct_ref, gt_ref, ovt_ref, o_ref,
                w_ref, sc_ref, sw_ref, ss_ref, mx_ref, e_ref, ew_ref, lo_ref, m_ref, acc_ref, out_ref):
    qb = pl.program_id(1)
    s0 = qb * Q_BLOCK
    nc = kc_ref.shape[1]
    two = 2 * Q_BLOCK
    t_lane = s0 + (lax.broadcasted_iota(jnp.int32, (1, two), 1) & (Q_BLOCK - 1))
    row_i = lax.broadcasted_iota(jnp.int32, (KEY_TILE, two), 0)
    gt = gt_ref[0]

    zero = jnp.zeros((HEAD_DIM, Q_BLOCK), BF16)
    for p in range(N_PAIRS):
        g = (2 * p) // NSA_REP
        halves = []
        for h in (2 * p, 2 * p + 1):
            qh = qt_ref[0, h * HEAD_DIM:(h + 1) * HEAD_DIM, :]
            halves.append(jnp.concatenate([qh, zero] if g == 0 else [zero, qh], axis=0))
        w_ref[p, 0:LANE, :] = jnp.concatenate(halves, axis=1)

    def store_scores(dst, s, valid):
        if valid is not None:
            s = jnp.where(valid, s, NEG)
        dst[...] = s
        return jnp.max(s, axis=0, keepdims=True)

    def exp_tiles(load, n_tiles, m, e_dst, lo_dst=None):
        for c in range(n_tiles):
            rows = slice(c * KEY_TILE, (c + 1) * KEY_TILE)
            e = jnp.exp2(load(rows) - m)
            eb = e.astype(BF16)
            e_dst[rows, :] = eb
            if lo_dst is not None:
                lo_dst[rows, :] = (e - eb.astype(F32)).astype(BF16)

    def values_t(ref, g, cols):
        return ref[0, g * V_AUG:(g + 1) * V_AUG, cols]

    def emit(p, branch, o_aug, first, guard=None):
        inv = 1.0 / o_aug[HEAD_DIM:HEAD_DIM + 1, :]
        if guard is not None:
            inv = jnp.where(guard, inv, 0.0)
        for hh in range(2):
            h = 2 * p + hh
            ls = slice(hh * Q_BLOCK, (hh + 1) * Q_BLOCK)
            o = o_aug[0:HEAD_DIM, ls] * (inv[:, ls] * gt[3 * h + branch:3 * h + branch + 1, :])
            rs = slice(h * HEAD_DIM, (h + 1) * HEAD_DIM)
            if first:
                out_ref[rs, :] = o
            else:
                out_ref[rs, :] += o
        return inv

    def key_rows(n):
        return lax.broadcasted_iota(jnp.int32, (n, two), 0)

    kc = kc_ref[0]
    cmp_valid = (key_rows(nc) * CMP_STRIDE + (CMP_LEN - 1)) <= t_lane
    m_cmp = [store_scores(sc_ref.at[p], _dot(kc, w_ref[p, 0:LANE, :]), cmp_valid) for p in range(N_PAIRS)]
    w0 = pl.multiple_of(jnp.maximum(s0 - WINDOW, 0), KEY_TILE)
    wlen = WINDOW + Q_BLOCK
    kwin = kw_ref[0, pl.ds(w0, wlen), :]
    diff = t_lane - (w0 + key_rows(wlen))
    win_valid = (diff >= 0) & (diff < WINDOW)
    m_win = [store_scores(sw_ref.at[p], _dot(kwin, w_ref[p, 0:LANE, :]), win_valid) for p in range(N_PAIRS)]

    imp_t = [jnp.zeros((LANE, Q_BLOCK), F32) for _ in range(NSA_GROUPS)]
    for p in range(N_PAIRS):
        g = (2 * p) // NSA_REP
        exp_tiles(lambda rows: sc_ref[p, rows, :], nc // KEY_TILE, m_cmp[p], e_ref.at[p], lo_ref.at[p])
        inv = emit(p, 0, _dot(values_t(vct_ref, g, slice(None)), e_ref[p, 0:nc, :]), True,
                   guard=m_cmp[p] > 0.5 * NEG)
        imp_p = (_dot(ovt_ref[...], e_ref[p, 0:nc, :]) + _dot(ovt_ref[...], lo_ref[p, 0:nc, :])) * inv
        imp_t[g] = imp_t[g] + imp_p[:, 0:Q_BLOCK] + imp_p[:, Q_BLOCK:two]

    for p in range(N_PAIRS):
        g = (2 * p) // NSA_REP
        exp_tiles(lambda rows: sw_ref[p, rows, :], wlen // KEY_TILE, m_win[p], ew_ref.at[p])
        emit(p, 2, _dot(values_t(vwt_ref, g, pl.ds(w0, wlen)), ew_ref[p]), False)

    for g in range(NSA_GROUPS):
        bias_t = _topk_bias_t(imp_t[g], s0).astype(BF16)
        for p in range(g * 2, g * 2 + 2):
            w_ref[p, LANE:2 * LANE, :] = jnp.concatenate([bias_t, bias_t], axis=1)

    m_ref[...] = jnp.full(m_ref.shape, NEG, F32)
    acc_ref[...] = jnp.zeros(acc_ref.shape, F32)

    def sel_scores(u, buf, p, causal):
        k0 = pl.multiple_of(u * KEY_UNIT, KEY_UNIT)
        ku = ksa_ref[0, pl.ds(k0, KEY_UNIT), :]
        valid = (k0 + key_rows(KEY_UNIT)) <= t_lane if causal else None
        mx_ref[buf, p] = store_scores(ss_ref.at[buf, p], _dot(ku, w_ref[p]), valid)

    def sel_update(u, buf, p):
        k0 = pl.multiple_of(u * KEY_UNIT, KEY_UNIT)
        g = (2 * p) // NSA_REP
        m_old = m_ref[p]
        m = jnp.maximum(m_old, mx_ref[buf, p])
        exp_tiles(lambda rows: ss_ref[buf, p, rows, :], KEY_UNIT // KEY_TILE, m, e_ref.at[p])
        m_ref[p] = m
        acc_ref[p] = jnp.exp2(m_old - m) * acc_ref[p] + _dot(
            values_t(vst_ref, g, pl.ds(k0, KEY_UNIT)), e_ref[p, 0:KEY_UNIT, :])

    def sel_step(u, buf, next_causal):
        for p in range(N_PAIRS):
            sel_scores(u + 1, 1 - buf, p, next_causal)
            sel_update(u, buf, p)

    def sel_last(u, buf):
        for p in range(N_PAIRS):
            sel_update(u, buf, p)

    n_before = qb // (KEY_UNIT // Q_BLOCK)
    n_trips = jnp.maximum(n_before - 1, 0) // 2
    rest = 2 * n_trips

    @pl.when(n_before > 0)
    def _():
        for p in range(N_PAIRS):
            sel_scores(0, 0, p, False)

    def body(i, carry):
        sel_step(2 * i, 0, False)
        sel_step(2 * i + 1, 1, False)
        return carry

    lax.fori_loop(0, n_trips, body, 0)

    @pl.when(n_before == 0)
    def _():
        for p in range(N_PAIRS):
            sel_scores(0, 0, p, True)
        sel_last(0, 0)

    @pl.when(n_before == rest + 1)
    def _():
        sel_step(rest, 0, True)
        sel_last(rest + 1, 1)

    @pl.when(n_before == rest + 2)
    def _():
        sel_step(rest, 0, False)
        sel_step(rest + 1, 1, True)
        sel_last(rest + 2, 0)

    for p in range(N_PAIRS):
        emit(p, 1, acc_ref[p], False)

    o_ref[0] = out_ref[...].T.astype(BF16)


def _nsa(qt, ksa, vst, kw, vwt, kc, vct, gt, ovt):
    b, _, t = qt.shape
    nc = kc.shape[1]
    qcol = lambda i, j: (i, 0, j)
    full = lambda i, j: (i, 0, 0)
    const = lambda i, j: (0, 0)
    return pl.pallas_call(
        _nsa_kernel,
        grid=(b, t // Q_BLOCK),
        in_specs=[
            pl.BlockSpec((1, NSA_WIDTH, Q_BLOCK), qcol),
            pl.BlockSpec((1, t, 2 * LANE), full),
            pl.BlockSpec((1, 2 * V_AUG, t), full),
            pl.BlockSpec((1, t, LANE), full),
            pl.BlockSpec((1, 2 * V_AUG, t), full),
            pl.BlockSpec((1, nc, LANE), full),
            pl.BlockSpec((1, 2 * V_AUG, nc), full),
            pl.BlockSpec((1, LANE, Q_BLOCK), qcol),
            pl.BlockSpec(ovt.shape, const),
        ],
        out_specs=pl.BlockSpec((1, Q_BLOCK, NSA_WIDTH), lambda i, j: (i, j, 0)),
        out_shape=jax.ShapeDtypeStruct((b, t, NSA_WIDTH), BF16),
        scratch_shapes=[
            pltpu.VMEM((N_PAIRS, 2 * LANE, 2 * Q_BLOCK), BF16),
            pltpu.VMEM((N_PAIRS, nc, 2 * Q_BLOCK), F32),
            pltpu.VMEM((N_PAIRS, WINDOW + Q_BLOCK, 2 * Q_BLOCK), F32),
            pltpu.VMEM((2, N_PAIRS, KEY_UNIT, 2 * Q_BLOCK), F32),
            pltpu.VMEM((2, N_PAIRS, 1, 2 * Q_BLOCK), F32),
            pltpu.VMEM((N_PAIRS, max(nc, KEY_UNIT), 2 * Q_BLOCK), BF16),
            pltpu.VMEM((N_PAIRS, WINDOW + Q_BLOCK, 2 * Q_BLOCK), BF16),
            pltpu.VMEM((N_PAIRS, nc, 2 * Q_BLOCK), BF16),
            pltpu.VMEM((N_PAIRS, 1, 2 * Q_BLOCK), F32),
            pltpu.VMEM((N_PAIRS, V_AUG, 2 * Q_BLOCK), F32),
            pltpu.VMEM((NSA_WIDTH, Q_BLOCK), F32),
        ],
        compiler_params=pltpu.CompilerParams(
            dimension_semantics=("arbitrary", "arbitrary"), vmem_limit_bytes=VMEM_LIMIT),
        name="nsa",
    )(qt, ksa, vst, kw, vwt, kc, vct, gt, ovt)


def _hgrn_kernel(ph_ref, lb_ref, og_ref, o_ref, st_ref, *, chunks):
    c_len = HGRN_CHUNK
    n_sub = c_len // HGRN_SUB

    @pl.when(pl.program_id(1) == 0)
    def _():
        st_ref[...] = jnp.zeros_like(st_ref)

    lg = lb_ref[...]
    lmax = jnp.max(lg, axis=0, keepdims=True)
    le = jnp.exp(lg - lmax)
    lb_all = le[0:1] / jnp.sum(le, axis=0, keepdims=True)

    ri = lax.broadcasted_iota(jnp.int32, (c_len, c_len), 0)
    ci = lax.broadcasted_iota(jnp.int32, (c_len, c_len), 1)
    tri = (ci <= ri).astype(BF16)
    causal = ci <= ri
    rowi = lax.broadcasted_iota(jnp.int32, (c_len, HGRN_DIM), 0)

    for h in range(HGRN_HEADS):
        hs = slice(h * HGRN_DIM, (h + 1) * HGRN_DIM)
        lb = lb_all[:, hs]
        gain = og_ref[...]
        st = st_ref[h]
        for c in range(chunks):
            rs = slice(c * c_len, (c + 1) * c_len)
            hq = ph_ref[0, rs, h * HGRN_DIM:(h + 1) * HGRN_DIM]
            hf = ph_ref[0, rs, HG_WIDTH + h * HGRN_DIM:HG_WIDTH + (h + 1) * HGRN_DIM]
            hi = ph_ref[0, rs, 2 * HG_WIDTH + h * HGRN_DIM:2 * HG_WIDTH + (h + 1) * HGRN_DIM]
            hg = ph_ref[0, rs, 3 * HG_WIDTH + h * HGRN_DIM:3 * HG_WIDTH + (h + 1) * HGRN_DIM]
            qv = jax.nn.silu(hq)
            f = lb + (1.0 - lb) * jax.nn.sigmoid(hf)
            logf = jnp.log(f)
            kv = 1.0 - f
            vb = hi.astype(BF16)
            g1 = logf.astype(BF16)
            r1 = logf - g1.astype(F32)
            g2 = r1.astype(BF16)
            g3 = (r1 - g2.astype(F32)).astype(BF16)
            bcum = _dot(tri, g1) + _dot(tri, g2) + _dot(tri, g3)
            b_last = bcum[c_len - 1:c_len]
            refs = [jnp.zeros((1, HGRN_DIM), F32)] + [bcum[i * HGRN_SUB - 1:i * HGRN_SUB] for i in range(1, n_sub)]
            rfull = jnp.concatenate([jnp.broadcast_to(r, (HGRN_SUB, HGRN_DIM)) for r in refs], axis=0)
            qd = (qv * jnp.exp(bcum - rfull)).astype(BF16)
            parts = []
            for i in range(n_sub):
                e = jnp.where(rowi < (i + 1) * HGRN_SUB, refs[i] - bcum, 0.0)
                kd = (kv * jnp.exp(e)).astype(BF16)
                parts.append(_dot_t(qd[i * HGRN_SUB:(i + 1) * HGRN_SUB], kd))
            attn = jnp.where(causal, jnp.concatenate(parts, axis=0), 0.0)
            o = _dot(attn.astype(BF16), vb) + _dot_t((qv * jnp.exp(bcum)).astype(BF16), st.astype(BF16))
            kdec = (kv * jnp.exp(b_last - bcum)).astype(BF16)
            st = st * jnp.exp(b_last) + _dot(hi.T.astype(BF16), kdec)
            ms = jnp.mean(o * o, axis=-1, keepdims=True)
            o = o * lax.rsqrt(ms + RMS_EPS) * gain * jax.nn.silu(hg)
            o_ref[0, rs, hs] = o.astype(BF16)
        st_ref[h] = st


def _hgrn(ph, lb_logits, o_gain, tt=256):
    b, t, _ = ph.shape
    blk = lambda i, j: (i, j, 0)
    const = lambda i, j: (0, 0)
    return pl.pallas_call(
        functools.partial(_hgrn_kernel, chunks=tt // HGRN_CHUNK),
        grid=(b, t // tt),
        in_specs=[
            pl.BlockSpec((1, tt, 4 * HG_WIDTH), blk),
            pl.BlockSpec(lb_logits.shape, const),
            pl.BlockSpec((1, HGRN_DIM), const),
        ],
        out_specs=pl.BlockSpec((1, tt, HG_WIDTH), blk),
        out_shape=jax.ShapeDtypeStruct((b, t, HG_WIDTH), BF16),
        scratch_shapes=[pltpu.VMEM((HGRN_HEADS, HGRN_DIM, HGRN_DIM), F32)],
        compiler_params=pltpu.CompilerParams(
            dimension_semantics=("arbitrary", "arbitrary"), vmem_limit_bytes=VMEM_LIMIT),
        name="hgrn2",
    )(ph, lb_logits, o_gain)


def _ff_chunks(d_ff, width):
    return tuple((s, min(width, d_ff - s)) for s in range(0, d_ff, width))


def _out_ffn_kernel(x_ref, on_ref, oh_ref, won_ref, woh_ref, g_ref, wgu_ref, wd_ref, o_ref, *, chunks):
    d_ff = wd_ref.shape[0]
    x1 = x_ref[...] + _dot(on_ref[...], won_ref[...]) + _dot(oh_ref[...], woh_ref[...])
    ms = jnp.mean(x1 * x1, axis=-1, keepdims=True)
    h = (x1 * lax.rsqrt(ms + RMS_EPS) * g_ref[...]).astype(BF16)
    o_ref[...] = x1
    for s, n in chunks:
        gate = _dot(h, wgu_ref[:, s:s + n])
        up = _dot(h, wgu_ref[:, d_ff + s:d_ff + s + n])
        act = (jax.nn.silu(gate) * up).astype(BF16)
        o_ref[...] += _dot(act, wd_ref[s:s + n, :])


def _out_ffn(x2, o_nsa, o_hg, won, woh, g, wgu, wd, tm=256, ff_width=4 * MXU_WIDTH):
    n, d = x2.shape
    d_ff = wd.shape[0]
    row = lambda i: (i, 0)
    const = lambda i: (0, 0)
    return pl.pallas_call(
        functools.partial(_out_ffn_kernel, chunks=_ff_chunks(d_ff, ff_width)),
        grid=(n // tm,),
        in_specs=[
            pl.BlockSpec((tm, d), row),
            pl.BlockSpec((tm, NSA_WIDTH), row),
            pl.BlockSpec((tm, HG_WIDTH), row),
            pl.BlockSpec(won.shape, const),
            pl.BlockSpec(woh.shape, const),
            pl.BlockSpec((1, d), const),
            pl.BlockSpec(wgu.shape, const),
            pl.BlockSpec(wd.shape, const),
        ],
        out_specs=pl.BlockSpec((tm, d), row),
        out_shape=jax.ShapeDtypeStruct((n, d), F32),
        compiler_params=pltpu.CompilerParams(
            dimension_semantics=("arbitrary",), vmem_limit_bytes=VMEM_LIMIT),
        name="out_ffn",
    )(x2, o_nsa, o_hg, won, woh, g, wgu, wd)


def _expand_cmp_weights(pos, w1, w2):
    eye = jnp.eye(NSA_GROUPS, dtype=F32)
    w1r = w1.reshape(CMP_LEN, HEAD_DIM, CMP_HIDDEN)

    def lift(wpart):
        return jnp.einsum('ldc,gk->lgdkc', wpart, eye).reshape(
            CMP_STRIDE * NSA_GROUPS * HEAD_DIM, NSA_GROUPS * CMP_HIDDEN)

    wlo = lift(w1r[:CMP_STRIDE]).astype(BF16)
    whi = lift(w1r[CMP_STRIDE:]).astype(BF16)
    w2x = jnp.einsum('cd,gk->gckd', w2, eye).reshape(NSA_GROUPS * CMP_HIDDEN, NSA_GROUPS * HEAD_DIM).astype(BF16)
    pbias = jnp.tile(pos.reshape(1, CMP_LEN * HEAD_DIM) @ w1, (1, NSA_GROUPS))
    return wlo, whi, w2x, pbias


def _mixers(x, norm_mix, w_in, q_norm, k_norm, cmp_pos_k, cmp_pos_v, cmp_k_w1, cmp_k_w2, cmp_v_w1, cmp_v_w2,
            hgrn_lb_logits, hgrn_o_norm):
    b, t, d = x.shape
    depth = norm_mix.shape[0]
    assert depth == 1 and hgrn_lb_logits.shape[0] == 2
    assert t % KEY_UNIT == 0 and t >= WINDOW + Q_BLOCK and t // SLC_LEN <= LANE
    assert (t // CMP_STRIDE) % KEY_TILE == 0
    l = 0

    o_g, o_h = NSA_WIDTH + 6 * KV_WIDTH, NSA_WIDTH + 6 * KV_WIDTH + N_GATES
    w = w_in[l]
    w_perm = jnp.concatenate([
        w[:, :o_g], w[:, o_h:], w[:, o_g:o_h], jnp.zeros((d, LANE - N_GATES), w.dtype)], axis=1).astype(BF16)

    bd = jnp.asarray(np.kron(np.eye(LANE // HEAD_DIM), np.ones((HEAD_DIM, HEAD_DIM))), BF16)
    tile2 = lambda v: jnp.tile(v.reshape(1, HEAD_DIM), (1, LANE // HEAD_DIM)).astype(F32)
    qg = tile2(q_norm[l]) * (HEAD_DIM ** -0.5 * math.log2(math.e))

    qt, kc_raw, vc_raw, ksa, vst, kw, vwt, gt, ph = _in_proj(
        x, norm_mix[l].reshape(1, d), w_perm, bd, qg, tile2(k_norm[l, 1]), tile2(k_norm[l, 2]))

    ns = t // CMP_STRIDE
    seg_w = CMP_STRIDE * KV_WIDTH
    kc, vct = _compress(
        kc_raw.reshape(b, ns, seg_w), vc_raw.reshape(b, ns, seg_w),
        _expand_cmp_weights(cmp_pos_k[l], cmp_k_w1[l], cmp_k_w2[l]),
        _expand_cmp_weights(cmp_pos_v[l], cmp_v_w1[l], cmp_v_w2[l]),
        bd, tile2(k_norm[l, 0]))

    cs = np.arange(ns)[None, :] * CMP_STRIDE
    ss = np.arange(LANE)[:, None] * SLC_LEN
    ovt = jnp.asarray(((cs < ss + SLC_LEN) & (cs + CMP_LEN > ss)).astype(np.float32), BF16)

    o_nsa = _nsa(qt, ksa, vst, kw, vwt, kc, vct, gt, ovt)
    o_hg = _hgrn(ph, hgrn_lb_logits, hgrn_o_norm[l].reshape(1, HGRN_DIM))
    return o_nsa, o_hg


def kernel(x, norm_mix, w_in, q_norm, k_norm, cmp_pos_k, cmp_pos_v, cmp_k_w1, cmp_k_w2, cmp_v_w1, cmp_v_w2,
           hgrn_lb_logits, hgrn_o_norm, w_out, norm_ffn, w_gate_up, w_down):
    b, t, d = x.shape
    l = 0
    o_nsa, o_hg = _mixers(x, norm_mix, w_in, q_norm, k_norm, cmp_pos_k, cmp_pos_v, cmp_k_w1, cmp_k_w2,
                          cmp_v_w1, cmp_v_w2, hgrn_lb_logits, hgrn_o_norm)
    wo = w_out[l]
    out = _out_ffn(
        x.reshape(b * t, d), o_nsa.reshape(b * t, NSA_WIDTH), o_hg.reshape(b * t, HG_WIDTH),
        wo[:NSA_WIDTH].astype(BF16), wo[NSA_WIDTH:].astype(BF16),
        norm_ffn[l].reshape(1, d), w_gate_up[l].astype(BF16), w_down[l].astype(BF16))
    return out.reshape(b, t, d)
```

```python
import functools
import math

import numpy as np
import jax
import jax.numpy as jnp
from jax import lax
from jax.experimental import pallas as pl
from jax.experimental.pallas import tpu as pltpu

F32 = jnp.float32
BF16 = jnp.bfloat16

LANE = 128
MXU_WIDTH = 256

NSA_HEADS = 8
NSA_GROUPS = 2
NSA_REP = NSA_HEADS // NSA_GROUPS
HEAD_DIM = 64
CMP_LEN = 32
CMP_STRIDE = 16
CMP_HIDDEN = 128
SLC_LEN = 64
N_SELECT = 16
WINDOW = 512
Q_BLOCK = 128
HGRN_HEADS = 4
HGRN_DIM = 128
HGRN_CHUNK = 64
HGRN_SUB = 16
RMS_EPS = 1e-6
NEG = -1e30

NSA_WIDTH = NSA_HEADS * HEAD_DIM
KV_WIDTH = NSA_GROUPS * HEAD_DIM
HG_WIDTH = HGRN_HEADS * HGRN_DIM
N_GATES = 3 * NSA_HEADS

KEY_TILE = 128
KEY_UNIT = 4 * KEY_TILE
N_PAIRS = NSA_HEADS // 2
V_AUG = HEAD_DIM + 16
VMEM_LIMIT = 56 * 1024 * 1024


def _dot(a, b):
    return jnp.dot(a, b, preferred_element_type=F32)


def _dot_t(a, b):
    return lax.dot_general(a, b, (((1,), (1,)), ((), ())), preferred_element_type=F32)


def _split2(x):
    hi = x.astype(BF16)
    lo = (x - hi.astype(F32)).astype(BF16)
    return hi, lo


def _group_sumsq(y, bd):
    hi, lo = _split2(y * y)
    return _dot(hi, bd) + _dot(lo, bd)


def _group_rms(y, bd, gain):
    ss = _group_sumsq(y, bd)
    return y * lax.rsqrt(ss * (1.0 / HEAD_DIM) + RMS_EPS) * gain


def _values_t_aug(v):
    vt = v.T
    ones = jnp.ones((V_AUG - HEAD_DIM, v.shape[0]), F32)
    return jnp.concatenate([vt[0:HEAD_DIM], ones, vt[HEAD_DIM:2 * HEAD_DIM], ones], axis=0).astype(BF16)


def _in_proj_kernel(x_ref, g_ref, w_ref, bd_ref, qg_ref, ksg_ref, kwg_ref,
                    qt_ref, kc_ref, vc_ref, ksa_ref, vst_ref, kw_ref, vwt_ref, gt_ref, ph_ref):
    tm = x_ref.shape[1]
    x = x_ref[0]
    ms = jnp.mean(x * x, axis=-1, keepdims=True)
    h = (x * lax.rsqrt(ms + RMS_EPS) * g_ref[...]).astype(BF16)
    y = _dot(h, w_ref[...])
    bd = bd_ref[...]
    for r in range(NSA_WIDTH // LANE):
        sl = slice(r * LANE, (r + 1) * LANE)
        qt_ref[0, sl, :] = _group_rms(y[:, sl], bd, qg_ref[...]).T.astype(BF16)
    o = NSA_WIDTH
    kc_ref[0] = y[:, o:o + LANE]
    vc_ref[0] = y[:, o + LANE:o + 2 * LANE]
    ksa_ref[0, :, 0:LANE] = _group_rms(y[:, o + 2 * LANE:o + 3 * LANE], bd, ksg_ref[...]).astype(BF16)
    key = pl.program_id(1) * tm + lax.broadcasted_iota(jnp.int32, (tm, LANE), 0)
    blk = lax.broadcasted_iota(jnp.int32, (tm, LANE), 1)
    ksa_ref[0, :, LANE:2 * LANE] = jnp.where(lax.shift_right_logical(key, 6) == blk, 1.0, 0.0).astype(BF16)
    vst_ref[0] = _values_t_aug(y[:, o + 3 * LANE:o + 4 * LANE])
    kw_ref[0] = _group_rms(y[:, o + 4 * LANE:o + 5 * LANE], bd, kwg_ref[...]).astype(BF16)
    vwt_ref[0] = _values_t_aug(y[:, o + 5 * LANE:o + 6 * LANE])
    o = NSA_WIDTH + 6 * LANE
    ph_ref[0] = y[:, o:o + 4 * HG_WIDTH]
    o = o + 4 * HG_WIDTH
    gt_ref[0] = jax.nn.sigmoid(y[:, o:o + LANE]).T


def _in_proj(x, g, w, bd, qg, ksg, kwg, tm=256):
    b, t, d = x.shape
    nw = w.shape[1]
    rows = lambda i, j: (i, j, 0)
    cols = lambda i, j: (i, 0, j)
    const = lambda i, j: (0, 0)
    outs = [
        (jax.ShapeDtypeStruct((b, NSA_WIDTH, t), BF16), pl.BlockSpec((1, NSA_WIDTH, tm), cols)),
        (jax.ShapeDtypeStruct((b, t, LANE), F32), pl.BlockSpec((1, tm, LANE), rows)),
        (jax.ShapeDtypeStruct((b, t, LANE), F32), pl.BlockSpec((1, tm, LANE), rows)),
        (jax.ShapeDtypeStruct((b, t, 2 * LANE), BF16), pl.BlockSpec((1, tm, 2 * LANE), rows)),
        (jax.ShapeDtypeStruct((b, 2 * V_AUG, t), BF16), pl.BlockSpec((1, 2 * V_AUG, tm), cols)),
        (jax.ShapeDtypeStruct((b, t, LANE), BF16), pl.BlockSpec((1, tm, LANE), rows)),
        (jax.ShapeDtypeStruct((b, 2 * V_AUG, t), BF16), pl.BlockSpec((1, 2 * V_AUG, tm), cols)),
        (jax.ShapeDtypeStruct((b, LANE, t), F32), pl.BlockSpec((1, LANE, tm), cols)),
        (jax.ShapeDtypeStruct((b, t, 4 * HG_WIDTH), F32), pl.BlockSpec((1, tm, 4 * HG_WIDTH), rows)),
    ]
    return pl.pallas_call(
        _in_proj_kernel,
        grid=(b, t // tm),
        in_specs=[
            pl.BlockSpec((1, tm, d), rows),
            pl.BlockSpec((1, d), const),
            pl.BlockSpec((d, nw), const),
            pl.BlockSpec((LANE, LANE), const),
            pl.BlockSpec((1, LANE), const),
            pl.BlockSpec((1, LANE), const),
            pl.BlockSpec((1, LANE), const),
        ],
        out_specs=[s for _, s in outs],
        out_shape=[s for s, _ in outs],
        compiler_params=pltpu.CompilerParams(
            dimension_semantics=("arbitrary", "arbitrary"), vmem_limit_bytes=VMEM_LIMIT),
        name="in_proj",
    )(x, g, w, bd, qg, ksg, kwg)


def _compress_kernel(xk_ref, xv_ref, wklo_ref, wkhi_ref, wk2_ref, pk_ref,
                     wvlo_ref, wvhi_ref, wv2_ref, pv_ref, bd_ref, kg_ref, kc_ref, vct_ref):
    ns = xk_ref.shape[1]

    def mlp(x_ref, wlo_ref, whi_ref, w2_ref, p_ref):
        xb = x_ref[0].astype(BF16)
        a = _dot(xb, wlo_ref[...])
        b = _dot(xb, whi_ref[...])
        h = a + pltpu.roll(b, ns - 1, axis=0) + p_ref[...]
        return _dot(jax.nn.gelu(h).astype(BF16), w2_ref[...])

    kc = mlp(xk_ref, wklo_ref, wkhi_ref, wk2_ref, pk_ref)
    kc_ref[0] = _group_rms(kc, bd_ref[...], kg_ref[...]).astype(BF16)
    vct_ref[0] = _values_t_aug(mlp(xv_ref, wvlo_ref, wvhi_ref, wv2_ref, pv_ref))


def _compress(xk, xv, wk, wv, bd, kg):
    b, ns, wd = xk.shape
    const2 = lambda i: (0, 0)
    bat = lambda i: (i, 0, 0)
    wspecs = [pl.BlockSpec(w.shape, const2) for w in wk] + [pl.BlockSpec(w.shape, const2) for w in wv]
    return pl.pallas_call(
        _compress_kernel,
        grid=(b,),
        in_specs=[pl.BlockSpec((1, ns, wd), bat), pl.BlockSpec((1, ns, wd), bat)] + wspecs + [
            pl.BlockSpec((LANE, LANE), const2), pl.BlockSpec((1, LANE), const2)],
        out_specs=[pl.BlockSpec((1, ns, LANE), bat), pl.BlockSpec((1, 2 * V_AUG, ns), bat)],
        out_shape=[jax.ShapeDtypeStruct((b, ns, LANE), BF16), jax.ShapeDtypeStruct((b, 2 * V_AUG, ns), BF16)],
        compiler_params=pltpu.CompilerParams(
            dimension_semantics=("arbitrary",), vmem_limit_bytes=VMEM_LIMIT),
        name="compress",
    )(xk, xv, *wk, *wv, bd, kg)


def _topk_bias_t(imp_t, s0):
    nj, nq = imp_t.shape
    jidx = lax.broadcasted_iota(jnp.int32, (nj, nq), 0)
    tq = s0 + lax.broadcasted_iota(jnp.int32, (nj, nq), 1)
    jcur = lax.shift_right_logical(tq, 6)
    forced = (jidx == 0) | (jidx == jcur) | (jidx == jcur - 1)
    future = jidx > jcur
    v = jnp.where(forced, -2.0, jnp.where(future, -1.0, imp_t))
    jf = jidx.astype(F32)
    sel = jnp.where(forced, 1.0, 0.0)
    for _ in range(N_SELECT - 3):
        m = jnp.max(v, axis=0, keepdims=True)
        jm = jnp.min(jnp.where(v == m, jf, float(nj)), axis=0, keepdims=True)
        hit = jf == jm
        sel = jnp.where(hit, 1.0, sel)
        v = jnp.where(hit, -2.0, v)
    return jnp.where((sel > 0.5) & jnp.logical_not(future), 0.0, NEG)


def _nsa_kernel(qt_ref, ksa_ref, vst_ref, kw_ref, vwt_ref, kc_ref, vct_ref, gt_ref, ovt_ref, o_ref,
                w_ref, sc_ref, sw_ref, ss_ref, mx_ref, e_ref, ew_ref, lo_ref, m_ref, acc_ref, out_ref):
    qb = pl.program_id(1)
    s0 = qb * Q_BLOCK
    nc = kc_ref.shape[1]
    two = 2 * Q_BLOCK
    t_lane = s0 + (lax.broadcasted_iota(jnp.int32, (1, two), 1) & (Q_BLOCK - 1))
    gt = gt_ref[0]

    zero = jnp.zeros((HEAD_DIM, Q_BLOCK), BF16)
    for p in range(N_PAIRS):
        g = (2 * p) // NSA_REP
        halves = []
        for h in (2 * p, 2 * p + 1):
            qh = qt_ref[0, h * HEAD_DIM:(h + 1) * HEAD_DIM, :]
            halves.append(jnp.concatenate([qh, zero] if g == 0 else [zero, qh], axis=0))
        w_ref[p, 0:LANE, :] = jnp.concatenate(halves, axis=1)

    def store_scores(dst, s, valid):
        if valid is not None:
            s = jnp.where(valid, s, NEG)
        dst[...] = s
        return jnp.max(s, axis=0, keepdims=True)

    def exp_tiles(load, n_tiles, m, e_dst, lo_dst=None):
        for c in range(n_tiles):
            rows = slice(c * KEY_TILE, (c + 1) * KEY_TILE)
            e = jnp.exp2(load(rows) - m)
            eb = e.astype(BF16)
            e_dst[rows, :] = eb
            if lo_dst is not None:
                lo_dst[rows, :] = (e - eb.astype(F32)).astype(BF16)

    def values_t(ref, g, cols):
        return ref[0, g * V_AUG:(g + 1) * V_AUG, cols]

    def emit(p, branch, o_aug, first, guard=None):
        inv = 1.0 / o_aug[HEAD_DIM:HEAD_DIM + 1, :]
        if guard is not None:
            inv = jnp.where(guard, inv, 0.0)
        for hh in range(2):
            h = 2 * p + hh
            ls = slice(hh * Q_BLOCK, (hh + 1) * Q_BLOCK)
            o = o_aug[0:HEAD_DIM, ls] * (inv[:, ls] * gt[3 * h + branch:3 * h + branch + 1, :])
            rs = slice(h * HEAD_DIM, (h + 1) * HEAD_DIM)
            if first:
                out_ref[rs, :] = o
            else:
                out_ref[rs, :] += o
        return inv

    def key_rows(n):
        return lax.broadcasted_iota(jnp.int32, (n, two), 0)

    kc = kc_ref[0]
    cmp_valid = (key_rows(nc) * CMP_STRIDE + (CMP_LEN - 1)) <= t_lane
    m_cmp = [store_scores(sc_ref.at[p], _dot(kc, w_ref[p, 0:LANE, :]), cmp_valid) for p in range(N_PAIRS)]
    w0 = pl.multiple_of(jnp.maximum(s0 - WINDOW, 0), KEY_TILE)
    wlen = WINDOW + Q_BLOCK
    kwin = kw_ref[0, pl.ds(w0, wlen), :]
    diff = t_lane - (w0 + key_rows(wlen))
    win_valid = (diff >= 0) & (diff < WINDOW)
    m_win = [store_scores(sw_ref.at[p], _dot(kwin, w_ref[p, 0:LANE, :]), win_valid) for p in range(N_PAIRS)]

    imp_t = [jnp.zeros((LANE, Q_BLOCK), F32) for _ in range(NSA_GROUPS)]
    for p in range(N_PAIRS):
        g = (2 * p) // NSA_REP
        exp_tiles(lambda rows: sc_ref[p, rows, :], nc // KEY_TILE, m_cmp[p], e_ref.at[p], lo_ref.at[p])
        inv = emit(p, 0, _dot(values_t(vct_ref, g, slice(None)), e_ref[p, 0:nc, :]), True,
                   guard=m_cmp[p] > 0.5 * NEG)
        imp_p = (_dot(ovt_ref[...], e_ref[p, 0:nc, :]) + _dot(ovt_ref[...], lo_ref[p, 0:nc, :])) * inv
        imp_t[g] = imp_t[g] + imp_p[:, 0:Q_BLOCK] + imp_p[:, Q_BLOCK:two]

    for p in range(N_PAIRS):
        g = (2 * p) // NSA_REP
        exp_tiles(lambda rows: sw_ref[p, rows, :], wlen // KEY_TILE, m_win[p], ew_ref.at[p])
        emit(p, 2, _dot(values_t(vwt_ref, g, pl.ds(w0, wlen)), ew_ref[p]), False)

    for g in range(NSA_GROUPS):
        bias_t = _topk_bias_t(imp_t[g], s0).astype(BF16)
        for p in range(g * 2, g * 2 + 2):
            w_ref[p, LANE:2 * LANE, :] = jnp.concatenate([bias_t, bias_t], axis=1)

    m_ref[...] = jnp.full(m_ref.shape, NEG, F32)
    acc_ref[...] = jnp.zeros(acc_ref.shape, F32)

    def sel_scores(u, buf, p, causal):
        k0 = pl.multiple_of(u * KEY_UNIT, KEY_UNIT)
        ku = ksa_ref[0, pl.ds(k0, KEY_UNIT), :]
        valid = (k0 + key_rows(KEY_UNIT)) <= t_lane if causal else None
        mx_ref[buf, p] = store_scores(ss_ref.at[buf, p], _dot(ku, w_ref[p]), valid)

    def sel_update(u, buf, p):
        k0 = pl.multiple_of(u * KEY_UNIT, KEY_UNIT)
        g = (2 * p) // NSA_REP
        m_old = m_ref[p]
        m = jnp.maximum(m_old, mx_ref[buf, p])
        exp_tiles(lambda rows: ss_ref[buf, p, rows, :], KEY_UNIT // KEY_TILE, m, e_ref.at[p])
        m_ref[p] = m
        acc_ref[p] = jnp.exp2(m_old - m) * acc_ref[p] + _dot(
            values_t(vst_ref, g, pl.ds(k0, KEY_UNIT)), e_ref[p, 0:KEY_UNIT, :])

    def sel_step(u, buf, next_causal):
        for p in range(N_PAIRS):
            sel_scores(u + 1, 1 - buf, p, next_causal)
            sel_update(u, buf, p)

    def sel_last(u, buf):
        for p in range(N_PAIRS):
            sel_update(u, buf, p)

    n_before = qb // (KEY_UNIT // Q_BLOCK)
    n_trips = jnp.maximum(n_before - 1, 0) // 2
    rest = 2 * n_trips

    @pl.when(n_before > 0)
    def _():
        for p in range(N_PAIRS):
            sel_scores(0, 0, p, False)

    def body(i, carry):
        sel_step(2 * i, 0, False)
        sel_step(2 * i + 1, 1, False)
        return carry

    lax.fori_loop(0, n_trips, body, 0)

    @pl.when(n_before == 0)
    def _():
        for p in range(N_PAIRS):
            sel_scores(0, 0, p, True)
        sel_last(0, 0)

    @pl.when(n_before == rest + 1)
    def _():
        sel_step(rest, 0, True)
        sel_last(rest + 1, 1)

    @pl.when(n_before == rest + 2)
    def _():
        sel_step(rest, 0, False)
        sel_step(rest + 1, 1, True)
        sel_last(rest + 2, 0)

    for p in range(N_PAIRS):
        emit(p, 1, acc_ref[p], False)

    o_ref[0] = out_ref[...].T.astype(BF16)


def _nsa(qt, ksa, vst, kw, vwt, kc, vct, gt, ovt):
    b, _, t = qt.shape
    nc = kc.shape[1]
    qcol = lambda i, j: (i, 0, j)
    full = lambda i, j: (i, 0, 0)
    const = lambda i, j: (0, 0)
    return pl.pallas_call(
        _nsa_kernel,
        grid=(b, t // Q_BLOCK),
        in_specs=[
            pl.BlockSpec((1, NSA_WIDTH, Q_BLOCK), qcol),
            pl.BlockSpec((1, t, 2 * LANE), full),
            pl.BlockSpec((1, 2 * V_AUG, t), full),
            pl.BlockSpec((1, t, LANE), full),
            pl.BlockSpec((1, 2 * V_AUG, t), full),
            pl.BlockSpec((1, nc, LANE), full),
            pl.BlockSpec((1, 2 * V_AUG, nc), full),
            pl.BlockSpec((1, LANE, Q_BLOCK), qcol),
            pl.BlockSpec(ovt.shape, const),
        ],
        out_specs=pl.BlockSpec((1, Q_BLOCK, NSA_WIDTH), lambda i, j: (i, j, 0)),
        out_shape=jax.ShapeDtypeStruct((b, t, NSA_WIDTH), BF16),
        scratch_shapes=[
            pltpu.VMEM((N_PAIRS, 2 * LANE, 2 * Q_BLOCK), BF16),
            pltpu.VMEM((N_PAIRS, nc, 2 * Q_BLOCK), F32),
            pltpu.VMEM((N_PAIRS, WINDOW + Q_BLOCK, 2 * Q_BLOCK), F32),
            pltpu.VMEM((2, N_PAIRS, KEY_UNIT, 2 * Q_BLOCK), F32),
            pltpu.VMEM((2, N_PAIRS, 1, 2 * Q_BLOCK), F32),
            pltpu.VMEM((N_PAIRS, max(nc, KEY_UNIT), 2 * Q_BLOCK), BF16),
            pltpu.VMEM((N_PAIRS, WINDOW + Q_BLOCK, 2 * Q_BLOCK), BF16),
            pltpu.VMEM((N_PAIRS, nc, 2 * Q_BLOCK), BF16),
            pltpu.VMEM((N_PAIRS, 1, 2 * Q_BLOCK), F32),
            pltpu.VMEM((N_PAIRS, V_AUG, 2 * Q_BLOCK), F32),
            pltpu.VMEM((NSA_WIDTH, Q_BLOCK), F32),
        ],
        compiler_params=pltpu.CompilerParams(
            dimension_semantics=("arbitrary", "arbitrary"), vmem_limit_bytes=VMEM_LIMIT),
        name="nsa",
    )(qt, ksa, vst, kw, vwt, kc, vct, gt, ovt)


def _hgrn_kernel(ph_ref, lb_ref, og_ref, o_ref, st_ref, *, chunks):
    c_len = HGRN_CHUNK
    n_sub = c_len // HGRN_SUB

    @pl.when(pl.program_id(1) == 0)
    def _():
        st_ref[...] = jnp.zeros_like(st_ref)

    lg = lb_ref[...]
    lmax = jnp.max(lg, axis=0, keepdims=True)
    le = jnp.exp(lg - lmax)
    lb_all = le[0:1] / jnp.sum(le, axis=0, keepdims=True)

    ri = lax.broadcasted_iota(jnp.int32, (c_len, c_len), 0)
    ci = lax.broadcasted_iota(jnp.int32, (c_len, c_len), 1)
    tri = (ci <= ri).astype(BF16)
    causal = ci <= ri
    rowi = lax.broadcasted_iota(jnp.int32, (c_len, HG_WIDTH), 0)
    gain = og_ref[...]
    heads = [slice(h * HGRN_DIM, (h + 1) * HGRN_DIM) for h in range(HGRN_HEADS)]

    pre = []
    for c in range(chunks):
        rs = slice(c * c_len, (c + 1) * c_len)
        hq = ph_ref[0, rs, 0:HG_WIDTH]
        hf = ph_ref[0, rs, HG_WIDTH:2 * HG_WIDTH]
        qv = jax.nn.silu(hq)
        f = lb_all + (1.0 - lb_all) * jax.nn.sigmoid(hf)
        logf = jnp.log(f)
        kv = 1.0 - f
        g1 = logf.astype(BF16)
        r1 = logf - g1.astype(F32)
        g2 = r1.astype(BF16)
        g3 = (r1 - g2.astype(F32)).astype(BF16)
        bcum = _dot(tri, g1) + _dot(tri, g2) + _dot(tri, g3)
        pre.append((qv, kv, bcum))

    mid = []
    for c in range(chunks):
        qv, kv, bcum = pre[c]
        b_last = bcum[c_len - 1:c_len]
        starts = [jnp.zeros((1, HG_WIDTH), F32)] + [bcum[i * HGRN_SUB - 1:i * HGRN_SUB] for i in range(1, n_sub)]
        ends = starts[1:] + [b_last]

        def per_sub(rows):
            return jnp.concatenate([jnp.broadcast_to(r, (HGRN_SUB, HG_WIDTH)) for r in rows], axis=0)

        start_full = per_sub(starts)
        qd = qv * jnp.exp(bcum - start_full)
        kb = kv * jnp.exp(per_sub(ends) - bcum)
        kdiag = kv * jnp.exp(start_full - bcum)
        qs = (qd * per_sub([jnp.exp(s) for s in starts])).astype(BF16)
        kdec = (kb * per_sub([jnp.exp(b_last - e) for e in ends])).astype(BF16)
        kds = []
        for i in range(n_sub):
            blocks = [kb[j * HGRN_SUB:(j + 1) * HGRN_SUB] * jnp.exp(starts[i] - ends[j]) for j in range(i - 1)]
            if i > 0:
                blocks.append(kb[(i - 1) * HGRN_SUB:i * HGRN_SUB])
            blocks.append(kdiag[i * HGRN_SUB:(i + 1) * HGRN_SUB])
            if i + 1 < n_sub:
                blocks.append(jnp.zeros(((n_sub - 1 - i) * HGRN_SUB, HG_WIDTH), F32))
            kds.append(jnp.concatenate(blocks, axis=0).astype(BF16))
        mid.append((qd.astype(BF16), kds, qs, kdec, jnp.exp(b_last)))

    attn = {}
    upd = {}
    for c in range(chunks):
        rs = slice(c * c_len, (c + 1) * c_len)
        qd, kds, qs, kdec, dec = mid[c]
        for h, hs in enumerate(heads):
            parts = [_dot_t(qd[i * HGRN_SUB:(i + 1) * HGRN_SUB, hs], kds[i][:, hs]) for i in range(n_sub)]
            attn[c, h] = jnp.where(causal, jnp.concatenate(parts, axis=0), 0.0).astype(BF16)
            hi = ph_ref[0, rs, 2 * HG_WIDTH + h * HGRN_DIM:2 * HG_WIDTH + (h + 1) * HGRN_DIM]
            upd[c, h] = _dot(hi.T.astype(BF16), kdec[:, hs])

    state = {}
    for h, hs in enumerate(heads):
        st = st_ref[h]
        for c in range(chunks):
            state[c, h] = st.astype(BF16)
            st = st * mid[c][4][:, hs] + upd[c, h]
        st_ref[h] = st

    for c in range(chunks):
        rs = slice(c * c_len, (c + 1) * c_len)
        qs = mid[c][2]
        for h, hs in enumerate(heads):
            vb = ph_ref[0, rs, 2 * HG_WIDTH + h * HGRN_DIM:2 * HG_WIDTH + (h + 1) * HGRN_DIM].astype(BF16)
            hg = ph_ref[0, rs, 3 * HG_WIDTH + h * HGRN_DIM:3 * HG_WIDTH + (h + 1) * HGRN_DIM]
            o = _dot(attn[c, h], vb) + _dot_t(qs[:, hs], state[c, h])
            ms = jnp.mean(o * o, axis=-1, keepdims=True)
            o = o * lax.rsqrt(ms + RMS_EPS) * gain * jax.nn.silu(hg)
            o_ref[0, rs, hs] = o.astype(BF16)


def _hgrn(ph, lb_logits, o_gain, tt=256):
    b, t, _ = ph.shape
    blk = lambda i, j: (i, j, 0)
    const = lambda i, j: (0, 0)
    return pl.pallas_call(
        functools.partial(_hgrn_kernel, chunks=tt // HGRN_CHUNK),
        grid=(b, t // tt),
        in_specs=[
            pl.BlockSpec((1, tt, 4 * HG_WIDTH), blk),
            pl.BlockSpec(lb_logits.shape, const),
            pl.BlockSpec((1, HGRN_DIM), const),
        ],
        out_specs=pl.BlockSpec((1, tt, HG_WIDTH), blk),
        out_shape=jax.ShapeDtypeStruct((b, t, HG_WIDTH), BF16),
        scratch_shapes=[pltpu.VMEM((HGRN_HEADS, HGRN_DIM, HGRN_DIM), F32)],
        compiler_params=pltpu.CompilerParams(
            dimension_semantics=("arbitrary", "arbitrary"), vmem_limit_bytes=VMEM_LIMIT),
        name="hgrn2",
    )(ph, lb_logits, o_gain)


def _ff_chunks(d_ff, width):
    return tuple((s, min(width, d_ff - s)) for s in range(0, d_ff, width))


def _out_ffn_kernel(x_ref, on_ref, oh_ref, won_ref, woh_ref, g_ref, wgu_ref, wd_ref, o_ref, *, chunks):
    d_ff = wd_ref.shape[0]
    x1 = x_ref[...] + _dot(on_ref[...], won_ref[...]) + _dot(oh_ref[...], woh_ref[...])
    ms = jnp.mean(x1 * x1, axis=-1, keepdims=True)
    h = (x1 * lax.rsqrt(ms + RMS_EPS) * g_ref[...]).astype(BF16)
    o_ref[...] = x1
    for s, n in chunks:
        gate = _dot(h, wgu_ref[:, s:s + n])
        up = _dot(h, wgu_ref[:, d_ff + s:d_ff + s + n])
        act = (jax.nn.silu(gate) * up).astype(BF16)
        o_ref[...] += _dot(act, wd_ref[s:s + n, :])


def _out_ffn(x2, o_nsa, o_hg, won, woh, g, wgu, wd, tm=256, ff_width=4 * MXU_WIDTH):
    n, d = x2.shape
    d_ff = wd.shape[0]
    row = lambda i: (i, 0)
    const = lambda i: (0, 0)
    return pl.pallas_call(
        functools.partial(_out_ffn_kernel, chunks=_ff_chunks(d_ff, ff_width)),
        grid=(n // tm,),
        in_specs=[
            pl.BlockSpec((tm, d), row),
            pl.BlockSpec((tm, NSA_WIDTH), row),
            pl.BlockSpec((tm, HG_WIDTH), row),
            pl.BlockSpec(won.shape, const),
            pl.BlockSpec(woh.shape, const),
            pl.BlockSpec((1, d), const),
            pl.BlockSpec(wgu.shape, const),
            pl.BlockSpec(wd.shape, const),
        ],
        out_specs=pl.BlockSpec((tm, d), row),
        out_shape=jax.ShapeDtypeStruct((n, d), F32),
        compiler_params=pltpu.CompilerParams(
            dimension_semantics=("arbitrary",), vmem_limit_bytes=VMEM_LIMIT),
        name="out_ffn",
    )(x2, o_nsa, o_hg, won, woh, g, wgu, wd)


def _expand_cmp_weights(pos, w1, w2):
    eye = jnp.eye(NSA_GROUPS, dtype=F32)
    w1r = w1.reshape(CMP_LEN, HEAD_DIM, CMP_HIDDEN)

    def lift(wpart):
        return jnp.einsum('ldc,gk->lgdkc', wpart, eye).reshape(
            CMP_STRIDE * NSA_GROUPS * HEAD_DIM, NSA_GROUPS * CMP_HIDDEN)

    wlo = lift(w1r[:CMP_STRIDE]).astype(BF16)
    whi = lift(w1r[CMP_STRIDE:]).astype(BF16)
    w2x = jnp.einsum('cd,gk->gckd', w2, eye).reshape(NSA_GROUPS * CMP_HIDDEN, NSA_GROUPS * HEAD_DIM).astype(BF16)
    pbias = jnp.tile(pos.reshape(1, CMP_LEN * HEAD_DIM) @ w1, (1, NSA_GROUPS))
    return wlo, whi, w2x, pbias


def _mixers(x, norm_mix, w_in, q_norm, k_norm, cmp_pos_k, cmp_pos_v, cmp_k_w1, cmp_k_w2, cmp_v_w1, cmp_v_w2,
            hgrn_lb_logits, hgrn_o_norm):
    b, t, d = x.shape
    depth = norm_mix.shape[0]
    assert depth == 1 and hgrn_lb_logits.shape[0] == 2
    assert t % KEY_UNIT == 0 and t >= WINDOW + Q_BLOCK and t // SLC_LEN <= LANE
    assert (t // CMP_STRIDE) % KEY_TILE == 0
    l = 0

    o_g, o_h = NSA_WIDTH + 6 * KV_WIDTH, NSA_WIDTH + 6 * KV_WIDTH + N_GATES
    w = w_in[l]
    w_perm = jnp.concatenate([
        w[:, :o_g], w[:, o_h:], w[:, o_g:o_h], jnp.zeros((d, LANE - N_GATES), w.dtype)], axis=1).astype(BF16)

    bd = jnp.asarray(np.kron(np.eye(LANE // HEAD_DIM), np.ones((HEAD_DIM, HEAD_DIM))), BF16)
    tile2 = lambda v: jnp.tile(v.reshape(1, HEAD_DIM), (1, LANE // HEAD_DIM)).astype(F32)
    qg = tile2(q_norm[l]) * (HEAD_DIM ** -0.5 * math.log2(math.e))

    qt, kc_raw, vc_raw, ksa, vst, kw, vwt, gt, ph = _in_proj(
        x, norm_mix[l].reshape(1, d), w_perm, bd, qg, tile2(k_norm[l, 1]), tile2(k_norm[l, 2]))

    ns = t // CMP_STRIDE
    seg_w = CMP_STRIDE * KV_WIDTH
    kc, vct = _compress(
        kc_raw.reshape(b, ns, seg_w), vc_raw.reshape(b, ns, seg_w),
        _expand_cmp_weights(cmp_pos_k[l], cmp_k_w1[l], cmp_k_w2[l]),
        _expand_cmp_weights(cmp_pos_v[l], cmp_v_w1[l], cmp_v_w2[l]),
        bd, tile2(k_norm[l, 0]))

    cs = np.arange(ns)[None, :] * CMP_STRIDE
    ss = np.arange(LANE)[:, None] * SLC_LEN
    ovt = jnp.asarray(((cs < ss + SLC_LEN) & (cs + CMP_LEN > ss)).astype(np.float32), BF16)

    o_nsa = _nsa(qt, ksa, vst, kw, vwt, kc, vct, gt, ovt)
    o_hg = _hgrn(ph, hgrn_lb_logits, hgrn_o_norm[l].reshape(1, HGRN_DIM))
    return o_nsa, o_hg


def kernel(x, norm_mix, w_in, q_norm, k_norm, cmp_pos_k, cmp_pos_v, cmp_k_w1, cmp_k_w2, cmp_v_w1, cmp_v_w2,
           hgrn_lb_logits, hgrn_o_norm, w_out, norm_ffn, w_gate_up, w_down):
    b, t, d = x.shape
    l = 0
    o_nsa, o_hg = _mixers(x, norm_mix, w_in, q_norm, k_norm, cmp_pos_k, cmp_pos_v, cmp_k_w1, cmp_k_w2,
                          cmp_v_w1, cmp_v_w2, hgrn_lb_logits, hgrn_o_norm)
    wo = w_out[l]
    out = _out_ffn(
        x.reshape(b * t, d), o_nsa.reshape(b * t, NSA_WIDTH), o_hg.reshape(b * t, HG_WIDTH),
        wo[:NSA_WIDTH].astype(BF16), wo[NSA_WIDTH:].astype(BF16),
        norm_ffn[l].reshape(1, d), w_gate_up[l].astype(BF16), w_down[l].astype(BF16))
    return out.reshape(b, t, d)
```

```python
import functools
import math

import numpy as np
import jax
import jax.numpy as jnp
from jax import lax
from jax.experimental import pallas as pl
from jax.experimental.pallas import tpu as pltpu

F32 = jnp.float32
BF16 = jnp.bfloat16

LANE = 128
MXU_WIDTH = 256

NSA_HEADS = 8
NSA_GROUPS = 2
NSA_REP = NSA_HEADS // NSA_GROUPS
HEAD_DIM = 64
CMP_LEN = 32
CMP_STRIDE = 16
CMP_HIDDEN = 128
SLC_LEN = 64
N_SELECT = 16
WINDOW = 512
Q_BLOCK = 128
HGRN_HEADS = 4
HGRN_DIM = 128
HGRN_CHUNK = 64
HGRN_SUB = 16
RMS_EPS = 1e-6
NEG = -1e30

NSA_WIDTH = NSA_HEADS * HEAD_DIM
KV_WIDTH = NSA_GROUPS * HEAD_DIM
HG_WIDTH = HGRN_HEADS * HGRN_DIM
N_GATES = 3 * NSA_HEADS

KEY_TILE = 128
KEY_UNIT = 4 * KEY_TILE
N_PAIRS = NSA_HEADS // 2
V_AUG = HEAD_DIM + 16
VMEM_LIMIT = 56 * 1024 * 1024


def _dot(a, b):
    return jnp.dot(a, b, preferred_element_type=F32)


def _dot_t(a, b):
    return lax.dot_general(a, b, (((1,), (1,)), ((), ())), preferred_element_type=F32)


def _split2(x):
    hi = x.astype(BF16)
    lo = (x - hi.astype(F32)).astype(BF16)
    return hi, lo


def _group_sumsq(y, bd):
    hi, lo = _split2(y * y)
    return _dot(hi, bd) + _dot(lo, bd)


def _group_rms(y, bd, gain):
    ss = _group_sumsq(y, bd)
    return y * lax.rsqrt(ss * (1.0 / HEAD_DIM) + RMS_EPS) * gain


def _values_t_aug(v):
    vt = v.T
    ones = jnp.ones((V_AUG - HEAD_DIM, v.shape[0]), F32)
    return jnp.concatenate([vt[0:HEAD_DIM], ones, vt[HEAD_DIM:2 * HEAD_DIM], ones], axis=0).astype(BF16)


def _in_proj_kernel(x_ref, g_ref, w_ref, bd_ref, qg_ref, ksg_ref, kwg_ref,
                    qt_ref, kc_ref, vc_ref, ksa_ref, vst_ref, kw_ref, vwt_ref, gt_ref, ph_ref):
    tm = x_ref.shape[1]
    x = x_ref[0]
    ms = jnp.mean(x * x, axis=-1, keepdims=True)
    h = (x * lax.rsqrt(ms + RMS_EPS) * g_ref[...]).astype(BF16)
    y = _dot(h, w_ref[...])
    bd = bd_ref[...]
    for r in range(NSA_WIDTH // LANE):
        sl = slice(r * LANE, (r + 1) * LANE)
        qt_ref[0, sl, :] = _group_rms(y[:, sl], bd, qg_ref[...]).T.astype(BF16)
    o = NSA_WIDTH
    kc_ref[0] = y[:, o:o + LANE]
    vc_ref[0] = y[:, o + LANE:o + 2 * LANE]
    ksa_ref[0, :, 0:LANE] = _group_rms(y[:, o + 2 * LANE:o + 3 * LANE], bd, ksg_ref[...]).astype(BF16)
    key = pl.program_id(1) * tm + lax.broadcasted_iota(jnp.int32, (tm, LANE), 0)
    blk = lax.broadcasted_iota(jnp.int32, (tm, LANE), 1)
    ksa_ref[0, :, LANE:2 * LANE] = jnp.where(lax.shift_right_logical(key, 6) == blk, 1.0, 0.0).astype(BF16)
    vst_ref[0] = _values_t_aug(y[:, o + 3 * LANE:o + 4 * LANE])
    kw_ref[0] = _group_rms(y[:, o + 4 * LANE:o + 5 * LANE], bd, kwg_ref[...]).astype(BF16)
    vwt_ref[0] = _values_t_aug(y[:, o + 5 * LANE:o + 6 * LANE])
    o = NSA_WIDTH + 6 * LANE
    ph_ref[0] = y[:, o:o + 4 * HG_WIDTH]
    o = o + 4 * HG_WIDTH
    gt_ref[0] = jax.nn.sigmoid(y[:, o:o + LANE]).T


def _in_proj(x, g, w, bd, qg, ksg, kwg, tm=256):
    b, t, d = x.shape
    nw = w.shape[1]
    rows = lambda i, j: (i, j, 0)
    cols = lambda i, j: (i, 0, j)
    const = lambda i, j: (0, 0)
    outs = [
        (jax.ShapeDtypeStruct((b, NSA_WIDTH, t), BF16), pl.BlockSpec((1, NSA_WIDTH, tm), cols)),
        (jax.ShapeDtypeStruct((b, t, LANE), F32), pl.BlockSpec((1, tm, LANE), rows)),
        (jax.ShapeDtypeStruct((b, t, LANE), F32), pl.BlockSpec((1, tm, LANE), rows)),
        (jax.ShapeDtypeStruct((b, t, 2 * LANE), BF16), pl.BlockSpec((1, tm, 2 * LANE), rows)),
        (jax.ShapeDtypeStruct((b, 2 * V_AUG, t), BF16), pl.BlockSpec((1, 2 * V_AUG, tm), cols)),
        (jax.ShapeDtypeStruct((b, t, LANE), BF16), pl.BlockSpec((1, tm, LANE), rows)),
        (jax.ShapeDtypeStruct((b, 2 * V_AUG, t), BF16), pl.BlockSpec((1, 2 * V_AUG, tm), cols)),
        (jax.ShapeDtypeStruct((b, LANE, t), F32), pl.BlockSpec((1, LANE, tm), cols)),
        (jax.ShapeDtypeStruct((b, t, 4 * HG_WIDTH), F32), pl.BlockSpec((1, tm, 4 * HG_WIDTH), rows)),
    ]
    return pl.pallas_call(
        _in_proj_kernel,
        grid=(b, t // tm),
        in_specs=[
            pl.BlockSpec((1, tm, d), rows),
            pl.BlockSpec((1, d), const),
            pl.BlockSpec((d, nw), const),
            pl.BlockSpec((LANE, LANE), const),
            pl.BlockSpec((1, LANE), const),
            pl.BlockSpec((1, LANE), const),
            pl.BlockSpec((1, LANE), const),
        ],
        out_specs=[s for _, s in outs],
        out_shape=[s for s, _ in outs],
        compiler_params=pltpu.CompilerParams(
            dimension_semantics=("arbitrary", "arbitrary"), vmem_limit_bytes=VMEM_LIMIT),
        name="in_proj",
    )(x, g, w, bd, qg, ksg, kwg)


def _compress_kernel(xk_ref, xv_ref, wklo_ref, wkhi_ref, wk2_ref, pk_ref,
                     wvlo_ref, wvhi_ref, wv2_ref, pv_ref, bd_ref, kg_ref, kc_ref, vct_ref):
    ns = xk_ref.shape[1]

    def mlp(x_ref, wlo_ref, whi_ref, w2_ref, p_ref):
        xb = x_ref[0].astype(BF16)
        a = _dot(xb, wlo_ref[...])
        b = _dot(xb, whi_ref[...])
        h = a + pltpu.roll(b, ns - 1, axis=0) + p_ref[...]
        return _dot(jax.nn.gelu(h).astype(BF16), w2_ref[...])

    kc = mlp(xk_ref, wklo_ref, wkhi_ref, wk2_ref, pk_ref)
    kc_ref[0] = _group_rms(kc, bd_ref[...], kg_ref[...]).astype(BF16)
    vct_ref[0] = _values_t_aug(mlp(xv_ref, wvlo_ref, wvhi_ref, wv2_ref, pv_ref))


def _compress(xk, xv, wk, wv, bd, kg):
    b, ns, wd = xk.shape
    const2 = lambda i: (0, 0)
    bat = lambda i: (i, 0, 0)
    wspecs = [pl.BlockSpec(w.shape, const2) for w in wk] + [pl.BlockSpec(w.shape, const2) for w in wv]
    return pl.pallas_call(
        _compress_kernel,
        grid=(b,),
        in_specs=[pl.BlockSpec((1, ns, wd), bat), pl.BlockSpec((1, ns, wd), bat)] + wspecs + [
            pl.BlockSpec((LANE, LANE), const2), pl.BlockSpec((1, LANE), const2)],
        out_specs=[pl.BlockSpec((1, ns, LANE), bat), pl.BlockSpec((1, 2 * V_AUG, ns), bat)],
        out_shape=[jax.ShapeDtypeStruct((b, ns, LANE), BF16), jax.ShapeDtypeStruct((b, 2 * V_AUG, ns), BF16)],
        compiler_params=pltpu.CompilerParams(
            dimension_semantics=("arbitrary",), vmem_limit_bytes=VMEM_LIMIT),
        name="compress",
    )(xk, xv, *wk, *wv, bd, kg)


def _topk_bias_t(imp_ts, s0):
    nj, nq = imp_ts[0].shape
    jidx = lax.broadcasted_iota(jnp.int32, (nj, nq), 0)
    tq = s0 + lax.broadcasted_iota(jnp.int32, (nj, nq), 1)
    jcur = lax.shift_right_logical(tq, 6)
    forced = (jidx == 0) | (jidx == jcur) | (jidx == jcur - 1)
    future = jidx > jcur
    vs = [jnp.where(forced, -2.0, jnp.where(future, -1.0, imp_t)) for imp_t in imp_ts]
    jf = jidx.astype(F32)
    for _ in range(N_SELECT - 3):
        for g in range(len(vs)):
            m = jnp.max(vs[g], axis=0, keepdims=True)
            jm = jnp.min(jnp.where(vs[g] == m, jf, float(nj)), axis=0, keepdims=True)
            vs[g] = jnp.where(jf == jm, -2.0, vs[g])
    return [jnp.where(future, NEG, jnp.where(v < -1.5, 0.0, NEG)) for v in vs]


def _nsa_kernel(qt_ref, qtn_ref, ksa_ref, vst_ref, kw_ref, vwt_ref, kc_ref, vct_ref, gt_ref, gtn_ref, ovt_ref, o_ref,
                w_ref, wn_ref, bias_ref, cmp_ref, sc_ref, sw_ref, ss_ref, mx_ref, e_ref, ew_ref,
                m_ref, acc_ref, out_ref):
    qb = pl.program_id(1)
    s0 = qb * Q_BLOCK
    nc = kc_ref.shape[1]
    two = 2 * Q_BLOCK
    lane_q = lax.broadcasted_iota(jnp.int32, (1, two), 1) & (Q_BLOCK - 1)
    t_lane = s0 + lane_q
    gt = gt_ref[0]

    def stationary_q(q_ref, dst):
        zero = jnp.zeros((HEAD_DIM, Q_BLOCK), BF16)
        for p in range(N_PAIRS):
            g = (2 * p) // NSA_REP
            halves = []
            for h in (2 * p, 2 * p + 1):
                qh = q_ref[0, h * HEAD_DIM:(h + 1) * HEAD_DIM, :]
                halves.append(jnp.concatenate([qh, zero] if g == 0 else [zero, qh], axis=0))
            dst[p, 0:LANE, :] = jnp.concatenate(halves, axis=1)

    def store_scores(dst, s, valid):
        if valid is not None:
            s = jnp.where(valid, s, NEG)
        dst[...] = s
        return jnp.max(s, axis=0, keepdims=True)

    def exp_tiles(load, n_tiles, m, e_dst):
        for c in range(n_tiles):
            rows = slice(c * KEY_TILE, (c + 1) * KEY_TILE)
            e_dst[rows, :] = jnp.exp2(load(rows) - m).astype(BF16)

    def values_t(ref, g, cols):
        return ref[0, g * V_AUG:(g + 1) * V_AUG, cols]

    def emit(dst, gates, p, branch, o_aug, first, guard=None):
        inv = 1.0 / o_aug[HEAD_DIM:HEAD_DIM + 1, :]
        if guard is not None:
            inv = jnp.where(guard, inv, 0.0)
        for hh in range(2):
            h = 2 * p + hh
            ls = slice(hh * Q_BLOCK, (hh + 1) * Q_BLOCK)
            o = o_aug[0:HEAD_DIM, ls] * (inv[:, ls] * gates[3 * h + branch:3 * h + branch + 1, :])
            rs = slice(h * HEAD_DIM, (h + 1) * HEAD_DIM)
            if first:
                dst[rs, :] = o
            else:
                dst[rs, :] += o
        return inv

    def key_rows(n):
        return lax.broadcasted_iota(jnp.int32, (n, two), 0)

    def cmp_scores(q_ref, s0x):
        stationary_q(q_ref, wn_ref)
        kc = kc_ref[0]
        valid = key_rows(nc) <= lax.shift_right_arithmetic(s0x + lane_q - (CMP_LEN - 1), 4)
        return [store_scores(sc_ref.at[p], _dot(kc, wn_ref[p]), valid) for p in range(N_PAIRS)]

    def cmp_finish(gates, m_cmp):
        imp_t = [jnp.zeros((LANE, Q_BLOCK), F32) for _ in range(NSA_GROUPS)]
        for p in range(N_PAIRS):
            exp_tiles(lambda rows: sc_ref[p, rows, :], nc // KEY_TILE, m_cmp[p], e_ref.at[p])
        for p in range(N_PAIRS):
            g = (2 * p) // NSA_REP
            inv = emit(cmp_ref, gates, p, 0, _dot(values_t(vct_ref, g, slice(None)), e_ref[p, 0:nc, :]), True,
                       guard=m_cmp[p] > 0.5 * NEG)
            imp_p = _dot(ovt_ref[...], e_ref[p, 0:nc, :]) * inv
            imp_t[g] = imp_t[g] + imp_p[:, 0:Q_BLOCK] + imp_p[:, Q_BLOCK:two]
        return imp_t

    def select(imp_t, s0x):
        for g, bias_t in enumerate(_topk_bias_t(imp_t, s0x)):
            bias_ref[g] = bias_t.astype(BF16)

    @pl.when(qb == 0)
    def _():
        select(cmp_finish(gt, cmp_scores(qt_ref, s0)), s0)

    w0 = pl.multiple_of(jnp.maximum(s0 - WINDOW, 0), KEY_TILE)
    wlen = WINDOW + Q_BLOCK
    kwin = kw_ref[0, pl.ds(w0, wlen), :]
    vwin = [values_t(vwt_ref, g, pl.ds(w0, wlen)) for g in range(NSA_GROUPS)]

    stationary_q(qt_ref, w_ref)
    for p in range(N_PAIRS):
        bias_t = bias_ref[(2 * p) // NSA_REP]
        w_ref[p, LANE:2 * LANE, :] = jnp.concatenate([bias_t, bias_t], axis=1)
    out_ref[...] = cmp_ref[...]

    imp_t = cmp_finish(gtn_ref[0], cmp_scores(qtn_ref, s0 + Q_BLOCK))
    diff = (t_lane - w0) - key_rows(wlen)
    win_valid = lax.shift_right_arithmetic(diff, WINDOW.bit_length() - 1) == 0
    m_win = [store_scores(sw_ref.at[p], _dot(kwin, w_ref[p, 0:LANE, :]), win_valid) for p in range(N_PAIRS)]
    for p in range(N_PAIRS):
        exp_tiles(lambda rows: sw_ref[p, rows, :], wlen // KEY_TILE, m_win[p], ew_ref.at[p])
    for p in range(N_PAIRS):
        emit(out_ref, gt, p, 2, _dot(vwin[(2 * p) // NSA_REP], ew_ref[p]), False)
    select(imp_t, s0 + Q_BLOCK)

    m_ref[...] = jnp.full(m_ref.shape, NEG, F32)
    acc_ref[...] = jnp.zeros(acc_ref.shape, F32)

    def sel_scores(u, buf, p, causal):
        k0 = pl.multiple_of(u * KEY_UNIT, KEY_UNIT)
        ku = ksa_ref[0, pl.ds(k0, KEY_UNIT), :]
        valid = (k0 + key_rows(KEY_UNIT)) <= t_lane if causal else None
        mx_ref[buf, p] = store_scores(ss_ref.at[buf, p], _dot(ku, w_ref[p]), valid)

    def sel_update(u, buf, p):
        k0 = pl.multiple_of(u * KEY_UNIT, KEY_UNIT)
        g = (2 * p) // NSA_REP
        m_old = m_ref[p]
        m = jnp.maximum(m_old, mx_ref[buf, p])
        exp_tiles(lambda rows: ss_ref[buf, p, rows, :], KEY_UNIT // KEY_TILE, m, e_ref.at[p])
        m_ref[p] = m
        acc_ref[p] = jnp.exp2(m_old - m) * acc_ref[p] + _dot(
            values_t(vst_ref, g, pl.ds(k0, KEY_UNIT)), e_ref[p, 0:KEY_UNIT, :])

    def sel_step(u, buf, next_causal):
        for p in range(N_PAIRS):
            sel_scores(u + 1, 1 - buf, p, next_causal)
            sel_update(u, buf, p)

    def sel_last(u, buf):
        for p in range(N_PAIRS):
            sel_update(u, buf, p)

    n_before = qb // (KEY_UNIT // Q_BLOCK)
    n_trips = jnp.maximum(n_before - 1, 0) // 2
    rest = 2 * n_trips

    for p in range(N_PAIRS):
        sel_scores(0, 0, p, False)

    def body(i, carry):
        sel_step(2 * i, 0, False)
        sel_step(2 * i + 1, 1, False)
        return carry

    lax.fori_loop(0, n_trips, body, 0)

    @pl.when(n_before == 0)
    def _():
        for p in range(N_PAIRS):
            sel_scores(0, 0, p, True)
        sel_last(0, 0)

    @pl.when(n_before == rest + 1)
    def _():
        sel_step(rest, 0, True)
        sel_last(rest + 1, 1)

    @pl.when(n_before == rest + 2)
    def _():
        sel_step(rest, 0, False)
        sel_step(rest + 1, 1, True)
        sel_last(rest + 2, 0)

    for p in range(N_PAIRS):
        emit(out_ref, gt, p, 1, acc_ref[p], False)

    o_ref[0] = out_ref[...].T.astype(BF16)


def _nsa(qt, ksa, vst, kw, vwt, kc, vct, gt, ovt):
    b, _, t = qt.shape
    nc = kc.shape[1]
    n_qb = t // Q_BLOCK
    qcol = lambda i, j: (i, 0, j)
    qnext = lambda i, j: (i, 0, jnp.minimum(j + 1, n_qb - 1))
    full = lambda i, j: (i, 0, 0)
    const = lambda i, j: (0, 0)
    return pl.pallas_call(
        _nsa_kernel,
        grid=(b, n_qb),
        in_specs=[
            pl.BlockSpec((1, NSA_WIDTH, Q_BLOCK), qcol),
            pl.BlockSpec((1, NSA_WIDTH, Q_BLOCK), qnext),
            pl.BlockSpec((1, t, 2 * LANE), full),
            pl.BlockSpec((1, 2 * V_AUG, t), full),
            pl.BlockSpec((1, t, LANE), full),
            pl.BlockSpec((1, 2 * V_AUG, t), full),
            pl.BlockSpec((1, nc, LANE), full),
            pl.BlockSpec((1, 2 * V_AUG, nc), full),
            pl.BlockSpec((1, LANE, Q_BLOCK), qcol),
            pl.BlockSpec((1, LANE, Q_BLOCK), qnext),
            pl.BlockSpec(ovt.shape, const),
        ],
        out_specs=pl.BlockSpec((1, Q_BLOCK, NSA_WIDTH), lambda i, j: (i, j, 0)),
        out_shape=jax.ShapeDtypeStruct((b, t, NSA_WIDTH), BF16),
        scratch_shapes=[
            pltpu.VMEM((N_PAIRS, 2 * LANE, 2 * Q_BLOCK), BF16),
            pltpu.VMEM((N_PAIRS, LANE, 2 * Q_BLOCK), BF16),
            pltpu.VMEM((NSA_GROUPS, LANE, Q_BLOCK), BF16),
            pltpu.VMEM((NSA_WIDTH, Q_BLOCK), F32),
            pltpu.VMEM((N_PAIRS, nc, 2 * Q_BLOCK), F32),
            pltpu.VMEM((N_PAIRS, WINDOW + Q_BLOCK, 2 * Q_BLOCK), F32),
            pltpu.VMEM((2, N_PAIRS, KEY_UNIT, 2 * Q_BLOCK), F32),
            pltpu.VMEM((2, N_PAIRS, 1, 2 * Q_BLOCK), F32),
            pltpu.VMEM((N_PAIRS, max(nc, KEY_UNIT), 2 * Q_BLOCK), BF16),
            pltpu.VMEM((N_PAIRS, WINDOW + Q_BLOCK, 2 * Q_BLOCK), BF16),
            pltpu.VMEM((N_PAIRS, 1, 2 * Q_BLOCK), F32),
            pltpu.VMEM((N_PAIRS, V_AUG, 2 * Q_BLOCK), F32),
            pltpu.VMEM((NSA_WIDTH, Q_BLOCK), F32),
        ],
        compiler_params=pltpu.CompilerParams(
            dimension_semantics=("arbitrary", "arbitrary"), vmem_limit_bytes=VMEM_LIMIT),
        name="nsa",
    )(qt, qt, ksa, vst, kw, vwt, kc, vct, gt, gt, ovt)


def _hgrn_kernel(ph_ref, lb_ref, og_ref, o_ref, st_ref, *, chunks):
    c_len = HGRN_CHUNK
    n_sub = c_len // HGRN_SUB

    @pl.when(pl.program_id(1) == 0)
    def _():
        st_ref[...] = jnp.zeros_like(st_ref)

    lg = lb_ref[...]
    lmax = jnp.max(lg, axis=0, keepdims=True)
    le = jnp.exp(lg - lmax)
    lb_all = le[0:1] / jnp.sum(le, axis=0, keepdims=True)

    ri = lax.broadcasted_iota(jnp.int32, (c_len, c_len), 0)
    ci = lax.broadcasted_iota(jnp.int32, (c_len, c_len), 1)
    tri = (ci <= ri).astype(BF16)
    causal = ci <= ri
    rowi = lax.broadcasted_iota(jnp.int32, (c_len, HG_WIDTH), 0)
    gain = og_ref[...]
    heads = [slice(h * HGRN_DIM, (h + 1) * HGRN_DIM) for h in range(HGRN_HEADS)]

    pre = []
    for c in range(chunks):
        rs = slice(c * c_len, (c + 1) * c_len)
        hq = ph_ref[0, rs, 0:HG_WIDTH]
        hf = ph_ref[0, rs, HG_WIDTH:2 * HG_WIDTH]
        qv = jax.nn.silu(hq)
        f = lb_all + (1.0 - lb_all) * jax.nn.sigmoid(hf)
        logf = jnp.log(f)
        kv = 1.0 - f
        g1 = logf.astype(BF16)
        r1 = logf - g1.astype(F32)
        g2 = r1.astype(BF16)
        g3 = (r1 - g2.astype(F32)).astype(BF16)
        bcum = _dot(tri, g1) + _dot(tri, g2) + _dot(tri, g3)
        pre.append((qv, kv, bcum))

    mid = []
    for c in range(chunks):
        qv, kv, bcum = pre[c]
        b_last = bcum[c_len - 1:c_len]
        starts = [jnp.zeros((1, HG_WIDTH), F32)] + [bcum[i * HGRN_SUB - 1:i * HGRN_SUB] for i in range(1, n_sub)]
        ends = starts[1:] + [b_last]

        def per_sub(rows):
            return jnp.concatenate([jnp.broadcast_to(r, (HGRN_SUB, HG_WIDTH)) for r in rows], axis=0)

        start_full = per_sub(starts)
        qd = qv * jnp.exp(bcum - start_full)
        kb = kv * jnp.exp(per_sub(ends) - bcum)
        kdiag = kv * jnp.exp(start_full - bcum)
        qs = (qd * per_sub([jnp.exp(s) for s in starts])).astype(BF16)
        kdec = (kb * per_sub([jnp.exp(b_last - e) for e in ends])).astype(BF16)
        kds = []
        for i in range(n_sub):
            blocks = [kb[j * HGRN_SUB:(j + 1) * HGRN_SUB] * jnp.exp(starts[i] - ends[j]) for j in range(i - 1)]
            if i > 0:
                blocks.append(kb[(i - 1) * HGRN_SUB:i * HGRN_SUB])
            blocks.append(kdiag[i * HGRN_SUB:(i + 1) * HGRN_SUB])
            if i + 1 < n_sub:
                blocks.append(jnp.zeros(((n_sub - 1 - i) * HGRN_SUB, HG_WIDTH), F32))
            kds.append(jnp.concatenate(blocks, axis=0).astype(BF16))
        mid.append((qd.astype(BF16), kds, qs, kdec, jnp.exp(b_last)))

    attn = {}
    upd = {}
    for c in range(chunks):
        rs = slice(c * c_len, (c + 1) * c_len)
        qd, kds, qs, kdec, dec = mid[c]
        for h, hs in enumerate(heads):
            parts = [_dot_t(qd[i * HGRN_SUB:(i + 1) * HGRN_SUB, hs], kds[i][:, hs]) for i in range(n_sub)]
            attn[c, h] = jnp.where(causal, jnp.concatenate(parts, axis=0), 0.0).astype(BF16)
            hi = ph_ref[0, rs, 2 * HG_WIDTH + h * HGRN_DIM:2 * HG_WIDTH + (h + 1) * HGRN_DIM]
            upd[c, h] = _dot(hi.T.astype(BF16), kdec[:, hs])

    state = {}
    for h, hs in enumerate(heads):
        st = st_ref[h]
        for c in range(chunks):
            state[c, h] = st.astype(BF16)
            st = st * mid[c][4][:, hs] + upd[c, h]
        st_ref[h] = st

    for c in range(chunks):
        rs = slice(c * c_len, (c + 1) * c_len)
        qs = mid[c][2]
        for h, hs in enumerate(heads):
            vb = ph_ref[0, rs, 2 * HG_WIDTH + h * HGRN_DIM:2 * HG_WIDTH + (h + 1) * HGRN_DIM].astype(BF16)
            hg = ph_ref[0, rs, 3 * HG_WIDTH + h * HGRN_DIM:3 * HG_WIDTH + (h + 1) * HGRN_DIM]
            o = _dot(attn[c, h], vb) + _dot_t(qs[:, hs], state[c, h])
            ms = jnp.mean(o * o, axis=-1, keepdims=True)
            o = o * lax.rsqrt(ms + RMS_EPS) * gain * jax.nn.silu(hg)
            o_ref[0, rs, hs] = o.astype(BF16)


def _hgrn(ph, lb_logits, o_gain, tt=256):
    b, t, _ = ph.shape
    blk = lambda i, j: (i, j, 0)
    const = lambda i, j: (0, 0)
    return pl.pallas_call(
        functools.partial(_hgrn_kernel, chunks=tt // HGRN_CHUNK),
        grid=(b, t // tt),
        in_specs=[
            pl.BlockSpec((1, tt, 4 * HG_WIDTH), blk),
            pl.BlockSpec(lb_logits.shape, const),
            pl.BlockSpec((1, HGRN_DIM), const),
        ],
        out_specs=pl.BlockSpec((1, tt, HG_WIDTH), blk),
        out_shape=jax.ShapeDtypeStruct((b, t, HG_WIDTH), BF16),
        scratch_shapes=[pltpu.VMEM((HGRN_HEADS, HGRN_DIM, HGRN_DIM), F32)],
        compiler_params=pltpu.CompilerParams(
            dimension_semantics=("arbitrary", "arbitrary"), vmem_limit_bytes=VMEM_LIMIT),
        name="hgrn2",
    )(ph, lb_logits, o_gain)


def _ff_chunks(d_ff, width):
    return tuple((s, min(width, d_ff - s)) for s in range(0, d_ff, width))


def _out_ffn_kernel(x_ref, on_ref, oh_ref, won_ref, woh_ref, g_ref, wgu_ref, wd_ref, o_ref, *, chunks):
    d_ff = wd_ref.shape[0]
    x1 = x_ref[...] + _dot(on_ref[...], won_ref[...]) + _dot(oh_ref[...], woh_ref[...])
    ms = jnp.mean(x1 * x1, axis=-1, keepdims=True)
    h = (x1 * lax.rsqrt(ms + RMS_EPS) * g_ref[...]).astype(BF16)
    o_ref[...] = x1
    for s, n in chunks:
        gate = _dot(h, wgu_ref[:, s:s + n])
        up = _dot(h, wgu_ref[:, d_ff + s:d_ff + s + n])
        act = (jax.nn.silu(gate) * up).astype(BF16)
        o_ref[...] += _dot(act, wd_ref[s:s + n, :])


def _out_ffn(x2, o_nsa, o_hg, won, woh, g, wgu, wd, tm=256, ff_width=4 * MXU_WIDTH):
    n, d = x2.shape
    d_ff = wd.shape[0]
    row = lambda i: (i, 0)
    const = lambda i: (0, 0)
    return pl.pallas_call(
        functools.partial(_out_ffn_kernel, chunks=_ff_chunks(d_ff, ff_width)),
        grid=(n // tm,),
        in_specs=[
            pl.BlockSpec((tm, d), row),
            pl.BlockSpec((tm, NSA_WIDTH), row),
            pl.BlockSpec((tm, HG_WIDTH), row),
            pl.BlockSpec(won.shape, const),
            pl.BlockSpec(woh.shape, const),
            pl.BlockSpec((1, d), const),
            pl.BlockSpec(wgu.shape, const),
            pl.BlockSpec(wd.shape, const),
        ],
        out_specs=pl.BlockSpec((tm, d), row),
        out_shape=jax.ShapeDtypeStruct((n, d), F32),
        compiler_params=pltpu.CompilerParams(
            dimension_semantics=("arbitrary",), vmem_limit_bytes=VMEM_LIMIT),
        name="out_ffn",
    )(x2, o_nsa, o_hg, won, woh, g, wgu, wd)


def _expand_cmp_weights(pos, w1, w2):
    eye = jnp.eye(NSA_GROUPS, dtype=F32)
    w1r = w1.reshape(CMP_LEN, HEAD_DIM, CMP_HIDDEN)

    def lift(wpart):
        return jnp.einsum('ldc,gk->lgdkc', wpart, eye).reshape(
            CMP_STRIDE * NSA_GROUPS * HEAD_DIM, NSA_GROUPS * CMP_HIDDEN)

    wlo = lift(w1r[:CMP_STRIDE]).astype(BF16)
    whi = lift(w1r[CMP_STRIDE:]).astype(BF16)
    w2x = jnp.einsum('cd,gk->gckd', w2, eye).reshape(NSA_GROUPS * CMP_HIDDEN, NSA_GROUPS * HEAD_DIM).astype(BF16)
    pbias = jnp.tile(pos.reshape(1, CMP_LEN * HEAD_DIM) @ w1, (1, NSA_GROUPS))
    return wlo, whi, w2x, pbias


def _mixers(x, norm_mix, w_in, q_norm, k_norm, cmp_pos_k, cmp_pos_v, cmp_k_w1, cmp_k_w2, cmp_v_w1, cmp_v_w2,
            hgrn_lb_logits, hgrn_o_norm):
    b, t, d = x.shape
    depth = norm_mix.shape[0]
    assert depth == 1 and hgrn_lb_logits.shape[0] == 2
    assert t % KEY_UNIT == 0 and t >= WINDOW + Q_BLOCK and t // SLC_LEN <= LANE
    assert (t // CMP_STRIDE) % KEY_TILE == 0
    l = 0

    o_g, o_h = NSA_WIDTH + 6 * KV_WIDTH, NSA_WIDTH + 6 * KV_WIDTH + N_GATES
    w = w_in[l]
    w_perm = jnp.concatenate([
        w[:, :o_g], w[:, o_h:], w[:, o_g:o_h], jnp.zeros((d, LANE - N_GATES), w.dtype)], axis=1).astype(BF16)

    bd = jnp.asarray(np.kron(np.eye(LANE // HEAD_DIM), np.ones((HEAD_DIM, HEAD_DIM))), BF16)
    tile2 = lambda v: jnp.tile(v.reshape(1, HEAD_DIM), (1, LANE // HEAD_DIM)).astype(F32)
    qg = tile2(q_norm[l]) * (HEAD_DIM ** -0.5 * math.log2(math.e))

    qt, kc_raw, vc_raw, ksa, vst, kw, vwt, gt, ph = _in_proj(
        x, norm_mix[l].reshape(1, d), w_perm, bd, qg, tile2(k_norm[l, 1]), tile2(k_norm[l, 2]))

    ns = t // CMP_STRIDE
    seg_w = CMP_STRIDE * KV_WIDTH
    kc, vct = _compress(
        kc_raw.reshape(b, ns, seg_w), vc_raw.reshape(b, ns, seg_w),
        _expand_cmp_weights(cmp_pos_k[l], cmp_k_w1[l], cmp_k_w2[l]),
        _expand_cmp_weights(cmp_pos_v[l], cmp_v_w1[l], cmp_v_w2[l]),
        bd, tile2(k_norm[l, 0]))

    cs = np.arange(ns)[None, :] * CMP_STRIDE
    ss = np.arange(LANE)[:, None] * SLC_LEN
    ovt = jnp.asarray(((cs < ss + SLC_LEN) & (cs + CMP_LEN > ss)).astype(np.float32), BF16)

    o_nsa = _nsa(qt, ksa, vst, kw, vwt, kc, vct, gt, ovt)
    o_hg = _hgrn(ph, hgrn_lb_logits, hgrn_o_norm[l].reshape(1, HGRN_DIM))
    return o_nsa, o_hg


def kernel(x, norm_mix, w_in, q_norm, k_norm, cmp_pos_k, cmp_pos_v, cmp_k_w1, cmp_k_w2, cmp_v_w1, cmp_v_w2,
           hgrn_lb_logits, hgrn_o_norm, w_out, norm_ffn, w_gate_up, w_down):
    b, t, d = x.shape
    l = 0
    o_nsa, o_hg = _mixers(x, norm_mix, w_in, q_norm, k_norm, cmp_pos_k, cmp_pos_v, cmp_k_w1, cmp_k_w2,
                          cmp_v_w1, cmp_v_w2, hgrn_lb_logits, hgrn_o_norm)
    wo = w_out[l]
    out = _out_ffn(
        x.reshape(b * t, d), o_nsa.reshape(b * t, NSA_WIDTH), o_hg.reshape(b * t, HG_WIDTH),
        wo[:NSA_WIDTH].astype(BF16), wo[NSA_WIDTH:].astype(BF16),
        norm_ffn[l].reshape(1, d), w_gate_up[l].astype(BF16), w_down[l].astype(BF16))
    return out.reshape(b, t, d)
```

```python
import functools
import math

import numpy as np
import jax
import jax.numpy as jnp
from jax import lax
from jax.experimental import pallas as pl
from jax.experimental.pallas import tpu as pltpu

F32 = jnp.float32
BF16 = jnp.bfloat16

LANE = 128
MXU_WIDTH = 256

NSA_HEADS = 8
NSA_GROUPS = 2
NSA_REP = NSA_HEADS // NSA_GROUPS
HEAD_DIM = 64
CMP_LEN = 32
CMP_STRIDE = 16
CMP_HIDDEN = 128
SLC_LEN = 64
N_SELECT = 16
WINDOW = 512
Q_BLOCK = 128
HGRN_HEADS = 4
HGRN_DIM = 128
HGRN_CHUNK = 64
HGRN_SUB = 16
RMS_EPS = 1e-6
NEG = -1e30

NSA_WIDTH = NSA_HEADS * HEAD_DIM
KV_WIDTH = NSA_GROUPS * HEAD_DIM
HG_WIDTH = HGRN_HEADS * HGRN_DIM
N_GATES = 3 * NSA_HEADS

KEY_TILE = 128
KEY_UNIT = 4 * KEY_TILE
N_PAIRS = NSA_HEADS // 2
V_AUG = HEAD_DIM + 16
VMEM_LIMIT = 56 * 1024 * 1024


def _dot(a, b):
    return jnp.dot(a, b, preferred_element_type=F32)


def _dot_t(a, b):
    return lax.dot_general(a, b, (((1,), (1,)), ((), ())), preferred_element_type=F32)


def _group_rms(y, bd, gain):
    ss = _dot((y * y).astype(BF16), bd)
    return y * lax.rsqrt(ss * (1.0 / HEAD_DIM) + RMS_EPS) * gain


def _values_t_aug(v):
    vt = v.T
    ones = jnp.ones((V_AUG - HEAD_DIM, v.shape[0]), F32)
    return jnp.concatenate([vt[0:HEAD_DIM], ones, vt[HEAD_DIM:2 * HEAD_DIM], ones], axis=0).astype(BF16)


def _in_proj_kernel(x_ref, g_ref, w_ref, bd_ref, qg_ref, ksg_ref, kwg_ref,
                    qt_ref, kc_ref, vc_ref, ksa_ref, vst_ref, kw_ref, vwt_ref, gt_ref, ph_ref, wp_ref):
    tm = x_ref.shape[1]
    d = x_ref.shape[2]

    @pl.when((pl.program_id(0) == 0) & (pl.program_id(1) == 0))
    def _():
        o_g = NSA_WIDTH + 6 * KV_WIDTH
        o_h = o_g + N_GATES
        rows_per = 128

        def copy_rows(i, carry):
            rs = pl.ds(pl.multiple_of(i * rows_per, rows_per), rows_per)
            wp_ref[rs, 0:o_g] = w_ref[rs, 0:o_g].astype(BF16)
            wp_ref[rs, o_g:o_g + 4 * HG_WIDTH] = w_ref[rs, o_h:o_h + 4 * HG_WIDTH].astype(BF16)
            tail = jnp.concatenate(
                [w_ref[rs, o_g:o_h], jnp.zeros((rows_per, LANE - N_GATES), F32)], axis=1)
            wp_ref[rs, o_g + 4 * HG_WIDTH:o_g + 4 * HG_WIDTH + LANE] = tail.astype(BF16)
            return carry

        lax.fori_loop(0, d // rows_per, copy_rows, 0)

    x = x_ref[0]
    ms = jnp.mean(x * x, axis=-1, keepdims=True)
    h = (x * lax.rsqrt(ms + RMS_EPS) * g_ref[...]).astype(BF16)
    y = _dot(h, wp_ref[...])
    bd = bd_ref[...]
    for r in range(NSA_WIDTH // LANE):
        sl = slice(r * LANE, (r + 1) * LANE)
        qt_ref[0, sl, :] = _group_rms(y[:, sl], bd, qg_ref[...]).T.astype(BF16)
    o = NSA_WIDTH
    kc_ref[0] = y[:, o:o + LANE].astype(BF16)
    vc_ref[0] = y[:, o + LANE:o + 2 * LANE].astype(BF16)
    ksa_ref[0, :, 0:LANE] = _group_rms(y[:, o + 2 * LANE:o + 3 * LANE], bd, ksg_ref[...]).astype(BF16)
    key = pl.program_id(1) * tm + lax.broadcasted_iota(jnp.int32, (tm, LANE), 0)
    blk = lax.broadcasted_iota(jnp.int32, (tm, LANE), 1)
    ksa_ref[0, :, LANE:2 * LANE] = jnp.where(lax.shift_right_logical(key, 6) == blk, 1.0, 0.0).astype(BF16)
    vst_ref[0] = _values_t_aug(y[:, o + 3 * LANE:o + 4 * LANE])
    kw_ref[0] = _group_rms(y[:, o + 4 * LANE:o + 5 * LANE], bd, kwg_ref[...]).astype(BF16)
    vwt_ref[0] = _values_t_aug(y[:, o + 5 * LANE:o + 6 * LANE])
    o = NSA_WIDTH + 6 * LANE
    ph_ref[0] = y[:, o:o + 4 * HG_WIDTH]
    o = o + 4 * HG_WIDTH
    gt_ref[0] = jax.nn.sigmoid(y[:, o:o + LANE]).T


def _in_proj(x, g, w, bd, qg, ksg, kwg, tm=256):
    b, t, d = x.shape
    nw = NSA_WIDTH + 6 * KV_WIDTH + 4 * HG_WIDTH + LANE
    rows = lambda i, j: (i, j, 0)
    cols = lambda i, j: (i, 0, j)
    const = lambda i, j: (0, 0)
    outs = [
        (jax.ShapeDtypeStruct((b, NSA_WIDTH, t), BF16), pl.BlockSpec((1, NSA_WIDTH, tm), cols)),
        (jax.ShapeDtypeStruct((b, t, LANE), BF16), pl.BlockSpec((1, tm, LANE), rows)),
        (jax.ShapeDtypeStruct((b, t, LANE), BF16), pl.BlockSpec((1, tm, LANE), rows)),
        (jax.ShapeDtypeStruct((b, t, 2 * LANE), BF16), pl.BlockSpec((1, tm, 2 * LANE), rows)),
        (jax.ShapeDtypeStruct((b, 2 * V_AUG, t), BF16), pl.BlockSpec((1, 2 * V_AUG, tm), cols)),
        (jax.ShapeDtypeStruct((b, t, LANE), BF16), pl.BlockSpec((1, tm, LANE), rows)),
        (jax.ShapeDtypeStruct((b, 2 * V_AUG, t), BF16), pl.BlockSpec((1, 2 * V_AUG, tm), cols)),
        (jax.ShapeDtypeStruct((b, LANE, t), F32), pl.BlockSpec((1, LANE, tm), cols)),
        (jax.ShapeDtypeStruct((b, t, 4 * HG_WIDTH), F32), pl.BlockSpec((1, tm, 4 * HG_WIDTH), rows)),
    ]
    return pl.pallas_call(
        _in_proj_kernel,
        grid=(b, t // tm),
        in_specs=[
            pl.BlockSpec((1, tm, d), rows),
            pl.BlockSpec((1, d), const),
            pl.BlockSpec(w.shape, const, pipeline_mode=pl.Buffered(1)),
            pl.BlockSpec((LANE, LANE), const),
            pl.BlockSpec((1, LANE), const),
            pl.BlockSpec((1, LANE), const),
            pl.BlockSpec((1, LANE), const),
        ],
        out_specs=[s for _, s in outs],
        out_shape=[s for s, _ in outs],
        scratch_shapes=[pltpu.VMEM((d, nw), BF16)],
        compiler_params=pltpu.CompilerParams(
            dimension_semantics=("arbitrary", "arbitrary"), vmem_limit_bytes=VMEM_LIMIT),
        name="in_proj",
    )(x, g, w, bd, qg, ksg, kwg)


def _compress_kernel(xk_ref, xv_ref, wklo_ref, wkhi_ref, wk2_ref, pk_ref,
                     wvlo_ref, wvhi_ref, wv2_ref, pv_ref, bd_ref, kg_ref, kc_ref, vct_ref):
    ns = xk_ref.shape[1]

    def mlp(x_ref, wlo_ref, whi_ref, w2_ref, p_ref):
        xb = x_ref[0]
        a = _dot(xb, wlo_ref[...])
        b = _dot(xb, whi_ref[...])
        h = a + pltpu.roll(b, ns - 1, axis=0) + p_ref[...]
        return _dot(jax.nn.gelu(h).astype(BF16), w2_ref[...])

    kc = mlp(xk_ref, wklo_ref, wkhi_ref, wk2_ref, pk_ref)
    kc_ref[0] = _group_rms(kc, bd_ref[...], kg_ref[...]).astype(BF16)
    vct_ref[0] = _values_t_aug(mlp(xv_ref, wvlo_ref, wvhi_ref, wv2_ref, pv_ref))


def _compress(xk, xv, wk, wv, bd, kg):
    b, ns, wd = xk.shape
    const2 = lambda i: (0, 0)
    bat = lambda i: (i, 0, 0)
    wspecs = [pl.BlockSpec(w.shape, const2) for w in wk] + [pl.BlockSpec(w.shape, const2) for w in wv]
    return pl.pallas_call(
        _compress_kernel,
        grid=(b,),
        in_specs=[pl.BlockSpec((1, ns, wd), bat), pl.BlockSpec((1, ns, wd), bat)] + wspecs + [
            pl.BlockSpec((LANE, LANE), const2), pl.BlockSpec((1, LANE), const2)],
        out_specs=[pl.BlockSpec((1, ns, LANE), bat), pl.BlockSpec((1, 2 * V_AUG, ns), bat)],
        out_shape=[jax.ShapeDtypeStruct((b, ns, LANE), BF16), jax.ShapeDtypeStruct((b, 2 * V_AUG, ns), BF16)],
        compiler_params=pltpu.CompilerParams(
            dimension_semantics=("arbitrary",), vmem_limit_bytes=VMEM_LIMIT),
        name="compress",
    )(xk, xv, *wk, *wv, bd, kg)


def _topk_bias_t(imp_ts, s0):
    nj, nq = imp_ts[0].shape
    jidx = lax.broadcasted_iota(jnp.int32, (nj, nq), 0)
    tq = s0 + lax.broadcasted_iota(jnp.int32, (nj, nq), 1)
    jcur = lax.shift_right_logical(tq, 6)
    forced = (jidx == 0) | (jidx == jcur) | (jidx == jcur - 1)
    future = jidx > jcur
    vs = [jnp.where(forced, -2.0, jnp.where(future, -1.0, imp_t)) for imp_t in imp_ts]
    jf = jidx.astype(F32)
    for _ in range(N_SELECT - 3):
        for g in range(len(vs)):
            m = jnp.max(vs[g], axis=0, keepdims=True)
            jm = jnp.min(jnp.where(vs[g] == m, jf, float(nj)), axis=0, keepdims=True)
            vs[g] = jnp.where(jf == jm, -2.0, vs[g])
    return [jnp.where(future, NEG, jnp.where(v < -1.5, 0.0, NEG)) for v in vs]


def _nsa_kernel(qt_ref, qtn_ref, ksa_ref, vst_ref, kw_ref, vwt_ref, kc_ref, vct_ref, gt_ref, gtn_ref, ovt_ref, o_ref,
                w_ref, wn_ref, bias_ref, cmp_ref, sc_ref, sw_ref, ss_ref, mx_ref, e_ref, ew_ref,
                m_ref, acc_ref, out_ref):
    qb = pl.program_id(1)
    s0 = qb * Q_BLOCK
    nc = kc_ref.shape[1]
    two = 2 * Q_BLOCK
    lane_q = lax.broadcasted_iota(jnp.int32, (1, two), 1) & (Q_BLOCK - 1)
    t_lane = s0 + lane_q
    gt = gt_ref[0]

    def stationary_q(q_ref, dst):
        zero = jnp.zeros((HEAD_DIM, Q_BLOCK), BF16)
        for p in range(N_PAIRS):
            g = (2 * p) // NSA_REP
            halves = []
            for h in (2 * p, 2 * p + 1):
                qh = q_ref[0, h * HEAD_DIM:(h + 1) * HEAD_DIM, :]
                halves.append(jnp.concatenate([qh, zero] if g == 0 else [zero, qh], axis=0))
            dst[p, 0:LANE, :] = jnp.concatenate(halves, axis=1)

    def store_scores(dst, s, valid):
        if valid is not None:
            s = jnp.where(valid, s, NEG)
        dst[...] = s
        return jnp.max(s, axis=0, keepdims=True)

    def exp_tiles(load, n_tiles, m, e_dst):
        for c in range(n_tiles):
            rows = slice(c * KEY_TILE, (c + 1) * KEY_TILE)
            e_dst[rows, :] = jnp.exp2(load(rows) - m).astype(BF16)

    def values_t(ref, g, cols):
        return ref[0, g * V_AUG:(g + 1) * V_AUG, cols]

    def emit(dst, gates, p, branch, o_aug, first, guard=None):
        inv = 1.0 / o_aug[HEAD_DIM:HEAD_DIM + 1, :]
        if guard is not None:
            inv = jnp.where(guard, inv, 0.0)
        for hh in range(2):
            h = 2 * p + hh
            ls = slice(hh * Q_BLOCK, (hh + 1) * Q_BLOCK)
            o = o_aug[0:HEAD_DIM, ls] * (inv[:, ls] * gates[3 * h + branch:3 * h + branch + 1, :])
            rs = slice(h * HEAD_DIM, (h + 1) * HEAD_DIM)
            if first:
                dst[rs, :] = o
            else:
                dst[rs, :] += o
        return inv

    def key_rows(n):
        return lax.broadcasted_iota(jnp.int32, (n, two), 0)

    def cmp_scores(q_ref, s0x):
        stationary_q(q_ref, wn_ref)
        kc = kc_ref[0]
        valid = key_rows(nc) <= lax.shift_right_arithmetic(s0x + lane_q - (CMP_LEN - 1), 4)
        return [store_scores(sc_ref.at[p], _dot(kc, wn_ref[p]), valid) for p in range(N_PAIRS)]

    def cmp_finish(gates, m_cmp):
        imp_t = [jnp.zeros((LANE, Q_BLOCK), F32) for _ in range(NSA_GROUPS)]
        for p in range(N_PAIRS):
            exp_tiles(lambda rows: sc_ref[p, rows, :], nc // KEY_TILE, m_cmp[p], e_ref.at[p])
        for p in range(N_PAIRS):
            g = (2 * p) // NSA_REP
            inv = emit(cmp_ref, gates, p, 0, _dot(values_t(vct_ref, g, slice(None)), e_ref[p, 0:nc, :]), True,
                       guard=m_cmp[p] > 0.5 * NEG)
            imp_p = _dot(ovt_ref[...], e_ref[p, 0:nc, :]) * inv
            imp_t[g] = imp_t[g] + imp_p[:, 0:Q_BLOCK] + imp_p[:, Q_BLOCK:two]
        return imp_t

    def select(imp_t, s0x):
        for g, bias_t in enumerate(_topk_bias_t(imp_t, s0x)):
            bias_ref[g] = bias_t.astype(BF16)

    @pl.when(qb == 0)
    def _():
        select(cmp_finish(gt, cmp_scores(qt_ref, s0)), s0)

    w0 = pl.multiple_of(jnp.maximum(s0 - WINDOW, 0), KEY_TILE)
    wlen = WINDOW + Q_BLOCK
    kwin = kw_ref[0, pl.ds(w0, wlen), :]
    vwin = [values_t(vwt_ref, g, pl.ds(w0, wlen)) for g in range(NSA_GROUPS)]

    stationary_q(qt_ref, w_ref)
    for p in range(N_PAIRS):
        bias_t = bias_ref[(2 * p) // NSA_REP]
        w_ref[p, LANE:2 * LANE, :] = jnp.concatenate([bias_t, bias_t], axis=1)
    out_ref[...] = cmp_ref[...]

    imp_t = cmp_finish(gtn_ref[0], cmp_scores(qtn_ref, s0 + Q_BLOCK))
    diff = (t_lane - w0) - key_rows(wlen)
    win_valid = lax.shift_right_arithmetic(diff, WINDOW.bit_length() - 1) == 0
    m_win = [store_scores(sw_ref.at[p], _dot(kwin, w_ref[p, 0:LANE, :]), win_valid) for p in range(N_PAIRS)]
    for p in range(N_PAIRS):
        exp_tiles(lambda rows: sw_ref[p, rows, :], wlen // KEY_TILE, m_win[p], ew_ref.at[p])
    for p in range(N_PAIRS):
        emit(out_ref, gt, p, 2, _dot(vwin[(2 * p) // NSA_REP], ew_ref[p]), False)
    select(imp_t, s0 + Q_BLOCK)

    m_ref[...] = jnp.full(m_ref.shape, NEG, F32)
    acc_ref[...] = jnp.zeros(acc_ref.shape, F32)

    def sel_scores(u, buf, p, causal):
        k0 = pl.multiple_of(u * KEY_UNIT, KEY_UNIT)
        ku = ksa_ref[0, pl.ds(k0, KEY_UNIT), :]
        valid = (k0 + key_rows(KEY_UNIT)) <= t_lane if causal else None
        mx_ref[buf, p] = store_scores(ss_ref.at[buf, p], _dot(ku, w_ref[p]), valid)

    def sel_update(u, buf, p):
        k0 = pl.multiple_of(u * KEY_UNIT, KEY_UNIT)
        g = (2 * p) // NSA_REP
        m_old = m_ref[p]
        m = jnp.maximum(m_old, mx_ref[buf, p])
        exp_tiles(lambda rows: ss_ref[buf, p, rows, :], KEY_UNIT // KEY_TILE, m, e_ref.at[p])
        m_ref[p] = m
        acc_ref[p] = jnp.exp2(m_old - m) * acc_ref[p] + _dot(
            values_t(vst_ref, g, pl.ds(k0, KEY_UNIT)), e_ref[p, 0:KEY_UNIT, :])

    def sel_step(u, buf, next_causal):
        for p in range(N_PAIRS):
            sel_scores(u + 1, 1 - buf, p, next_causal)
            sel_update(u, buf, p)

    def sel_last(u, buf):
        for p in range(N_PAIRS):
            sel_update(u, buf, p)

    n_before = qb // (KEY_UNIT // Q_BLOCK)
    n_trips = jnp.maximum(n_before - 1, 0) // 2
    rest = 2 * n_trips

    for p in range(N_PAIRS):
        sel_scores(0, 0, p, False)

    def body(i, carry):
        sel_step(2 * i, 0, False)
        sel_step(2 * i + 1, 1, False)
        return carry

    lax.fori_loop(0, n_trips, body, 0)

    @pl.when(n_before == 0)
    def _():
        for p in range(N_PAIRS):
            sel_scores(0, 0, p, True)
        sel_last(0, 0)

    @pl.when(n_before == rest + 1)
    def _():
        sel_step(rest, 0, True)
        sel_last(rest + 1, 1)

    @pl.when(n_before == rest + 2)
    def _():
        sel_step(rest, 0, False)
        sel_step(rest + 1, 1, True)
        sel_last(rest + 2, 0)

    for p in range(N_PAIRS):
        emit(out_ref, gt, p, 1, acc_ref[p], False)

    o_ref[0] = out_ref[...].T.astype(BF16)


def _nsa(qt, ksa, vst, kw, vwt, kc, vct, gt, ovt):
    b, _, t = qt.shape
    nc = kc.shape[1]
    n_qb = t // Q_BLOCK
    qcol = lambda i, j: (i, 0, j)
    qnext = lambda i, j: (i, 0, jnp.minimum(j + 1, n_qb - 1))
    full = lambda i, j: (i, 0, 0)
    const = lambda i, j: (0, 0)
    return pl.pallas_call(
        _nsa_kernel,
        grid=(b, n_qb),
        in_specs=[
            pl.BlockSpec((1, NSA_WIDTH, Q_BLOCK), qcol),
            pl.BlockSpec((1, NSA_WIDTH, Q_BLOCK), qnext),
            pl.BlockSpec((1, t, 2 * LANE), full),
            pl.BlockSpec((1, 2 * V_AUG, t), full),
            pl.BlockSpec((1, t, LANE), full),
            pl.BlockSpec((1, 2 * V_AUG, t), full),
            pl.BlockSpec((1, nc, LANE), full),
            pl.BlockSpec((1, 2 * V_AUG, nc), full),
            pl.BlockSpec((1, LANE, Q_BLOCK), qcol),
            pl.BlockSpec((1, LANE, Q_BLOCK), qnext),
            pl.BlockSpec(ovt.shape, const),
        ],
        out_specs=pl.BlockSpec((1, Q_BLOCK, NSA_WIDTH), lambda i, j: (i, j, 0)),
        out_shape=jax.ShapeDtypeStruct((b, t, NSA_WIDTH), BF16),
        scratch_shapes=[
            pltpu.VMEM((N_PAIRS, 2 * LANE, 2 * Q_BLOCK), BF16),
            pltpu.VMEM((N_PAIRS, LANE, 2 * Q_BLOCK), BF16),
            pltpu.VMEM((NSA_GROUPS, LANE, Q_BLOCK), BF16),
            pltpu.VMEM((NSA_WIDTH, Q_BLOCK), F32),
            pltpu.VMEM((N_PAIRS, nc, 2 * Q_BLOCK), F32),
            pltpu.VMEM((N_PAIRS, WINDOW + Q_BLOCK, 2 * Q_BLOCK), F32),
            pltpu.VMEM((2, N_PAIRS, KEY_UNIT, 2 * Q_BLOCK), F32),
            pltpu.VMEM((2, N_PAIRS, 1, 2 * Q_BLOCK), F32),
            pltpu.VMEM((N_PAIRS, max(nc, KEY_UNIT), 2 * Q_BLOCK), BF16),
            pltpu.VMEM((N_PAIRS, WINDOW + Q_BLOCK, 2 * Q_BLOCK), BF16),
            pltpu.VMEM((N_PAIRS, 1, 2 * Q_BLOCK), F32),
            pltpu.VMEM((N_PAIRS, V_AUG, 2 * Q_BLOCK), F32),
            pltpu.VMEM((NSA_WIDTH, Q_BLOCK), F32),
        ],
        compiler_params=pltpu.CompilerParams(
            dimension_semantics=("arbitrary", "arbitrary"), vmem_limit_bytes=VMEM_LIMIT),
        name="nsa",
    )(qt, qt, ksa, vst, kw, vwt, kc, vct, gt, gt, ovt)


def _hgrn_kernel(ph_ref, lb_ref, og_ref, o_ref, st_ref, *, chunks):
    c_len = HGRN_CHUNK
    n_sub = c_len // HGRN_SUB

    @pl.when(pl.program_id(1) == 0)
    def _():
        st_ref[...] = jnp.zeros_like(st_ref)

    lg = lb_ref[...]
    lmax = jnp.max(lg, axis=0, keepdims=True)
    le = jnp.exp(lg - lmax)
    lb_all = le[0:1] / jnp.sum(le, axis=0, keepdims=True)

    ri = lax.broadcasted_iota(jnp.int32, (c_len, c_len), 0)
    ci = lax.broadcasted_iota(jnp.int32, (c_len, c_len), 1)
    tri = (ci <= ri).astype(BF16)
    causal = ci <= ri
    rowi = lax.broadcasted_iota(jnp.int32, (c_len, HG_WIDTH), 0)
    gain = og_ref[...]
    heads = [slice(h * HGRN_DIM, (h + 1) * HGRN_DIM) for h in range(HGRN_HEADS)]

    pre = []
    for c in range(chunks):
        rs = slice(c * c_len, (c + 1) * c_len)
        hq = ph_ref[0, rs, 0:HG_WIDTH]
        hf = ph_ref[0, rs, HG_WIDTH:2 * HG_WIDTH]
        qv = jax.nn.silu(hq)
        f = lb_all + (1.0 - lb_all) * jax.nn.sigmoid(hf)
        logf = jnp.log(f)
        kv = 1.0 - f
        g1 = logf.astype(BF16)
        g2 = (logf - g1.astype(F32)).astype(BF16)
        bcum = _dot(tri, g1) + _dot(tri, g2)
        pre.append((qv, kv, bcum))

    mid = []
    for c in range(chunks):
        qv, kv, bcum = pre[c]
        b_last = bcum[c_len - 1:c_len]
        starts = [jnp.zeros((1, HG_WIDTH), F32)] + [bcum[i * HGRN_SUB - 1:i * HGRN_SUB] for i in range(1, n_sub)]
        ends = starts[1:] + [b_last]

        def per_sub(rows):
            return jnp.concatenate([jnp.broadcast_to(r, (HGRN_SUB, HG_WIDTH)) for r in rows], axis=0)

        start_full = per_sub(starts)
        qd = qv * jnp.exp(bcum - start_full)
        kb = kv * jnp.exp(per_sub(ends) - bcum)
        kdiag = kv * jnp.exp(start_full - bcum)
        qs = (qd * per_sub([jnp.exp(s) for s in starts])).astype(BF16)
        kdec = (kb * per_sub([jnp.exp(b_last - e) for e in ends])).astype(BF16)
        kds = []
        for i in range(n_sub):
            blocks = [kb[j * HGRN_SUB:(j + 1) * HGRN_SUB] * jnp.exp(starts[i] - ends[j]) for j in range(i - 1)]
            if i > 0:
                blocks.append(kb[(i - 1) * HGRN_SUB:i * HGRN_SUB])
            blocks.append(kdiag[i * HGRN_SUB:(i + 1) * HGRN_SUB])
            if i + 1 < n_sub:
                blocks.append(jnp.zeros(((n_sub - 1 - i) * HGRN_SUB, HG_WIDTH), F32))
            kds.append(jnp.concatenate(blocks, axis=0).astype(BF16))
        mid.append((qd.astype(BF16), kds, qs, kdec, jnp.exp(b_last)))

    attn = {}
    upd = {}
    for c in range(chunks):
        rs = slice(c * c_len, (c + 1) * c_len)
        qd, kds, qs, kdec, dec = mid[c]
        for h, hs in enumerate(heads):
            parts = [_dot_t(qd[i * HGRN_SUB:(i + 1) * HGRN_SUB, hs], kds[i][:, hs]) for i in range(n_sub)]
            attn[c, h] = jnp.where(causal, jnp.concatenate(parts, axis=0), 0.0).astype(BF16)
            hi = ph_ref[0, rs, 2 * HG_WIDTH + h * HGRN_DIM:2 * HG_WIDTH + (h + 1) * HGRN_DIM]
            upd[c, h] = _dot(hi.T.astype(BF16), kdec[:, hs])

    state = {}
    for h, hs in enumerate(heads):
        st = st_ref[h]
        for c in range(chunks):
            state[c, h] = st.astype(BF16)
            st = st * mid[c][4][:, hs] + upd[c, h]
        st_ref[h] = st

    for c in range(chunks):
        rs = slice(c * c_len, (c + 1) * c_len)
        qs = mid[c][2]
        for h, hs in enumerate(heads):
            vb = ph_ref[0, rs, 2 * HG_WIDTH + h * HGRN_DIM:2 * HG_WIDTH + (h + 1) * HGRN_DIM].astype(BF16)
            hg = ph_ref[0, rs, 3 * HG_WIDTH + h * HGRN_DIM:3 * HG_WIDTH + (h + 1) * HGRN_DIM]
            o = _dot(attn[c, h], vb) + _dot_t(qs[:, hs], state[c, h])
            ms = jnp.mean(o * o, axis=-1, keepdims=True)
            o = o * lax.rsqrt(ms + RMS_EPS) * gain * jax.nn.silu(hg)
            o_ref[0, rs, hs] = o.astype(BF16)


def _hgrn(ph, lb_logits, o_gain, tt=256):
    b, t, _ = ph.shape
    blk = lambda i, j: (i, j, 0)
    const = lambda i, j: (0, 0)
    return pl.pallas_call(
        functools.partial(_hgrn_kernel, chunks=tt // HGRN_CHUNK),
        grid=(b, t // tt),
        in_specs=[
            pl.BlockSpec((1, tt, 4 * HG_WIDTH), blk),
            pl.BlockSpec(lb_logits.shape, const),
            pl.BlockSpec((1, HGRN_DIM), const),
        ],
        out_specs=pl.BlockSpec((1, tt, HG_WIDTH), blk),
        out_shape=jax.ShapeDtypeStruct((b, t, HG_WIDTH), BF16),
        scratch_shapes=[pltpu.VMEM((HGRN_HEADS, HGRN_DIM, HGRN_DIM), F32)],
        compiler_params=pltpu.CompilerParams(
            dimension_semantics=("arbitrary", "arbitrary"), vmem_limit_bytes=VMEM_LIMIT),
        name="hgrn2",
    )(ph, lb_logits, o_gain)


def _ff_chunks(d_ff, width):
    return tuple((s, min(width, d_ff - s)) for s in range(0, d_ff, width))


def _out_ffn_kernel(x_ref, on_ref, oh_ref, won_ref, woh_ref, g_ref, wgu_ref, wd_ref, o_ref, *, chunks):
    d_ff = wd_ref.shape[0]
    x1 = x_ref[...] + _dot(on_ref[...], won_ref[...]) + _dot(oh_ref[...], woh_ref[...])
    ms = jnp.mean(x1 * x1, axis=-1, keepdims=True)
    h = (x1 * lax.rsqrt(ms + RMS_EPS) * g_ref[...]).astype(BF16)
    o_ref[...] = x1
    for s, n in chunks:
        gate = _dot(h, wgu_ref[:, s:s + n])
        up = _dot(h, wgu_ref[:, d_ff + s:d_ff + s + n])
        act = (jax.nn.silu(gate) * up).astype(BF16)
        o_ref[...] += _dot(act, wd_ref[s:s + n, :])


def _out_ffn(x2, o_nsa, o_hg, won, woh, g, wgu, wd, tm=256, ff_width=4 * MXU_WIDTH):
    n, d = x2.shape
    d_ff = wd.shape[0]
    row = lambda i: (i, 0)
    const = lambda i: (0, 0)
    return pl.pallas_call(
        functools.partial(_out_ffn_kernel, chunks=_ff_chunks(d_ff, ff_width)),
        grid=(n // tm,),
        in_specs=[
            pl.BlockSpec((tm, d), row),
            pl.BlockSpec((tm, NSA_WIDTH), row),
            pl.BlockSpec((tm, HG_WIDTH), row),
            pl.BlockSpec(won.shape, const),
            pl.BlockSpec(woh.shape, const),
            pl.BlockSpec((1, d), const),
            pl.BlockSpec(wgu.shape, const),
            pl.BlockSpec(wd.shape, const),
        ],
        out_specs=pl.BlockSpec((tm, d), row),
        out_shape=jax.ShapeDtypeStruct((n, d), F32),
        compiler_params=pltpu.CompilerParams(
            dimension_semantics=("arbitrary",), vmem_limit_bytes=VMEM_LIMIT),
        name="out_ffn",
    )(x2, o_nsa, o_hg, won, woh, g, wgu, wd)


def _expand_cmp_weights(pos, w1, w2):
    eye = jnp.eye(NSA_GROUPS, dtype=F32)
    w1r = w1.reshape(CMP_LEN, HEAD_DIM, CMP_HIDDEN)

    def lift(wpart):
        return jnp.einsum('ldc,gk->lgdkc', wpart, eye).reshape(
            CMP_STRIDE * NSA_GROUPS * HEAD_DIM, NSA_GROUPS * CMP_HIDDEN)

    wlo = lift(w1r[:CMP_STRIDE]).astype(BF16)
    whi = lift(w1r[CMP_STRIDE:]).astype(BF16)
    w2x = jnp.einsum('cd,gk->gckd', w2, eye).reshape(NSA_GROUPS * CMP_HIDDEN, NSA_GROUPS * HEAD_DIM).astype(BF16)
    pbias = jnp.tile(pos.reshape(1, CMP_LEN * HEAD_DIM) @ w1, (1, NSA_GROUPS))
    return wlo, whi, w2x, pbias


def _mixers(x, norm_mix, w_in, q_norm, k_norm, cmp_pos_k, cmp_pos_v, cmp_k_w1, cmp_k_w2, cmp_v_w1, cmp_v_w2,
            hgrn_lb_logits, hgrn_o_norm):
    b, t, d = x.shape
    depth = norm_mix.shape[0]
    assert depth == 1 and hgrn_lb_logits.shape[0] == 2
    assert t % KEY_UNIT == 0 and t >= WINDOW + Q_BLOCK and t // SLC_LEN <= LANE
    assert (t // CMP_STRIDE) % KEY_TILE == 0
    l = 0

    bd = jnp.asarray(np.kron(np.eye(LANE // HEAD_DIM), np.ones((HEAD_DIM, HEAD_DIM))), BF16)
    tile2 = lambda v: jnp.tile(v.reshape(1, HEAD_DIM), (1, LANE // HEAD_DIM)).astype(F32)
    qg = tile2(q_norm[l]) * (HEAD_DIM ** -0.5 * math.log2(math.e))

    qt, kc_raw, vc_raw, ksa, vst, kw, vwt, gt, ph = _in_proj(
        x, norm_mix[l].reshape(1, d), w_in[l], bd, qg, tile2(k_norm[l, 1]), tile2(k_norm[l, 2]))

    ns = t // CMP_STRIDE
    seg_w = CMP_STRIDE * KV_WIDTH
    kc, vct = _compress(
        kc_raw.reshape(b, ns, seg_w), vc_raw.reshape(b, ns, seg_w),
        _expand_cmp_weights(cmp_pos_k[l], cmp_k_w1[l], cmp_k_w2[l]),
        _expand_cmp_weights(cmp_pos_v[l], cmp_v_w1[l], cmp_v_w2[l]),
        bd, tile2(k_norm[l, 0]))

    cs = np.arange(ns)[None, :] * CMP_STRIDE
    ss = np.arange(LANE)[:, None] * SLC_LEN
    ovt = jnp.asarray(((cs < ss + SLC_LEN) & (cs + CMP_LEN > ss)).astype(np.float32), BF16)

    o_nsa = _nsa(qt, ksa, vst, kw, vwt, kc, vct, gt, ovt)
    o_hg = _hgrn(ph, hgrn_lb_logits, hgrn_o_norm[l].reshape(1, HGRN_DIM))
    return o_nsa, o_hg


def kernel(x, norm_mix, w_in, q_norm, k_norm, cmp_pos_k, cmp_pos_v, cmp_k_w1, cmp_k_w2, cmp_v_w1, cmp_v_w2,
           hgrn_lb_logits, hgrn_o_norm, w_out, norm_ffn, w_gate_up, w_down):
    b, t, d = x.shape
    l = 0
    o_nsa, o_hg = _mixers(x, norm_mix, w_in, q_norm, k_norm, cmp_pos_k, cmp_pos_v, cmp_k_w1, cmp_k_w2,
                          cmp_v_w1, cmp_v_w2, hgrn_lb_logits, hgrn_o_norm)
    wo = w_out[l]
    out = _out_ffn(
        x.reshape(b * t, d), o_nsa.reshape(b * t, NSA_WIDTH), o_hg.reshape(b * t, HG_WIDTH),
        wo[:NSA_WIDTH].astype(BF16), wo[NSA_WIDTH:].astype(BF16),
        norm_ffn[l].reshape(1, d), w_gate_up[l].astype(BF16), w_down[l].astype(BF16))
    return out.reshape(b, t, d)
```

```python
import functools
import math

import numpy as np
import jax
import jax.numpy as jnp
from jax import lax
from jax.experimental import pallas as pl
from jax.experimental.pallas import tpu as pltpu

F32 = jnp.float32
BF16 = jnp.bfloat16

LANE = 128
MXU_WIDTH = 256

NSA_HEADS = 8
NSA_GROUPS = 2
NSA_REP = NSA_HEADS // NSA_GROUPS
HEAD_DIM = 64
CMP_LEN = 32
CMP_STRIDE = 16
CMP_HIDDEN = 128
SLC_LEN = 64
N_SELECT = 16
WINDOW = 512
Q_BLOCK = 128
HGRN_HEADS = 4
HGRN_DIM = 128
HGRN_CHUNK = 64
HGRN_SUB = 16
RMS_EPS = 1e-6
NEG = -1e30

NSA_WIDTH = NSA_HEADS * HEAD_DIM
KV_WIDTH = NSA_GROUPS * HEAD_DIM
HG_WIDTH = HGRN_HEADS * HGRN_DIM
N_GATES = 3 * NSA_HEADS

KEY_TILE = 128
KEY_UNIT = 4 * KEY_TILE
N_PAIRS = NSA_HEADS // 2
V_AUG = HEAD_DIM + 16
VMEM_LIMIT = 56 * 1024 * 1024


def _dot(a, b):
    return jnp.dot(a, b, preferred_element_type=F32)


def _dot_t(a, b):
    return lax.dot_general(a, b, (((1,), (1,)), ((), ())), preferred_element_type=F32)


def _group_rms(y, bd, gain):
    ss = _dot((y * y).astype(BF16), bd)
    return y * lax.rsqrt(ss * (1.0 / HEAD_DIM) + RMS_EPS) * gain


def _values_t_aug(v):
    vt = v.T
    ones = jnp.ones((V_AUG - HEAD_DIM, v.shape[0]), F32)
    return jnp.concatenate([vt[0:HEAD_DIM], ones, vt[HEAD_DIM:2 * HEAD_DIM], ones], axis=0).astype(BF16)


def _hgrn_stages(ph_ref, lb_ref, og_ref, o_ref, st_ref, chunks):
    c_len = HGRN_CHUNK
    n_sub = c_len // HGRN_SUB
    lg = lb_ref[...]
    lmax = jnp.max(lg, axis=0, keepdims=True)
    le = jnp.exp(lg - lmax)
    lb_all = le[0:1] / jnp.sum(le, axis=0, keepdims=True)

    ri = lax.broadcasted_iota(jnp.int32, (c_len, c_len), 0)
    ci = lax.broadcasted_iota(jnp.int32, (c_len, c_len), 1)
    tri = (ci <= ri).astype(BF16)
    causal = ci <= ri
    gain = og_ref[...]
    heads = [slice(h * HGRN_DIM, (h + 1) * HGRN_DIM) for h in range(HGRN_HEADS)]
    rows = [slice(c * c_len, (c + 1) * c_len) for c in range(chunks)]

    pre = []
    for rs in rows:
        qv = jax.nn.silu(ph_ref[rs, 0:HG_WIDTH])
        f = lb_all + (1.0 - lb_all) * jax.nn.sigmoid(ph_ref[rs, HG_WIDTH:2 * HG_WIDTH])
        logf = jnp.log(f)
        g1 = logf.astype(BF16)
        g2 = (logf - g1.astype(F32)).astype(BF16)
        pre.append((qv, 1.0 - f, g1, g2))
    his = [ph_ref[rs, 2 * HG_WIDTH:3 * HG_WIDTH] for rs in rows]
    gates = [jax.nn.silu(ph_ref[rs, 3 * HG_WIDTH:4 * HG_WIDTH]) for rs in rows]
    yield

    bcums = [_dot(tri, g1) + _dot(tri, g2) for _, _, g1, g2 in pre]
    yield

    mid = []
    for c in range(chunks):
        qv, kv = pre[c][0], pre[c][1]
        bcum = bcums[c]
        b_last = bcum[c_len - 1:c_len]
        starts = [jnp.zeros((1, HG_WIDTH), F32)] + [bcum[i * HGRN_SUB - 1:i * HGRN_SUB] for i in range(1, n_sub)]
        ends = starts[1:] + [b_last]

        def per_sub(vals):
            return jnp.concatenate([jnp.broadcast_to(r, (HGRN_SUB, HG_WIDTH)) for r in vals], axis=0)

        start_full = per_sub(starts)
        qd = qv * jnp.exp(bcum - start_full)
        kb = kv * jnp.exp(per_sub(ends) - bcum)
        kdiag = kv * jnp.exp(start_full - bcum)
        qs = (qd * per_sub([jnp.exp(s) for s in starts])).astype(BF16)
        kdec = (kb * per_sub([jnp.exp(b_last - e) for e in ends])).astype(BF16)
        kds = []
        for i in range(n_sub):
            blocks = [kb[j * HGRN_SUB:(j + 1) * HGRN_SUB] * jnp.exp(starts[i] - ends[j]) for j in range(i - 1)]
            if i > 0:
                blocks.append(kb[(i - 1) * HGRN_SUB:i * HGRN_SUB])
            blocks.append(kdiag[i * HGRN_SUB:(i + 1) * HGRN_SUB])
            if i + 1 < n_sub:
                blocks.append(jnp.zeros(((n_sub - 1 - i) * HGRN_SUB, HG_WIDTH), F32))
            kds.append(jnp.concatenate(blocks, axis=0).astype(BF16))
        mid.append((qd.astype(BF16), kds, qs, kdec, jnp.exp(b_last)))
    attn = {}
    upd = {}
    for c in range(chunks):
        qd, kds, qs, kdec, dec = mid[c]
        for h, hs in enumerate(heads):
            parts = [_dot_t(qd[i * HGRN_SUB:(i + 1) * HGRN_SUB, hs], kds[i][:, hs]) for i in range(n_sub)]
            attn[c, h] = jnp.where(causal, jnp.concatenate(parts, axis=0), 0.0).astype(BF16)
            upd[c, h] = _dot(his[c][:, hs].T.astype(BF16), kdec[:, hs])
    yield

    state = {}
    for h, hs in enumerate(heads):
        st = st_ref[h]
        for c in range(chunks):
            state[c, h] = st.astype(BF16)
            st = st * mid[c][4][:, hs] + upd[c, h]
        st_ref[h] = st
    for c, rs in enumerate(rows):
        qs = mid[c][2]
        for h, hs in enumerate(heads):
            o = _dot(attn[c, h], his[c][:, hs].astype(BF16)) + _dot_t(qs[:, hs], state[c, h])
            ms = jnp.mean(o * o, axis=-1, keepdims=True)
            o = o * lax.rsqrt(ms + RMS_EPS) * gain * gates[c][:, hs]
            o_ref[0, rs, hs] = o.astype(BF16)
    yield


def _in_proj_kernel(x_ref, g_ref, w_ref, bd_ref, qg_ref, ksg_ref, kwg_ref, lb_ref, og_ref,
                    qt_ref, kc_ref, vc_ref, ksa_ref, vst_ref, kw_ref, vwt_ref, gt_ref, ohg_ref,
                    wp_ref, ph_ref, st_ref):
    tm = x_ref.shape[1]
    d = x_ref.shape[2]
    j = pl.program_id(1)
    tile = jnp.minimum(j, pl.num_programs(1) - 2)
    o_g = NSA_WIDTH + 6 * KV_WIDTH

    @pl.when((pl.program_id(0) == 0) & (j == 0))
    def _():
        o_h = o_g + N_GATES
        rows_per = 128

        def copy_rows(i, carry):
            rs = pl.ds(pl.multiple_of(i * rows_per, rows_per), rows_per)
            wp_ref[rs, 0:o_g] = w_ref[rs, 0:o_g].astype(BF16)
            wp_ref[rs, o_g:o_g + 4 * HG_WIDTH] = w_ref[rs, o_h:o_h + 4 * HG_WIDTH].astype(BF16)
            tail = jnp.concatenate(
                [w_ref[rs, o_g:o_h], jnp.zeros((rows_per, LANE - N_GATES), F32)], axis=1)
            wp_ref[rs, o_g + 4 * HG_WIDTH:o_g + 4 * HG_WIDTH + LANE] = tail.astype(BF16)
            return carry

        lax.fori_loop(0, d // rows_per, copy_rows, 0)
        ph_ref[...] = jnp.zeros(ph_ref.shape, F32)
        st_ref[...] = jnp.zeros(st_ref.shape, F32)

    x = x_ref[0]
    ms = jnp.mean(x * x, axis=-1, keepdims=True)
    h = (x * lax.rsqrt(ms + RMS_EPS) * g_ref[...]).astype(BF16)

    mixer = _hgrn_stages(ph_ref, lb_ref, og_ref, ohg_ref, st_ref, tm // HGRN_CHUNK)
    next(mixer)
    y_nsa = _dot(h, wp_ref[:, 0:o_g])
    next(mixer)

    def hgrn_cols(k):
        return _dot(h, wp_ref[:, o_g + k * HG_WIDTH:o_g + (k + 1) * HG_WIDTH])

    y_h = [hgrn_cols(0), hgrn_cols(1)]
    next(mixer)

    bd = bd_ref[...]
    for r in range(NSA_WIDTH // LANE):
        sl = slice(r * LANE, (r + 1) * LANE)
        qt_ref[0, sl, :] = _group_rms(y_nsa[:, sl], bd, qg_ref[...]).T.astype(BF16)
    o = NSA_WIDTH
    kc_ref[0] = y_nsa[:, o:o + LANE].astype(BF16)
    vc_ref[0] = y_nsa[:, o + LANE:o + 2 * LANE].astype(BF16)
    ksa_ref[0, :, 0:LANE] = _group_rms(y_nsa[:, o + 2 * LANE:o + 3 * LANE], bd, ksg_ref[...]).astype(BF16)
    key = tile * tm + lax.broadcasted_iota(jnp.int32, (tm, LANE), 0)
    blk = lax.broadcasted_iota(jnp.int32, (tm, LANE), 1)
    ksa_ref[0, :, LANE:2 * LANE] = jnp.where(lax.shift_right_logical(key, 6) == blk, 1.0, 0.0).astype(BF16)
    vst_ref[0] = _values_t_aug(y_nsa[:, o + 3 * LANE:o + 4 * LANE])
    kw_ref[0] = _group_rms(y_nsa[:, o + 4 * LANE:o + 5 * LANE], bd, kwg_ref[...]).astype(BF16)
    vwt_ref[0] = _values_t_aug(y_nsa[:, o + 5 * LANE:o + 6 * LANE])
    y_gate = _dot(h, wp_ref[:, o_g + 4 * HG_WIDTH:o_g + 4 * HG_WIDTH + LANE])
    gt_ref[0] = jax.nn.sigmoid(y_gate).T
    y_h.append(hgrn_cols(2))
    next(mixer)
    y_h.append(hgrn_cols(3))
    for k in range(4):
        ph_ref[:, k * HG_WIDTH:(k + 1) * HG_WIDTH] = y_h[k]

    @pl.when(j == 0)
    def _():
        st_ref[...] = jnp.zeros(st_ref.shape, F32)


def _in_proj(x, g, w, bd, qg, ksg, kwg, lb_logits, o_gain, tm=256):
    b, t, d = x.shape
    n_t = t // tm
    nw = NSA_WIDTH + 6 * KV_WIDTH + 4 * HG_WIDTH + LANE
    rows = lambda i, j: (i, jnp.minimum(j, n_t - 1), 0)
    cols = lambda i, j: (i, 0, jnp.minimum(j, n_t - 1))
    prev = lambda i, j: (i, jnp.maximum(j - 1, 0), 0)
    const = lambda i, j: (0, 0)
    outs = [
        (jax.ShapeDtypeStruct((b, NSA_WIDTH, t), BF16), pl.BlockSpec((1, NSA_WIDTH, tm), cols)),
        (jax.ShapeDtypeStruct((b, t, LANE), BF16), pl.BlockSpec((1, tm, LANE), rows)),
        (jax.ShapeDtypeStruct((b, t, LANE), BF16), pl.BlockSpec((1, tm, LANE), rows)),
        (jax.ShapeDtypeStruct((b, t, 2 * LANE), BF16), pl.BlockSpec((1, tm, 2 * LANE), rows)),
        (jax.ShapeDtypeStruct((b, 2 * V_AUG, t), BF16), pl.BlockSpec((1, 2 * V_AUG, tm), cols)),
        (jax.ShapeDtypeStruct((b, t, LANE), BF16), pl.BlockSpec((1, tm, LANE), rows)),
        (jax.ShapeDtypeStruct((b, 2 * V_AUG, t), BF16), pl.BlockSpec((1, 2 * V_AUG, tm), cols)),
        (jax.ShapeDtypeStruct((b, LANE, t), F32), pl.BlockSpec((1, LANE, tm), cols)),
        (jax.ShapeDtypeStruct((b, t, HG_WIDTH), BF16), pl.BlockSpec((1, tm, HG_WIDTH), prev)),
    ]
    return pl.pallas_call(
        _in_proj_kernel,
        grid=(b, n_t + 1),
        in_specs=[
            pl.BlockSpec((1, tm, d), rows),
            pl.BlockSpec((1, d), const),
            pl.BlockSpec(w.shape, const, pipeline_mode=pl.Buffered(1)),
            pl.BlockSpec((LANE, LANE), const),
            pl.BlockSpec((1, LANE), const),
            pl.BlockSpec((1, LANE), const),
            pl.BlockSpec((1, LANE), const),
            pl.BlockSpec(lb_logits.shape, const),
            pl.BlockSpec((1, HGRN_DIM), const),
        ],
        out_specs=[s for _, s in outs],
        out_shape=[s for s, _ in outs],
        scratch_shapes=[
            pltpu.VMEM((d, nw), BF16),
            pltpu.VMEM((tm, 4 * HG_WIDTH), F32),
            pltpu.VMEM((HGRN_HEADS, HGRN_DIM, HGRN_DIM), F32),
        ],
        compiler_params=pltpu.CompilerParams(
            dimension_semantics=("arbitrary", "arbitrary"), vmem_limit_bytes=VMEM_LIMIT),
        name="in_proj",
    )(x, g, w, bd, qg, ksg, kwg, lb_logits, o_gain)


def _compress_kernel(xk_ref, xv_ref, wklo_ref, wkhi_ref, wk2_ref, pk_ref,
                     wvlo_ref, wvhi_ref, wv2_ref, pv_ref, bd_ref, kg_ref, kc_ref, vct_ref):
    ns = xk_ref.shape[1]

    def mlp(x_ref, wlo_ref, whi_ref, w2_ref, p_ref):
        xb = x_ref[0]
        a = _dot(xb, wlo_ref[...])
        b = _dot(xb, whi_ref[...])
        h = a + pltpu.roll(b, ns - 1, axis=0) + p_ref[...]
        return _dot(jax.nn.gelu(h).astype(BF16), w2_ref[...])

    kc = mlp(xk_ref, wklo_ref, wkhi_ref, wk2_ref, pk_ref)
    kc_ref[0] = _group_rms(kc, bd_ref[...], kg_ref[...]).astype(BF16)
    vct_ref[0] = _values_t_aug(mlp(xv_ref, wvlo_ref, wvhi_ref, wv2_ref, pv_ref))


def _compress(xk, xv, wk, wv, bd, kg):
    b, ns, wd = xk.shape
    const2 = lambda i: (0, 0)
    bat = lambda i: (i, 0, 0)
    wspecs = [pl.BlockSpec(w.shape, const2) for w in wk] + [pl.BlockSpec(w.shape, const2) for w in wv]
    return pl.pallas_call(
        _compress_kernel,
        grid=(b,),
        in_specs=[pl.BlockSpec((1, ns, wd), bat), pl.BlockSpec((1, ns, wd), bat)] + wspecs + [
            pl.BlockSpec((LANE, LANE), const2), pl.BlockSpec((1, LANE), const2)],
        out_specs=[pl.BlockSpec((1, ns, LANE), bat), pl.BlockSpec((1, 2 * V_AUG, ns), bat)],
        out_shape=[jax.ShapeDtypeStruct((b, ns, LANE), BF16), jax.ShapeDtypeStruct((b, 2 * V_AUG, ns), BF16)],
        compiler_params=pltpu.CompilerParams(
            dimension_semantics=("arbitrary",), vmem_limit_bytes=VMEM_LIMIT),
        name="compress",
    )(xk, xv, *wk, *wv, bd, kg)


def _topk_bias_t(imp_ts, s0):
    nj, nq = imp_ts[0].shape
    jidx = lax.broadcasted_iota(jnp.int32, (nj, nq), 0)
    tq = s0 + lax.broadcasted_iota(jnp.int32, (nj, nq), 1)
    jcur = lax.shift_right_logical(tq, 6)
    forced = (jidx == 0) | (jidx == jcur) | (jidx == jcur - 1)
    future = jidx > jcur
    vs = [jnp.where(forced, -2.0, jnp.where(future, -1.0, imp_t)) for imp_t in imp_ts]
    jf = jidx.astype(F32)
    for _ in range(N_SELECT - 3):
        for g in range(len(vs)):
            m = jnp.max(vs[g], axis=0, keepdims=True)
            jm = jnp.min(jnp.where(vs[g] == m, jf, float(nj)), axis=0, keepdims=True)
            vs[g] = jnp.where(jf == jm, -2.0, vs[g])
    return [jnp.where(future, NEG, jnp.where(v < -1.5, 0.0, NEG)) for v in vs]


def _nsa_kernel(qt_ref, qtn_ref, ksa_ref, vst_ref, kw_ref, vwt_ref, kc_ref, vct_ref, gt_ref, gtn_ref, ovt_ref, o_ref,
                w_ref, wn_ref, bias_ref, cmp_ref, sc_ref, sw_ref, ss_ref, mx_ref, e_ref, ew_ref,
                m_ref, acc_ref, out_ref):
    qb = pl.program_id(1)
    s0 = qb * Q_BLOCK
    nc = kc_ref.shape[1]
    two = 2 * Q_BLOCK
    lane_q = lax.broadcasted_iota(jnp.int32, (1, two), 1) & (Q_BLOCK - 1)
    t_lane = s0 + lane_q
    gt = gt_ref[0]

    def stationary_q(q_ref, dst):
        zero = jnp.zeros((HEAD_DIM, Q_BLOCK), BF16)
        for p in range(N_PAIRS):
            g = (2 * p) // NSA_REP
            halves = []
            for h in (2 * p, 2 * p + 1):
                qh = q_ref[0, h * HEAD_DIM:(h + 1) * HEAD_DIM, :]
                halves.append(jnp.concatenate([qh, zero] if g == 0 else [zero, qh], axis=0))
            dst[p, 0:LANE, :] = jnp.concatenate(halves, axis=1)

    def store_scores(dst, s, valid):
        if valid is not None:
            s = jnp.where(valid, s, NEG)
        dst[...] = s
        return jnp.max(s, axis=0, keepdims=True)

    def exp_tiles(load, n_tiles, m, e_dst):
        for c in range(n_tiles):
            rows = slice(c * KEY_TILE, (c + 1) * KEY_TILE)
            e_dst[rows, :] = jnp.exp2(load(rows) - m).astype(BF16)

    def values_t(ref, g, cols):
        return ref[0, g * V_AUG:(g + 1) * V_AUG, cols]

    def emit(dst, gates, p, branch, o_aug, first, guard=None):
        inv = 1.0 / o_aug[HEAD_DIM:HEAD_DIM + 1, :]
        if guard is not None:
            inv = jnp.where(guard, inv, 0.0)
        for hh in range(2):
            h = 2 * p + hh
            ls = slice(hh * Q_BLOCK, (hh + 1) * Q_BLOCK)
            o = o_aug[0:HEAD_DIM, ls] * (inv[:, ls] * gates[3 * h + branch:3 * h + branch + 1, :])
            rs = slice(h * HEAD_DIM, (h + 1) * HEAD_DIM)
            if first:
                dst[rs, :] = o
            else:
                dst[rs, :] += o
        return inv

    def key_rows(n):
        return lax.broadcasted_iota(jnp.int32, (n, two), 0)

    def cmp_scores(q_ref, s0x):
        stationary_q(q_ref, wn_ref)
        kc = kc_ref[0]
        valid = key_rows(nc) <= lax.shift_right_arithmetic(s0x + lane_q - (CMP_LEN - 1), 4)
        return [store_scores(sc_ref.at[p], _dot(kc, wn_ref[p]), valid) for p in range(N_PAIRS)]

    def cmp_finish(gates, m_cmp):
        imp_t = [jnp.zeros((LANE, Q_BLOCK), F32) for _ in range(NSA_GROUPS)]
        for p in range(N_PAIRS):
            exp_tiles(lambda rows: sc_ref[p, rows, :], nc // KEY_TILE, m_cmp[p], e_ref.at[p])
        for p in range(N_PAIRS):
            g = (2 * p) // NSA_REP
            inv = emit(cmp_ref, gates, p, 0, _dot(values_t(vct_ref, g, slice(None)), e_ref[p, 0:nc, :]), True,
                       guard=m_cmp[p] > 0.5 * NEG)
            imp_p = _dot(ovt_ref[...], e_ref[p, 0:nc, :]) * inv
            imp_t[g] = imp_t[g] + imp_p[:, 0:Q_BLOCK] + imp_p[:, Q_BLOCK:two]
        return imp_t

    def select(imp_t, s0x):
        for g, bias_t in enumerate(_topk_bias_t(imp_t, s0x)):
            bias_ref[g] = bias_t.astype(BF16)

    @pl.when(qb == 0)
    def _():
        select(cmp_finish(gt, cmp_scores(qt_ref, s0)), s0)

    w0 = pl.multiple_of(jnp.maximum(s0 - WINDOW, 0), KEY_TILE)
    wlen = WINDOW + Q_BLOCK
    kwin = kw_ref[0, pl.ds(w0, wlen), :]
    vwin = [values_t(vwt_ref, g, pl.ds(w0, wlen)) for g in range(NSA_GROUPS)]

    stationary_q(qt_ref, w_ref)
    for p in range(N_PAIRS):
        bias_t = bias_ref[(2 * p) // NSA_REP]
        w_ref[p, LANE:2 * LANE, :] = jnp.concatenate([bias_t, bias_t], axis=1)
    out_ref[...] = cmp_ref[...]

    imp_t = cmp_finish(gtn_ref[0], cmp_scores(qtn_ref, s0 + Q_BLOCK))
    diff = (t_lane - w0) - key_rows(wlen)
    win_valid = lax.shift_right_arithmetic(diff, WINDOW.bit_length() - 1) == 0
    m_win = [store_scores(sw_ref.at[p], _dot(kwin, w_ref[p, 0:LANE, :]), win_valid) for p in range(N_PAIRS)]
    for p in range(N_PAIRS):
        exp_tiles(lambda rows: sw_ref[p, rows, :], wlen // KEY_TILE, m_win[p], ew_ref.at[p])
    for p in range(N_PAIRS):
        emit(out_ref, gt, p, 2, _dot(vwin[(2 * p) // NSA_REP], ew_ref[p]), False)

    m_ref[...] = jnp.full(m_ref.shape, NEG, F32)
    acc_ref[...] = jnp.zeros(acc_ref.shape, F32)

    def sel_scores(u, buf, p, causal):
        k0 = pl.multiple_of(u * KEY_UNIT, KEY_UNIT)
        ku = ksa_ref[0, pl.ds(k0, KEY_UNIT), :]
        valid = (k0 + key_rows(KEY_UNIT)) <= t_lane if causal else None
        mx_ref[buf, p] = store_scores(ss_ref.at[buf, p], _dot(ku, w_ref[p]), valid)

    def sel_update(u, buf, p):
        k0 = pl.multiple_of(u * KEY_UNIT, KEY_UNIT)
        g = (2 * p) // NSA_REP
        m_old = m_ref[p]
        m = jnp.maximum(m_old, mx_ref[buf, p])
        exp_tiles(lambda rows: ss_ref[buf, p, rows, :], KEY_UNIT // KEY_TILE, m, e_ref.at[p])
        m_ref[p] = m
        acc_ref[p] = jnp.exp2(m_old - m) * acc_ref[p] + _dot(
            values_t(vst_ref, g, pl.ds(k0, KEY_UNIT)), e_ref[p, 0:KEY_UNIT, :])

    def sel_step(u, buf, next_causal):
        for p in range(N_PAIRS):
            sel_scores(u + 1, 1 - buf, p, next_causal)
            sel_update(u, buf, p)

    def sel_last(u, buf):
        for p in range(N_PAIRS):
            sel_update(u, buf, p)

    n_before = qb // (KEY_UNIT // Q_BLOCK)
    n_trips = jnp.maximum(n_before - 1, 0) // 2
    rest = 2 * n_trips

    for p in range(N_PAIRS):
        sel_scores(0, 0, p, False)
    select(imp_t, s0 + Q_BLOCK)

    def body(i, carry):
        sel_step(2 * i, 0, False)
        sel_step(2 * i + 1, 1, False)
        return carry

    lax.fori_loop(0, n_trips, body, 0)

    @pl.when(n_before == 0)
    def _():
        for p in range(N_PAIRS):
            sel_scores(0, 0, p, True)
        sel_last(0, 0)

    @pl.when(n_before == rest + 1)
    def _():
        sel_step(rest, 0, True)
        sel_last(rest + 1, 1)

    @pl.when(n_before == rest + 2)
    def _():
        sel_step(rest, 0, False)
        sel_step(rest + 1, 1, True)
        sel_last(rest + 2, 0)

    for p in range(N_PAIRS):
        emit(out_ref, gt, p, 1, acc_ref[p], False)

    o_ref[0] = out_ref[...].T.astype(BF16)


def _nsa(qt, ksa, vst, kw, vwt, kc, vct, gt, ovt):
    b, _, t = qt.shape
    nc = kc.shape[1]
    n_qb = t // Q_BLOCK
    qcol = lambda i, j: (i, 0, j)
    qnext = lambda i, j: (i, 0, jnp.minimum(j + 1, n_qb - 1))
    full = lambda i, j: (i, 0, 0)
    const = lambda i, j: (0, 0)
    return pl.pallas_call(
        _nsa_kernel,
        grid=(b, n_qb),
        in_specs=[
            pl.BlockSpec((1, NSA_WIDTH, Q_BLOCK), qcol),
            pl.BlockSpec((1, NSA_WIDTH, Q_BLOCK), qnext),
            pl.BlockSpec((1, t, 2 * LANE), full),
            pl.BlockSpec((1, 2 * V_AUG, t), full),
            pl.BlockSpec((1, t, LANE), full),
            pl.BlockSpec((1, 2 * V_AUG, t), full),
            pl.BlockSpec((1, nc, LANE), full),
            pl.BlockSpec((1, 2 * V_AUG, nc), full),
            pl.BlockSpec((1, LANE, Q_BLOCK), qcol),
            pl.BlockSpec((1, LANE, Q_BLOCK), qnext),
            pl.BlockSpec(ovt.shape, const),
        ],
        out_specs=pl.BlockSpec((1, Q_BLOCK, NSA_WIDTH), lambda i, j: (i, j, 0)),
        out_shape=jax.ShapeDtypeStruct((b, t, NSA_WIDTH), BF16),
        scratch_shapes=[
            pltpu.VMEM((N_PAIRS, 2 * LANE, 2 * Q_BLOCK), BF16),
            pltpu.VMEM((N_PAIRS, LANE, 2 * Q_BLOCK), BF16),
            pltpu.VMEM((NSA_GROUPS, LANE, Q_BLOCK), BF16),
            pltpu.VMEM((NSA_WIDTH, Q_BLOCK), F32),
            pltpu.VMEM((N_PAIRS, nc, 2 * Q_BLOCK), F32),
            pltpu.VMEM((N_PAIRS, WINDOW + Q_BLOCK, 2 * Q_BLOCK), F32),
            pltpu.VMEM((2, N_PAIRS, KEY_UNIT, 2 * Q_BLOCK), F32),
            pltpu.VMEM((2, N_PAIRS, 1, 2 * Q_BLOCK), F32),
            pltpu.VMEM((N_PAIRS, max(nc, KEY_UNIT), 2 * Q_BLOCK), BF16),
            pltpu.VMEM((N_PAIRS, WINDOW + Q_BLOCK, 2 * Q_BLOCK), BF16),
            pltpu.VMEM((N_PAIRS, 1, 2 * Q_BLOCK), F32),
            pltpu.VMEM((N_PAIRS, V_AUG, 2 * Q_BLOCK), F32),
            pltpu.VMEM((NSA_WIDTH, Q_BLOCK), F32),
        ],
        compiler_params=pltpu.CompilerParams(
            dimension_semantics=("arbitrary", "arbitrary"), vmem_limit_bytes=VMEM_LIMIT),
        name="nsa",
    )(qt, qt, ksa, vst, kw, vwt, kc, vct, gt, gt, ovt)


def _ff_chunks(d_ff, width):
    return tuple((s, min(width, d_ff - s)) for s in range(0, d_ff, width))


def _out_ffn_kernel(x_ref, on_ref, oh_ref, won_ref, woh_ref, g_ref, wgu_ref, wd_ref, o_ref, *, chunks):
    d_ff = wd_ref.shape[0]
    x1 = x_ref[...] + _dot(on_ref[...], won_ref[...]) + _dot(oh_ref[...], woh_ref[...])
    ms = jnp.mean(x1 * x1, axis=-1, keepdims=True)
    h = (x1 * lax.rsqrt(ms + RMS_EPS) * g_ref[...]).astype(BF16)
    o_ref[...] = x1
    for s, n in chunks:
        gate = _dot(h, wgu_ref[:, s:s + n])
        up = _dot(h, wgu_ref[:, d_ff + s:d_ff + s + n])
        act = (jax.nn.silu(gate) * up).astype(BF16)
        o_ref[...] += _dot(act, wd_ref[s:s + n, :])


def _out_ffn(x2, o_nsa, o_hg, won, woh, g, wgu, wd, tm=256, ff_width=4 * MXU_WIDTH):
    n, d = x2.shape
    d_ff = wd.shape[0]
    row = lambda i: (i, 0)
    const = lambda i: (0, 0)
    return pl.pallas_call(
        functools.partial(_out_ffn_kernel, chunks=_ff_chunks(d_ff, ff_width)),
        grid=(n // tm,),
        in_specs=[
            pl.BlockSpec((tm, d), row),
            pl.BlockSpec((tm, NSA_WIDTH), row),
            pl.BlockSpec((tm, HG_WIDTH), row),
            pl.BlockSpec(won.shape, const),
            pl.BlockSpec(woh.shape, const),
            pl.BlockSpec((1, d), const),
            pl.BlockSpec(wgu.shape, const),
            pl.BlockSpec(wd.shape, const),
        ],
        out_specs=pl.BlockSpec((tm, d), row),
        out_shape=jax.ShapeDtypeStruct((n, d), F32),
        compiler_params=pltpu.CompilerParams(
            dimension_semantics=("arbitrary",), vmem_limit_bytes=VMEM_LIMIT),
        name="out_ffn",
    )(x2, o_nsa, o_hg, won, woh, g, wgu, wd)


def _expand_cmp_weights(pos, w1, w2):
    eye = jnp.eye(NSA_GROUPS, dtype=F32)
    w1r = w1.reshape(CMP_LEN, HEAD_DIM, CMP_HIDDEN)

    def lift(wpart):
        return jnp.einsum('ldc,gk->lgdkc', wpart, eye).reshape(
            CMP_STRIDE * NSA_GROUPS * HEAD_DIM, NSA_GROUPS * CMP_HIDDEN)

    wlo = lift(w1r[:CMP_STRIDE]).astype(BF16)
    whi = lift(w1r[CMP_STRIDE:]).astype(BF16)
    w2x = jnp.einsum('cd,gk->gckd', w2, eye).reshape(NSA_GROUPS * CMP_HIDDEN, NSA_GROUPS * HEAD_DIM).astype(BF16)
    pbias = jnp.tile(pos.reshape(1, CMP_LEN * HEAD_DIM) @ w1, (1, NSA_GROUPS))
    return wlo, whi, w2x, pbias


def _mixers(x, norm_mix, w_in, q_norm, k_norm, cmp_pos_k, cmp_pos_v, cmp_k_w1, cmp_k_w2, cmp_v_w1, cmp_v_w2,
            hgrn_lb_logits, hgrn_o_norm):
    b, t, d = x.shape
    depth = norm_mix.shape[0]
    assert depth == 1 and hgrn_lb_logits.shape[0] == 2
    assert t % KEY_UNIT == 0 and t >= WINDOW + Q_BLOCK and t // SLC_LEN <= LANE
    assert (t // CMP_STRIDE) % KEY_TILE == 0
    l = 0

    bd = jnp.asarray(np.kron(np.eye(LANE // HEAD_DIM), np.ones((HEAD_DIM, HEAD_DIM))), BF16)
    tile2 = lambda v: jnp.tile(v.reshape(1, HEAD_DIM), (1, LANE // HEAD_DIM)).astype(F32)
    qg = tile2(q_norm[l]) * (HEAD_DIM ** -0.5 * math.log2(math.e))

    qt, kc_raw, vc_raw, ksa, vst, kw, vwt, gt, o_hg = _in_proj(
        x, norm_mix[l].reshape(1, d), w_in[l], bd, qg, tile2(k_norm[l, 1]), tile2(k_norm[l, 2]),
        hgrn_lb_logits, hgrn_o_norm[l].reshape(1, HGRN_DIM))

    ns = t // CMP_STRIDE
    seg_w = CMP_STRIDE * KV_WIDTH
    kc, vct = _compress(
        kc_raw.reshape(b, ns, seg_w), vc_raw.reshape(b, ns, seg_w),
        _expand_cmp_weights(cmp_pos_k[l], cmp_k_w1[l], cmp_k_w2[l]),
        _expand_cmp_weights(cmp_pos_v[l], cmp_v_w1[l], cmp_v_w2[l]),
        bd, tile2(k_norm[l, 0]))

    cs = np.arange(ns)[None, :] * CMP_STRIDE
    ss = np.arange(LANE)[:, None] * SLC_LEN
    ovt = jnp.asarray(((cs < ss + SLC_LEN) & (cs + CMP_LEN > ss)).astype(np.float32), BF16)

    o_nsa = _nsa(qt, ksa, vst, kw, vwt, kc, vct, gt, ovt)
    return o_nsa, o_hg


def kernel(x, norm_mix, w_in, q_norm, k_norm, cmp_pos_k, cmp_pos_v, cmp_k_w1, cmp_k_w2, cmp_v_w1, cmp_v_w2,
           hgrn_lb_logits, hgrn_o_norm, w_out, norm_ffn, w_gate_up, w_down):
    b, t, d = x.shape
    l = 0
    o_nsa, o_hg = _mixers(x, norm_mix, w_in, q_norm, k_norm, cmp_pos_k, cmp_pos_v, cmp_k_w1, cmp_k_w2,
                          cmp_v_w1, cmp_v_w2, hgrn_lb_logits, hgrn_o_norm)
    wo = w_out[l]
    out = _out_ffn(
        x.reshape(b * t, d), o_nsa.reshape(b * t, NSA_WIDTH), o_hg.reshape(b * t, HG_WIDTH),
        wo[:NSA_WIDTH].astype(BF16), wo[NSA_WIDTH:].astype(BF16),
        norm_ffn[l].reshape(1, d), w_gate_up[l].astype(BF16), w_down[l].astype(BF16))
    return out.reshape(b, t, d)
```

```python
import functools
import math

import numpy as np
import jax
import jax.numpy as jnp
from jax import lax
from jax.experimental import pallas as pl
from jax.experimental.pallas import tpu as pltpu

F32 = jnp.float32
BF16 = jnp.bfloat16

LANE = 128
MXU_WIDTH = 256

NSA_HEADS = 8
NSA_GROUPS = 2
NSA_REP = NSA_HEADS // NSA_GROUPS
HEAD_DIM = 64
CMP_LEN = 32
CMP_STRIDE = 16
CMP_HIDDEN = 128
SLC_LEN = 64
N_SELECT = 16
WINDOW = 512
Q_BLOCK = 128
HGRN_HEADS = 4
HGRN_DIM = 128
HGRN_CHUNK = 64
HGRN_SUB = 16
RMS_EPS = 1e-6
NEG = -1e30

NSA_WIDTH = NSA_HEADS * HEAD_DIM
KV_WIDTH = NSA_GROUPS * HEAD_DIM
HG_WIDTH = HGRN_HEADS * HGRN_DIM
N_GATES = 3 * NSA_HEADS

KEY_TILE = 128
KEY_UNIT = 4 * KEY_TILE
N_PAIRS = NSA_HEADS // 2
V_AUG = HEAD_DIM + 16
VMEM_LIMIT = 56 * 1024 * 1024


def _dot(a, b):
    return jnp.dot(a, b, preferred_element_type=F32)


def _dot_t(a, b):
    return lax.dot_general(a, b, (((1,), (1,)), ((), ())), preferred_element_type=F32)


def _group_rms(y, bd, gain):
    ss = _dot((y * y).astype(BF16), bd)
    return y * lax.rsqrt(ss * (1.0 / HEAD_DIM) + RMS_EPS) * gain


def _values_t_aug(v):
    vt = v.T
    ones = jnp.ones((V_AUG - HEAD_DIM, v.shape[0]), F32)
    return jnp.concatenate([vt[0:HEAD_DIM], ones, vt[HEAD_DIM:2 * HEAD_DIM], ones], axis=0).astype(BF16)


def _hgrn_stages(ph_ref, lb_ref, og_ref, o_ref, st_ref, chunks):
    c_len = HGRN_CHUNK
    n_sub = c_len // HGRN_SUB
    lg = lb_ref[...]
    lmax = jnp.max(lg, axis=0, keepdims=True)
    le = jnp.exp(lg - lmax)
    lb_all = le[0:1] / jnp.sum(le, axis=0, keepdims=True)

    ri = lax.broadcasted_iota(jnp.int32, (c_len, c_len), 0)
    ci = lax.broadcasted_iota(jnp.int32, (c_len, c_len), 1)
    tri = (ci <= ri).astype(BF16)
    causal = ci <= ri
    gain = og_ref[...]
    heads = [slice(h * HGRN_DIM, (h + 1) * HGRN_DIM) for h in range(HGRN_HEADS)]
    rows = [slice(c * c_len, (c + 1) * c_len) for c in range(chunks)]

    pre = []
    for rs in rows:
        qv = jax.nn.silu(ph_ref[rs, 0:HG_WIDTH])
        f = lb_all + (1.0 - lb_all) * jax.nn.sigmoid(ph_ref[rs, HG_WIDTH:2 * HG_WIDTH])
        logf = jnp.log(f)
        g1 = logf.astype(BF16)
        g2 = (logf - g1.astype(F32)).astype(BF16)
        pre.append((qv, 1.0 - f, g1, g2))
    his = [ph_ref[rs, 2 * HG_WIDTH:3 * HG_WIDTH] for rs in rows]
    gates = [jax.nn.silu(ph_ref[rs, 3 * HG_WIDTH:4 * HG_WIDTH]) for rs in rows]
    yield

    bcums = [_dot(tri, g1) + _dot(tri, g2) for _, _, g1, g2 in pre]
    yield

    mid = []
    for c in range(chunks):
        qv, kv = pre[c][0], pre[c][1]
        bcum = bcums[c]
        b_last = bcum[c_len - 1:c_len]
        starts = [jnp.zeros((1, HG_WIDTH), F32)] + [bcum[i * HGRN_SUB - 1:i * HGRN_SUB] for i in range(1, n_sub)]
        ends = starts[1:] + [b_last]

        def per_sub(vals):
            return jnp.concatenate([jnp.broadcast_to(r, (HGRN_SUB, HG_WIDTH)) for r in vals], axis=0)

        start_full = per_sub(starts)
        qd = qv * jnp.exp(bcum - start_full)
        kb = kv * jnp.exp(per_sub(ends) - bcum)
        kdiag = kv * jnp.exp(start_full - bcum)
        qs = (qd * per_sub([jnp.exp(s) for s in starts])).astype(BF16)
        kdec = (kb * per_sub([jnp.exp(b_last - e) for e in ends])).astype(BF16)
        kds = []
        for i in range(n_sub):
            blocks = [kb[j * HGRN_SUB:(j + 1) * HGRN_SUB] * jnp.exp(starts[i] - ends[j]) for j in range(i - 1)]
            if i > 0:
                blocks.append(kb[(i - 1) * HGRN_SUB:i * HGRN_SUB])
            blocks.append(kdiag[i * HGRN_SUB:(i + 1) * HGRN_SUB])
            if i + 1 < n_sub:
                blocks.append(jnp.zeros(((n_sub - 1 - i) * HGRN_SUB, HG_WIDTH), F32))
            kds.append(jnp.concatenate(blocks, axis=0).astype(BF16))
        mid.append((qd.astype(BF16), kds, qs, kdec, jnp.exp(b_last)))
    attn = {}
    upd = {}
    for c in range(chunks):
        qd, kds, qs, kdec, dec = mid[c]
        for h, hs in enumerate(heads):
            parts = [_dot_t(qd[i * HGRN_SUB:(i + 1) * HGRN_SUB, hs], kds[i][:, hs]) for i in range(n_sub)]
            attn[c, h] = jnp.where(causal, jnp.concatenate(parts, axis=0), 0.0).astype(BF16)
            upd[c, h] = _dot(his[c][:, hs].T.astype(BF16), kdec[:, hs])
    yield

    state = {}
    for h, hs in enumerate(heads):
        st = st_ref[h]
        for c in range(chunks):
            state[c, h] = st.astype(BF16)
            st = st * mid[c][4][:, hs] + upd[c, h]
        st_ref[h] = st
    for c, rs in enumerate(rows):
        qs = mid[c][2]
        for h, hs in enumerate(heads):
            o = _dot(attn[c, h], his[c][:, hs].astype(BF16)) + _dot_t(qs[:, hs], state[c, h])
            ms = jnp.mean(o * o, axis=-1, keepdims=True)
            o = o * lax.rsqrt(ms + RMS_EPS) * gain * gates[c][:, hs]
            o_ref[0, rs, hs] = o.astype(BF16)
    yield


def _in_proj_kernel(x_ref, g_ref, w_ref, bd_ref, qg_ref, ksg_ref, kwg_ref, lb_ref, og_ref,
                    qt_ref, kc_ref, vc_ref, ksa_ref, vst_ref, kw_ref, vwt_ref, gt_ref, ohg_ref,
                    wp_ref, ph_ref, st_ref):
    tm = x_ref.shape[1]
    d = x_ref.shape[2]
    j = pl.program_id(1)
    tile = jnp.minimum(j, pl.num_programs(1) - 2)
    o_g = NSA_WIDTH + 6 * KV_WIDTH

    @pl.when((pl.program_id(0) == 0) & (j == 0))
    def _():
        o_h = o_g + N_GATES
        rows_per = 128

        def copy_rows(i, carry):
            rs = pl.ds(pl.multiple_of(i * rows_per, rows_per), rows_per)
            wp_ref[rs, 0:o_g] = w_ref[rs, 0:o_g].astype(BF16)
            wp_ref[rs, o_g:o_g + 4 * HG_WIDTH] = w_ref[rs, o_h:o_h + 4 * HG_WIDTH].astype(BF16)
            tail = jnp.concatenate(
                [w_ref[rs, o_g:o_h], jnp.zeros((rows_per, LANE - N_GATES), F32)], axis=1)
            wp_ref[rs, o_g + 4 * HG_WIDTH:o_g + 4 * HG_WIDTH + LANE] = tail.astype(BF16)
            return carry

        lax.fori_loop(0, d // rows_per, copy_rows, 0)
        ph_ref[...] = jnp.zeros(ph_ref.shape, F32)
        st_ref[...] = jnp.zeros(st_ref.shape, F32)

    x = x_ref[0]
    ms = jnp.mean(x * x, axis=-1, keepdims=True)
    h = (x * lax.rsqrt(ms + RMS_EPS) * g_ref[...]).astype(BF16)

    mixer = _hgrn_stages(ph_ref, lb_ref, og_ref, ohg_ref, st_ref, tm // HGRN_CHUNK)
    next(mixer)
    y_nsa = _dot(h, wp_ref[:, 0:o_g])
    next(mixer)

    def hgrn_cols(k):
        return _dot(h, wp_ref[:, o_g + k * HG_WIDTH:o_g + (k + 1) * HG_WIDTH])

    y_h = [hgrn_cols(0), hgrn_cols(1)]
    next(mixer)

    bd = bd_ref[...]
    for r in range(NSA_WIDTH // LANE):
        sl = slice(r * LANE, (r + 1) * LANE)
        qt_ref[0, sl, :] = _group_rms(y_nsa[:, sl], bd, qg_ref[...]).T.astype(BF16)
    o = NSA_WIDTH
    kc_ref[0] = y_nsa[:, o:o + LANE].astype(BF16)
    vc_ref[0] = y_nsa[:, o + LANE:o + 2 * LANE].astype(BF16)
    ksa_ref[0, :, 0:LANE] = _group_rms(y_nsa[:, o + 2 * LANE:o + 3 * LANE], bd, ksg_ref[...]).astype(BF16)
    key = tile * tm + lax.broadcasted_iota(jnp.int32, (tm, LANE), 0)
    blk = lax.broadcasted_iota(jnp.int32, (tm, LANE), 1)
    ksa_ref[0, :, LANE:2 * LANE] = jnp.where(lax.shift_right_logical(key, 6) == blk, 1.0, 0.0).astype(BF16)
    vst_ref[0] = _values_t_aug(y_nsa[:, o + 3 * LANE:o + 4 * LANE])
    kw_ref[0] = _group_rms(y_nsa[:, o + 4 * LANE:o + 5 * LANE], bd, kwg_ref[...]).astype(BF16)
    vwt_ref[0] = _values_t_aug(y_nsa[:, o + 5 * LANE:o + 6 * LANE])
    y_gate = _dot(h, wp_ref[:, o_g + 4 * HG_WIDTH:o_g + 4 * HG_WIDTH + LANE])
    gt_ref[0] = jax.nn.sigmoid(y_gate).T
    y_h.append(hgrn_cols(2))
    next(mixer)
    y_h.append(hgrn_cols(3))
    for k in range(4):
        ph_ref[:, k * HG_WIDTH:(k + 1) * HG_WIDTH] = y_h[k]

    @pl.when(j == 0)
    def _():
        st_ref[...] = jnp.zeros(st_ref.shape, F32)


def _in_proj(x, g, w, bd, qg, ksg, kwg, lb_logits, o_gain, tm=256):
    b, t, d = x.shape
    n_t = t // tm
    nw = NSA_WIDTH + 6 * KV_WIDTH + 4 * HG_WIDTH + LANE
    rows = lambda i, j: (i, jnp.minimum(j, n_t - 1), 0)
    cols = lambda i, j: (i, 0, jnp.minimum(j, n_t - 1))
    prev = lambda i, j: (i, jnp.maximum(j - 1, 0), 0)
    const = lambda i, j: (0, 0)
    outs = [
        (jax.ShapeDtypeStruct((b, NSA_WIDTH, t), BF16), pl.BlockSpec((1, NSA_WIDTH, tm), cols)),
        (jax.ShapeDtypeStruct((b, t, LANE), BF16), pl.BlockSpec((1, tm, LANE), rows)),
        (jax.ShapeDtypeStruct((b, t, LANE), BF16), pl.BlockSpec((1, tm, LANE), rows)),
        (jax.ShapeDtypeStruct((b, t, 2 * LANE), BF16), pl.BlockSpec((1, tm, 2 * LANE), rows)),
        (jax.ShapeDtypeStruct((b, 2 * V_AUG, t), BF16), pl.BlockSpec((1, 2 * V_AUG, tm), cols)),
        (jax.ShapeDtypeStruct((b, t, LANE), BF16), pl.BlockSpec((1, tm, LANE), rows)),
        (jax.ShapeDtypeStruct((b, 2 * V_AUG, t), BF16), pl.BlockSpec((1, 2 * V_AUG, tm), cols)),
        (jax.ShapeDtypeStruct((b, LANE, t), F32), pl.BlockSpec((1, LANE, tm), cols)),
        (jax.ShapeDtypeStruct((b, t, HG_WIDTH), BF16), pl.BlockSpec((1, tm, HG_WIDTH), prev)),
    ]
    return pl.pallas_call(
        _in_proj_kernel,
        grid=(b, n_t + 1),
        in_specs=[
            pl.BlockSpec((1, tm, d), rows),
            pl.BlockSpec((1, d), const),
            pl.BlockSpec(w.shape, const, pipeline_mode=pl.Buffered(1)),
            pl.BlockSpec((LANE, LANE), const),
            pl.BlockSpec((1, LANE), const),
            pl.BlockSpec((1, LANE), const),
            pl.BlockSpec((1, LANE), const),
            pl.BlockSpec(lb_logits.shape, const),
            pl.BlockSpec((1, HGRN_DIM), const),
        ],
        out_specs=[s for _, s in outs],
        out_shape=[s for s, _ in outs],
        scratch_shapes=[
            pltpu.VMEM((d, nw), BF16),
            pltpu.VMEM((tm, 4 * HG_WIDTH), F32),
            pltpu.VMEM((HGRN_HEADS, HGRN_DIM, HGRN_DIM), F32),
        ],
        compiler_params=pltpu.CompilerParams(
            dimension_semantics=("arbitrary", "arbitrary"), vmem_limit_bytes=VMEM_LIMIT),
        name="in_proj",
    )(x, g, w, bd, qg, ksg, kwg, lb_logits, o_gain)


def _compress_kernel(xk_ref, xv_ref, wklo_ref, wkhi_ref, wk2_ref, pk_ref,
                     wvlo_ref, wvhi_ref, wv2_ref, pv_ref, bd_ref, kg_ref, kc_ref, vct_ref):
    ns = xk_ref.shape[1]

    def mlp(x_ref, wlo_ref, whi_ref, w2_ref, p_ref):
        xb = x_ref[0]
        a = _dot(xb, wlo_ref[...])
        b = _dot(xb, whi_ref[...])
        h = a + pltpu.roll(b, ns - 1, axis=0) + p_ref[...]
        return _dot(jax.nn.gelu(h).astype(BF16), w2_ref[...])

    kc = mlp(xk_ref, wklo_ref, wkhi_ref, wk2_ref, pk_ref)
    kc_ref[0] = _group_rms(kc, bd_ref[...], kg_ref[...]).astype(BF16)
    vct_ref[0] = _values_t_aug(mlp(xv_ref, wvlo_ref, wvhi_ref, wv2_ref, pv_ref))


def _compress(xk, xv, wk, wv, bd, kg):
    b, ns, wd = xk.shape
    const2 = lambda i: (0, 0)
    bat = lambda i: (i, 0, 0)
    wspecs = [pl.BlockSpec(w.shape, const2) for w in wk] + [pl.BlockSpec(w.shape, const2) for w in wv]
    return pl.pallas_call(
        _compress_kernel,
        grid=(b,),
        in_specs=[pl.BlockSpec((1, ns, wd), bat), pl.BlockSpec((1, ns, wd), bat)] + wspecs + [
            pl.BlockSpec((LANE, LANE), const2), pl.BlockSpec((1, LANE), const2)],
        out_specs=[pl.BlockSpec((1, ns, LANE), bat), pl.BlockSpec((1, 2 * V_AUG, ns), bat)],
        out_shape=[jax.ShapeDtypeStruct((b, ns, LANE), BF16), jax.ShapeDtypeStruct((b, 2 * V_AUG, ns), BF16)],
        compiler_params=pltpu.CompilerParams(
            dimension_semantics=("arbitrary",), vmem_limit_bytes=VMEM_LIMIT),
        name="compress",
    )(xk, xv, *wk, *wv, bd, kg)


def _topk_bias_t(imp_ts, s0):
    nj, nq = imp_ts[0].shape
    jidx = lax.broadcasted_iota(jnp.int32, (nj, nq), 0)
    tq = s0 + lax.broadcasted_iota(jnp.int32, (nj, nq), 1)
    jcur = lax.shift_right_logical(tq, 6)
    forced = (jidx == 0) | (jidx == jcur) | (jidx == jcur - 1)
    future = jidx > jcur
    vs = [jnp.where(forced, -2.0, jnp.where(future, -1.0, imp_t)) for imp_t in imp_ts]
    jf = jidx.astype(F32)
    for _ in range(N_SELECT - 3):
        for g in range(len(vs)):
            m = jnp.max(vs[g], axis=0, keepdims=True)
            jm = jnp.min(jnp.where(vs[g] == m, jf, float(nj)), axis=0, keepdims=True)
            vs[g] = jnp.where(jf == jm, -2.0, vs[g])
    return [jnp.where(future, NEG, jnp.where(v < -1.5, 0.0, NEG)) for v in vs]


def _nsa_kernel(qt_ref, qtn_ref, ksa_ref, vst_ref, kw_ref, vwt_ref, kc_ref, vct_ref, gt_ref, gtn_ref, ovt_ref, o_ref,
                w_ref, wn_ref, bias_ref, cmp_ref, sc_ref, sw_ref, ss_ref, mx_ref, e_ref, ew_ref,
                m_ref, acc_ref, out_ref):
    qb = pl.program_id(1)
    s0 = qb * Q_BLOCK
    nc = kc_ref.shape[1]
    two = 2 * Q_BLOCK
    lane_q = lax.broadcasted_iota(jnp.int32, (1, two), 1) & (Q_BLOCK - 1)
    t_lane = s0 + lane_q
    gt = gt_ref[0]

    def stationary_q(q_ref, dst):
        zero = jnp.zeros((HEAD_DIM, Q_BLOCK), BF16)
        for p in range(N_PAIRS):
            g = (2 * p) // NSA_REP
            halves = []
            for h in (2 * p, 2 * p + 1):
                qh = q_ref[0, h * HEAD_DIM:(h + 1) * HEAD_DIM, :]
                halves.append(jnp.concatenate([qh, zero] if g == 0 else [zero, qh], axis=0))
            dst[p, 0:LANE, :] = jnp.concatenate(halves, axis=1)

    def store_scores(dst, s, valid):
        if valid is not None:
            s = jnp.where(valid, s, NEG)
        dst[...] = s
        return jnp.max(s, axis=0, keepdims=True)

    def exp_tiles(load, n_tiles, m, e_dst):
        for c in range(n_tiles):
            rows = slice(c * KEY_TILE, (c + 1) * KEY_TILE)
            e_dst[rows, :] = jnp.exp2(load(rows) - m).astype(BF16)

    def values_t(ref, g, cols):
        return ref[0, g * V_AUG:(g + 1) * V_AUG, cols]

    def emit(dst, gates, p, branch, o_aug, first, guard=None):
        inv = 1.0 / o_aug[HEAD_DIM:HEAD_DIM + 1, :]
        if guard is not None:
            inv = jnp.where(guard, inv, 0.0)
        for hh in range(2):
            h = 2 * p + hh
            ls = slice(hh * Q_BLOCK, (hh + 1) * Q_BLOCK)
            o = o_aug[0:HEAD_DIM, ls] * (inv[:, ls] * gates[3 * h + branch:3 * h + branch + 1, :])
            rs = slice(h * HEAD_DIM, (h + 1) * HEAD_DIM)
            if first:
                dst[rs, :] = o
            else:
                dst[rs, :] += o
        return inv

    def key_rows(n):
        return lax.broadcasted_iota(jnp.int32, (n, two), 0)

    def cmp_scores(q_ref, s0x):
        stationary_q(q_ref, wn_ref)
        kc = kc_ref[0]
        valid = key_rows(nc) <= lax.shift_right_arithmetic(s0x + lane_q - (CMP_LEN - 1), 4)
        return [store_scores(sc_ref.at[p], _dot(kc, wn_ref[p]), valid) for p in range(N_PAIRS)]

    def cmp_finish(gates, m_cmp):
        imp_t = [jnp.zeros((LANE, Q_BLOCK), F32) for _ in range(NSA_GROUPS)]
        for p in range(N_PAIRS):
            exp_tiles(lambda rows: sc_ref[p, rows, :], nc // KEY_TILE, m_cmp[p], e_ref.at[p])
        for p in range(N_PAIRS):
            g = (2 * p) // NSA_REP
            inv = emit(cmp_ref, gates, p, 0, _dot(values_t(vct_ref, g, slice(None)), e_ref[p, 0:nc, :]), True,
                       guard=m_cmp[p] > 0.5 * NEG)
            imp_p = _dot(ovt_ref[...], e_ref[p, 0:nc, :]) * inv
            imp_t[g] = imp_t[g] + imp_p[:, 0:Q_BLOCK] + imp_p[:, Q_BLOCK:two]
        return imp_t

    def select(imp_t, s0x):
        for g, bias_t in enumerate(_topk_bias_t(imp_t, s0x)):
            bias_ref[g] = bias_t.astype(BF16)

    @pl.when(qb == 0)
    def _():
        select(cmp_finish(gt, cmp_scores(qt_ref, s0)), s0)

    w0 = pl.multiple_of(jnp.maximum(s0 - WINDOW, 0), KEY_TILE)
    wlen = WINDOW + Q_BLOCK
    kwin = kw_ref[0, pl.ds(w0, wlen), :]
    vwin = [values_t(vwt_ref, g, pl.ds(w0, wlen)) for g in range(NSA_GROUPS)]

    stationary_q(qt_ref, w_ref)
    for p in range(N_PAIRS):
        bias_t = bias_ref[(2 * p) // NSA_REP]
        w_ref[p, LANE:2 * LANE, :] = jnp.concatenate([bias_t, bias_t], axis=1)
    out_ref[...] = cmp_ref[...]

    imp_t = cmp_finish(gtn_ref[0], cmp_scores(qtn_ref, s0 + Q_BLOCK))
    diff = (t_lane - w0) - key_rows(wlen)
    win_valid = lax.shift_right_arithmetic(diff, WINDOW.bit_length() - 1) == 0
    m_win = [store_scores(sw_ref.at[p], _dot(kwin, w_ref[p, 0:LANE, :]), win_valid) for p in range(N_PAIRS)]
    for p in range(N_PAIRS):
        exp_tiles(lambda rows: sw_ref[p, rows, :], wlen // KEY_TILE, m_win[p], ew_ref.at[p])
    for p in range(N_PAIRS):
        emit(out_ref, gt, p, 2, _dot(vwin[(2 * p) // NSA_REP], ew_ref[p]), False)

    m_ref[...] = jnp.full(m_ref.shape, NEG, F32)
    acc_ref[...] = jnp.zeros(acc_ref.shape, F32)

    def sel_scores(u, buf, p, causal):
        k0 = pl.multiple_of(u * KEY_UNIT, KEY_UNIT)
        ku = ksa_ref[0, pl.ds(k0, KEY_UNIT), :]
        valid = (k0 + key_rows(KEY_UNIT)) <= t_lane if causal else None
        mx_ref[buf, p] = store_scores(ss_ref.at[buf, p], _dot(ku, w_ref[p]), valid)

    def sel_update(u, buf, p):
        k0 = pl.multiple_of(u * KEY_UNIT, KEY_UNIT)
        g = (2 * p) // NSA_REP
        m_old = m_ref[p]
        m = jnp.maximum(m_old, mx_ref[buf, p])
        exp_tiles(lambda rows: ss_ref[buf, p, rows, :], KEY_UNIT // KEY_TILE, m, e_ref.at[p])
        m_ref[p] = m
        acc_ref[p] = jnp.exp2(m_old - m) * acc_ref[p] + _dot(
            values_t(vst_ref, g, pl.ds(k0, KEY_UNIT)), e_ref[p, 0:KEY_UNIT, :])

    def sel_step(u, buf, next_causal):
        for p in range(N_PAIRS):
            sel_scores(u + 1, 1 - buf, p, next_causal)
            sel_update(u, buf, p)

    def sel_last(u, buf):
        for p in range(N_PAIRS):
            sel_update(u, buf, p)

    n_before = qb // (KEY_UNIT // Q_BLOCK)
    n_trips = jnp.maximum(n_before - 1, 0) // 2
    rest = 2 * n_trips

    for p in range(N_PAIRS):
        sel_scores(0, 0, p, False)
    select(imp_t, s0 + Q_BLOCK)

    def body(i, carry):
        sel_step(2 * i, 0, False)
        sel_step(2 * i + 1, 1, False)
        return carry

    def body2(i, carry):
        body(2 * i, carry)
        return body(2 * i + 1, carry)

    lax.fori_loop(0, n_trips // 2, body2, 0)
    lax.fori_loop(2 * (n_trips // 2), n_trips, body, 0)

    @pl.when(n_before == 0)
    def _():
        for p in range(N_PAIRS):
            sel_scores(0, 0, p, True)
        sel_last(0, 0)

    @pl.when(n_before == rest + 1)
    def _():
        sel_step(rest, 0, True)
        sel_last(rest + 1, 1)

    @pl.when(n_before == rest + 2)
    def _():
        sel_step(rest, 0, False)
        sel_step(rest + 1, 1, True)
        sel_last(rest + 2, 0)

    for p in range(N_PAIRS):
        emit(out_ref, gt, p, 1, acc_ref[p], False)

    o_ref[0] = out_ref[...].T.astype(BF16)


def _nsa(qt, ksa, vst, kw, vwt, kc, vct, gt, ovt):
    b, _, t = qt.shape
    nc = kc.shape[1]
    n_qb = t // Q_BLOCK
    qcol = lambda i, j: (i, 0, j)
    qnext = lambda i, j: (i, 0, jnp.minimum(j + 1, n_qb - 1))
    full = lambda i, j: (i, 0, 0)
    const = lambda i, j: (0, 0)
    return pl.pallas_call(
        _nsa_kernel,
        grid=(b, n_qb),
        in_specs=[
            pl.BlockSpec((1, NSA_WIDTH, Q_BLOCK), qcol),
            pl.BlockSpec((1, NSA_WIDTH, Q_BLOCK), qnext),
            pl.BlockSpec((1, t, 2 * LANE), full),
            pl.BlockSpec((1, 2 * V_AUG, t), full),
            pl.BlockSpec((1, t, LANE), full),
            pl.BlockSpec((1, 2 * V_AUG, t), full),
            pl.BlockSpec((1, nc, LANE), full),
            pl.BlockSpec((1, 2 * V_AUG, nc), full),
            pl.BlockSpec((1, LANE, Q_BLOCK), qcol),
            pl.BlockSpec((1, LANE, Q_BLOCK), qnext),
            pl.BlockSpec(ovt.shape, const),
        ],
        out_specs=pl.BlockSpec((1, Q_BLOCK, NSA_WIDTH), lambda i, j: (i, j, 0)),
        out_shape=jax.ShapeDtypeStruct((b, t, NSA_WIDTH), BF16),
        scratch_shapes=[
            pltpu.VMEM((N_PAIRS, 2 * LANE, 2 * Q_BLOCK), BF16),
            pltpu.VMEM((N_PAIRS, LANE, 2 * Q_BLOCK), BF16),
            pltpu.VMEM((NSA_GROUPS, LANE, Q_BLOCK), BF16),
            pltpu.VMEM((NSA_WIDTH, Q_BLOCK), F32),
            pltpu.VMEM((N_PAIRS, nc, 2 * Q_BLOCK), F32),
            pltpu.VMEM((N_PAIRS, WINDOW + Q_BLOCK, 2 * Q_BLOCK), F32),
            pltpu.VMEM((2, N_PAIRS, KEY_UNIT, 2 * Q_BLOCK), F32),
            pltpu.VMEM((2, N_PAIRS, 1, 2 * Q_BLOCK), F32),
            pltpu.VMEM((N_PAIRS, max(nc, KEY_UNIT), 2 * Q_BLOCK), BF16),
            pltpu.VMEM((N_PAIRS, WINDOW + Q_BLOCK, 2 * Q_BLOCK), BF16),
            pltpu.VMEM((N_PAIRS, 1, 2 * Q_BLOCK), F32),
            pltpu.VMEM((N_PAIRS, V_AUG, 2 * Q_BLOCK), F32),
            pltpu.VMEM((NSA_WIDTH, Q_BLOCK), F32),
        ],
        compiler_params=pltpu.CompilerParams(
            dimension_semantics=("arbitrary", "arbitrary"), vmem_limit_bytes=VMEM_LIMIT),
        name="nsa",
    )(qt, qt, ksa, vst, kw, vwt, kc, vct, gt, gt, ovt)


def _ff_chunks(d_ff, width):
    return tuple((s, min(width, d_ff - s)) for s in range(0, d_ff, width))


def _out_ffn_kernel(x_ref, on_ref, oh_ref, won_ref, woh_ref, g_ref, wgu_ref, wd_ref, o_ref, *, chunks):
    d_ff = wd_ref.shape[0]
    x1 = x_ref[...] + _dot(on_ref[...], won_ref[...]) + _dot(oh_ref[...], woh_ref[...])
    ms = jnp.mean(x1 * x1, axis=-1, keepdims=True)
    h = (x1 * lax.rsqrt(ms + RMS_EPS) * g_ref[...]).astype(BF16)
    o_ref[...] = x1
    for s, n in chunks:
        gate = _dot(h, wgu_ref[:, s:s + n])
        up = _dot(h, wgu_ref[:, d_ff + s:d_ff + s + n])
        act = (jax.nn.silu(gate) * up).astype(BF16)
        o_ref[...] += _dot(act, wd_ref[s:s + n, :])


def _out_ffn(x2, o_nsa, o_hg, won, woh, g, wgu, wd, tm=512, ff_width=4 * MXU_WIDTH):
    n, d = x2.shape
    d_ff = wd.shape[0]
    row = lambda i: (i, 0)
    const = lambda i: (0, 0)
    resident = lambda a: pl.BlockSpec(a.shape, const, pipeline_mode=pl.Buffered(1))
    return pl.pallas_call(
        functools.partial(_out_ffn_kernel, chunks=_ff_chunks(d_ff, ff_width)),
        grid=(n // tm,),
        in_specs=[
            pl.BlockSpec((tm, d), row),
            pl.BlockSpec((tm, NSA_WIDTH), row),
            pl.BlockSpec((tm, HG_WIDTH), row),
            resident(won),
            resident(woh),
            pl.BlockSpec((1, d), const),
            resident(wgu),
            resident(wd),
        ],
        out_specs=pl.BlockSpec((tm, d), row),
        out_shape=jax.ShapeDtypeStruct((n, d), F32),
        compiler_params=pltpu.CompilerParams(
            dimension_semantics=("arbitrary",), vmem_limit_bytes=VMEM_LIMIT),
        name="out_ffn",
    )(x2, o_nsa, o_hg, won, woh, g, wgu, wd)


def _expand_cmp_weights(pos, w1, w2):
    eye = jnp.eye(NSA_GROUPS, dtype=F32)
    w1r = w1.reshape(CMP_LEN, HEAD_DIM, CMP_HIDDEN)

    def lift(wpart):
        return jnp.einsum('ldc,gk->lgdkc', wpart, eye).reshape(
            CMP_STRIDE * NSA_GROUPS * HEAD_DIM, NSA_GROUPS * CMP_HIDDEN)

    wlo = lift(w1r[:CMP_STRIDE]).astype(BF16)
    whi = lift(w1r[CMP_STRIDE:]).astype(BF16)
    w2x = jnp.einsum('cd,gk->gckd', w2, eye).reshape(NSA_GROUPS * CMP_HIDDEN, NSA_GROUPS * HEAD_DIM).astype(BF16)
    pbias = jnp.tile(pos.reshape(1, CMP_LEN * HEAD_DIM) @ w1, (1, NSA_GROUPS))
    return wlo, whi, w2x, pbias


def _mixers(x, norm_mix, w_in, q_norm, k_norm, cmp_pos_k, cmp_pos_v, cmp_k_w1, cmp_k_w2, cmp_v_w1, cmp_v_w2,
            hgrn_lb_logits, hgrn_o_norm):
    b, t, d = x.shape
    depth = norm_mix.shape[0]
    assert depth == 1 and hgrn_lb_logits.shape[0] == 2
    assert t % KEY_UNIT == 0 and t >= WINDOW + Q_BLOCK and t // SLC_LEN <= LANE
    assert (t // CMP_STRIDE) % KEY_TILE == 0
    l = 0

    bd = jnp.asarray(np.kron(np.eye(LANE // HEAD_DIM), np.ones((HEAD_DIM, HEAD_DIM))), BF16)
    tile2 = lambda v: jnp.tile(v.reshape(1, HEAD_DIM), (1, LANE // HEAD_DIM)).astype(F32)
    qg = tile2(q_norm[l]) * (HEAD_DIM ** -0.5 * math.log2(math.e))

    qt, kc_raw, vc_raw, ksa, vst, kw, vwt, gt, o_hg = _in_proj(
        x, norm_mix[l].reshape(1, d), w_in[l], bd, qg, tile2(k_norm[l, 1]), tile2(k_norm[l, 2]),
        hgrn_lb_logits, hgrn_o_norm[l].reshape(1, HGRN_DIM))

    ns = t // CMP_STRIDE
    seg_w = CMP_STRIDE * KV_WIDTH
    kc, vct = _compress(
        kc_raw.reshape(b, ns, seg_w), vc_raw.reshape(b, ns, seg_w),
        _expand_cmp_weights(cmp_pos_k[l], cmp_k_w1[l], cmp_k_w2[l]),
        _expand_cmp_weights(cmp_pos_v[l], cmp_v_w1[l], cmp_v_w2[l]),
        bd, tile2(k_norm[l, 0]))

    cs = np.arange(ns)[None, :] * CMP_STRIDE
    ss = np.arange(LANE)[:, None] * SLC_LEN
    ovt = jnp.asarray(((cs < ss + SLC_LEN) & (cs + CMP_LEN > ss)).astype(np.float32), BF16)

    o_nsa = _nsa(qt, ksa, vst, kw, vwt, kc, vct, gt, ovt)
    return o_nsa, o_hg


def kernel(x, norm_mix, w_in, q_norm, k_norm, cmp_pos_k, cmp_pos_v, cmp_k_w1, cmp_k_w2, cmp_v_w1, cmp_v_w2,
           hgrn_lb_logits, hgrn_o_norm, w_out, norm_ffn, w_gate_up, w_down):
    b, t, d = x.shape
    l = 0
    o_nsa, o_hg = _mixers(x, norm_mix, w_in, q_norm, k_norm, cmp_pos_k, cmp_pos_v, cmp_k_w1, cmp_k_w2,
                          cmp_v_w1, cmp_v_w2, hgrn_lb_logits, hgrn_o_norm)
    wo = w_out[l]
    out = _out_ffn(
        x.reshape(b * t, d), o_nsa.reshape(b * t, NSA_WIDTH), o_hg.reshape(b * t, HG_WIDTH),
        wo[:NSA_WIDTH].astype(BF16), wo[NSA_WIDTH:].astype(BF16),
        norm_ffn[l].reshape(1, d), w_gate_up[l].astype(BF16), w_down[l].astype(BF16))
    return out.reshape(b, t, d)
```

```python
import functools
import math

import numpy as np
import jax
import jax.numpy as jnp
from jax import lax
from jax.experimental import pallas as pl
from jax.experimental.pallas import tpu as pltpu

F32 = jnp.float32
BF16 = jnp.bfloat16

LANE = 128
MXU_WIDTH = 256

NSA_HEADS = 8
NSA_GROUPS = 2
NSA_REP = NSA_HEADS // NSA_GROUPS
HEAD_DIM = 64
CMP_LEN = 32
CMP_STRIDE = 16
CMP_HIDDEN = 128
SLC_LEN = 64
N_SELECT = 16
WINDOW = 512
Q_BLOCK = 128
HGRN_HEADS = 4
HGRN_DIM = 128
HGRN_CHUNK = 64
HGRN_SUB = 16
RMS_EPS = 1e-6
NEG = -1e30

NSA_WIDTH = NSA_HEADS * HEAD_DIM
KV_WIDTH = NSA_GROUPS * HEAD_DIM
HG_WIDTH = HGRN_HEADS * HGRN_DIM
N_GATES = 3 * NSA_HEADS

KEY_TILE = 128
KEY_UNIT = 4 * KEY_TILE
N_PAIRS = NSA_HEADS // 2
V_AUG = HEAD_DIM + 16
VMEM_LIMIT = 56 * 1024 * 1024


def _dot(a, b):
    return jnp.dot(a, b, preferred_element_type=F32)


def _dot_t(a, b):
    return lax.dot_general(a, b, (((1,), (1,)), ((), ())), preferred_element_type=F32)


def _group_rms(y, bd, gain):
    ss = _dot((y * y).astype(BF16), bd)
    return y * lax.rsqrt(ss * (1.0 / HEAD_DIM) + RMS_EPS) * gain


def _values_t_aug(v):
    vt = v.T
    ones = jnp.ones((V_AUG - HEAD_DIM, v.shape[0]), F32)
    return jnp.concatenate([vt[0:HEAD_DIM], ones, vt[HEAD_DIM:2 * HEAD_DIM], ones], axis=0).astype(BF16)


def _hgrn_stages(ph_ref, lb_ref, og_ref, o_ref, st_ref, chunks):
    c_len = HGRN_CHUNK
    n_sub = c_len // HGRN_SUB
    lg = lb_ref[...]
    lmax = jnp.max(lg, axis=0, keepdims=True)
    le = jnp.exp(lg - lmax)
    lb_all = le[0:1] / jnp.sum(le, axis=0, keepdims=True)

    ri = lax.broadcasted_iota(jnp.int32, (c_len, c_len), 0)
    ci = lax.broadcasted_iota(jnp.int32, (c_len, c_len), 1)
    tri = (ci <= ri).astype(BF16)
    causal = ci <= ri
    gain = og_ref[...]
    heads = [slice(h * HGRN_DIM, (h + 1) * HGRN_DIM) for h in range(HGRN_HEADS)]
    rows = [slice(c * c_len, (c + 1) * c_len) for c in range(chunks)]

    pre = []
    for rs in rows:
        qv = jax.nn.silu(ph_ref[rs, 0:HG_WIDTH])
        f = lb_all + (1.0 - lb_all) * jax.nn.sigmoid(ph_ref[rs, HG_WIDTH:2 * HG_WIDTH])
        logf = jnp.log(f)
        g1 = logf.astype(BF16)
        g2 = (logf - g1.astype(F32)).astype(BF16)
        pre.append((qv, 1.0 - f, g1, g2))
    his = [ph_ref[rs, 2 * HG_WIDTH:3 * HG_WIDTH] for rs in rows]
    gates = [jax.nn.silu(ph_ref[rs, 3 * HG_WIDTH:4 * HG_WIDTH]) for rs in rows]
    yield

    bcums = [_dot(tri, g1) + _dot(tri, g2) for _, _, g1, g2 in pre]
    yield

    mid = []
    for c in range(chunks):
        qv, kv = pre[c][0], pre[c][1]
        bcum = bcums[c]
        b_last = bcum[c_len - 1:c_len]
        starts = [jnp.zeros((1, HG_WIDTH), F32)] + [bcum[i * HGRN_SUB - 1:i * HGRN_SUB] for i in range(1, n_sub)]
        ends = starts[1:] + [b_last]

        def per_sub(vals):
            return jnp.concatenate([jnp.broadcast_to(r, (HGRN_SUB, HG_WIDTH)) for r in vals], axis=0)

        start_full = per_sub(starts)
        qd = qv * jnp.exp(bcum - start_full)
        kb = kv * jnp.exp(per_sub(ends) - bcum)
        kdiag = kv * jnp.exp(start_full - bcum)
        qs = (qd * per_sub([jnp.exp(s) for s in starts])).astype(BF16)
        kdec = (kb * per_sub([jnp.exp(b_last - e) for e in ends])).astype(BF16)
        kds = []
        for i in range(n_sub):
            blocks = [kb[j * HGRN_SUB:(j + 1) * HGRN_SUB] * jnp.exp(starts[i] - ends[j]) for j in range(i - 1)]
            if i > 0:
                blocks.append(kb[(i - 1) * HGRN_SUB:i * HGRN_SUB])
            blocks.append(kdiag[i * HGRN_SUB:(i + 1) * HGRN_SUB])
            if i + 1 < n_sub:
                blocks.append(jnp.zeros(((n_sub - 1 - i) * HGRN_SUB, HG_WIDTH), F32))
            kds.append(jnp.concatenate(blocks, axis=0).astype(BF16))
        mid.append((qd.astype(BF16), kds, qs, kdec, jnp.exp(b_last)))
    attn = {}
    upd = {}
    for c in range(chunks):
        qd, kds, qs, kdec, dec = mid[c]
        for h, hs in enumerate(heads):
            parts = [_dot_t(qd[i * HGRN_SUB:(i + 1) * HGRN_SUB, hs], kds[i][:, hs]) for i in range(n_sub)]
            attn[c, h] = jnp.where(causal, jnp.concatenate(parts, axis=0), 0.0).astype(BF16)
            upd[c, h] = _dot(his[c][:, hs].T.astype(BF16), kdec[:, hs])
    yield

    state = {}
    for h, hs in enumerate(heads):
        st = st_ref[h]
        for c in range(chunks):
            state[c, h] = st.astype(BF16)
            st = st * mid[c][4][:, hs] + upd[c, h]
        st_ref[h] = st
    for c, rs in enumerate(rows):
        qs = mid[c][2]
        for h, hs in enumerate(heads):
            o = _dot(attn[c, h], his[c][:, hs].astype(BF16)) + _dot_t(qs[:, hs], state[c, h])
            ms = jnp.mean(o * o, axis=-1, keepdims=True)
            o = o * lax.rsqrt(ms + RMS_EPS) * gain * gates[c][:, hs]
            o_ref[0, rs, hs] = o.astype(BF16)
    yield


def _in_proj_kernel(x_ref, g_ref, w_ref, bd_ref, qg_ref, ksg_ref, kwg_ref, lb_ref, og_ref,
                    qt_ref, kc_ref, vc_ref, ksa_ref, vst_ref, kw_ref, vwt_ref, gt_ref, ohg_ref,
                    wp_ref, ph_ref, st_ref):
    tm = x_ref.shape[1]
    d = x_ref.shape[2]
    j = pl.program_id(1)
    tile = jnp.minimum(j, pl.num_programs(1) - 2)
    o_g = NSA_WIDTH + 6 * KV_WIDTH

    @pl.when((pl.program_id(0) == 0) & (j == 0))
    def _():
        o_h = o_g + N_GATES
        rows_per = 128

        def copy_rows(i, carry):
            rs = pl.ds(pl.multiple_of(i * rows_per, rows_per), rows_per)
            wp_ref[rs, 0:o_g] = w_ref[rs, 0:o_g].astype(BF16)
            wp_ref[rs, o_g:o_g + 4 * HG_WIDTH] = w_ref[rs, o_h:o_h + 4 * HG_WIDTH].astype(BF16)
            tail = jnp.concatenate(
                [w_ref[rs, o_g:o_h], jnp.zeros((rows_per, LANE - N_GATES), F32)], axis=1)
            wp_ref[rs, o_g + 4 * HG_WIDTH:o_g + 4 * HG_WIDTH + LANE] = tail.astype(BF16)
            return carry

        lax.fori_loop(0, d // rows_per, copy_rows, 0)
        ph_ref[...] = jnp.zeros(ph_ref.shape, F32)
        st_ref[...] = jnp.zeros(st_ref.shape, F32)

    x = x_ref[0]
    ms = jnp.mean(x * x, axis=-1, keepdims=True)
    h = (x * lax.rsqrt(ms + RMS_EPS) * g_ref[...]).astype(BF16)

    mixer = _hgrn_stages(ph_ref, lb_ref, og_ref, ohg_ref, st_ref, tm // HGRN_CHUNK)
    next(mixer)
    y_nsa = _dot(h, wp_ref[:, 0:o_g])
    next(mixer)

    def hgrn_cols(k):
        return _dot(h, wp_ref[:, o_g + k * HG_WIDTH:o_g + (k + 1) * HG_WIDTH])

    y_h = [hgrn_cols(0), hgrn_cols(1)]
    next(mixer)

    bd = bd_ref[...]
    for r in range(NSA_WIDTH // LANE):
        sl = slice(r * LANE, (r + 1) * LANE)
        qt_ref[0, sl, :] = _group_rms(y_nsa[:, sl], bd, qg_ref[...]).T.astype(BF16)
    o = NSA_WIDTH
    kc_ref[0] = y_nsa[:, o:o + LANE].astype(BF16)
    vc_ref[0] = y_nsa[:, o + LANE:o + 2 * LANE].astype(BF16)
    ksa_ref[0, :, 0:LANE] = _group_rms(y_nsa[:, o + 2 * LANE:o + 3 * LANE], bd, ksg_ref[...]).astype(BF16)
    key = tile * tm + lax.broadcasted_iota(jnp.int32, (tm, LANE), 0)
    blk = lax.broadcasted_iota(jnp.int32, (tm, LANE), 1)
    ksa_ref[0, :, LANE:2 * LANE] = jnp.where(lax.shift_right_logical(key, 6) == blk, 1.0, 0.0).astype(BF16)
    vst_ref[0] = _values_t_aug(y_nsa[:, o + 3 * LANE:o + 4 * LANE])
    kw_ref[0] = _group_rms(y_nsa[:, o + 4 * LANE:o + 5 * LANE], bd, kwg_ref[...]).astype(BF16)
    vwt_ref[0] = _values_t_aug(y_nsa[:, o + 5 * LANE:o + 6 * LANE])
    y_gate = _dot(h, wp_ref[:, o_g + 4 * HG_WIDTH:o_g + 4 * HG_WIDTH + LANE])
    gt_ref[0] = jax.nn.sigmoid(y_gate).T
    y_h.append(hgrn_cols(2))
    next(mixer)
    y_h.append(hgrn_cols(3))
    for k in range(4):
        ph_ref[:, k * HG_WIDTH:(k + 1) * HG_WIDTH] = y_h[k]

    @pl.when(j == 0)
    def _():
        st_ref[...] = jnp.zeros(st_ref.shape, F32)


def _in_proj(x, g, w, bd, qg, ksg, kwg, lb_logits, o_gain, tm=256):
    b, t, d = x.shape
    n_t = t // tm
    nw = NSA_WIDTH + 6 * KV_WIDTH + 4 * HG_WIDTH + LANE
    rows = lambda i, j: (i, jnp.minimum(j, n_t - 1), 0)
    cols = lambda i, j: (i, 0, jnp.minimum(j, n_t - 1))
    prev = lambda i, j: (i, jnp.maximum(j - 1, 0), 0)
    const = lambda i, j: (0, 0)
    outs = [
        (jax.ShapeDtypeStruct((b, NSA_WIDTH, t), BF16), pl.BlockSpec((1, NSA_WIDTH, tm), cols)),
        (jax.ShapeDtypeStruct((b, t, LANE), BF16), pl.BlockSpec((1, tm, LANE), rows)),
        (jax.ShapeDtypeStruct((b, t, LANE), BF16), pl.BlockSpec((1, tm, LANE), rows)),
        (jax.ShapeDtypeStruct((b, t, 2 * LANE), BF16), pl.BlockSpec((1, tm, 2 * LANE), rows)),
        (jax.ShapeDtypeStruct((b, 2 * V_AUG, t), BF16), pl.BlockSpec((1, 2 * V_AUG, tm), cols)),
        (jax.ShapeDtypeStruct((b, t, LANE), BF16), pl.BlockSpec((1, tm, LANE), rows)),
        (jax.ShapeDtypeStruct((b, 2 * V_AUG, t), BF16), pl.BlockSpec((1, 2 * V_AUG, tm), cols)),
        (jax.ShapeDtypeStruct((b, LANE, t), F32), pl.BlockSpec((1, LANE, tm), cols)),
        (jax.ShapeDtypeStruct((b, t, HG_WIDTH), BF16), pl.BlockSpec((1, tm, HG_WIDTH), prev)),
    ]
    return pl.pallas_call(
        _in_proj_kernel,
        grid=(b, n_t + 1),
        in_specs=[
            pl.BlockSpec((1, tm, d), rows),
            pl.BlockSpec((1, d), const),
            pl.BlockSpec(w.shape, const, pipeline_mode=pl.Buffered(1)),
            pl.BlockSpec((LANE, LANE), const),
            pl.BlockSpec((1, LANE), const),
            pl.BlockSpec((1, LANE), const),
            pl.BlockSpec((1, LANE), const),
            pl.BlockSpec(lb_logits.shape, const),
            pl.BlockSpec((1, HGRN_DIM), const),
        ],
        out_specs=[s for _, s in outs],
        out_shape=[s for s, _ in outs],
        scratch_shapes=[
            pltpu.VMEM((d, nw), BF16),
            pltpu.VMEM((tm, 4 * HG_WIDTH), F32),
            pltpu.VMEM((HGRN_HEADS, HGRN_DIM, HGRN_DIM), F32),
        ],
        compiler_params=pltpu.CompilerParams(
            dimension_semantics=("arbitrary", "arbitrary"), vmem_limit_bytes=VMEM_LIMIT),
        name="in_proj",
    )(x, g, w, bd, qg, ksg, kwg, lb_logits, o_gain)


def _compress_kernel(xk_ref, xv_ref, wklo_ref, wkhi_ref, wk2_ref, pk_ref,
                     wvlo_ref, wvhi_ref, wv2_ref, pv_ref, bd_ref, kg_ref, kc_ref, vct_ref):
    ns = xk_ref.shape[1]

    def mlp(x_ref, wlo_ref, whi_ref, w2_ref, p_ref):
        xb = x_ref[0]
        a = _dot(xb, wlo_ref[...])
        b = _dot(xb, whi_ref[...])
        h = a + pltpu.roll(b, ns - 1, axis=0) + p_ref[...]
        return _dot(jax.nn.gelu(h).astype(BF16), w2_ref[...])

    kc = mlp(xk_ref, wklo_ref, wkhi_ref, wk2_ref, pk_ref)
    kc_ref[0] = _group_rms(kc, bd_ref[...], kg_ref[...]).astype(BF16)
    vct_ref[0] = _values_t_aug(mlp(xv_ref, wvlo_ref, wvhi_ref, wv2_ref, pv_ref))


def _compress(xk, xv, wk, wv, bd, kg):
    b, ns, wd = xk.shape
    const2 = lambda i: (0, 0)
    bat = lambda i: (i, 0, 0)
    wspecs = [pl.BlockSpec(w.shape, const2) for w in wk] + [pl.BlockSpec(w.shape, const2) for w in wv]
    return pl.pallas_call(
        _compress_kernel,
        grid=(b,),
        in_specs=[pl.BlockSpec((1, ns, wd), bat), pl.BlockSpec((1, ns, wd), bat)] + wspecs + [
            pl.BlockSpec((LANE, LANE), const2), pl.BlockSpec((1, LANE), const2)],
        out_specs=[pl.BlockSpec((1, ns, LANE), bat), pl.BlockSpec((1, 2 * V_AUG, ns), bat)],
        out_shape=[jax.ShapeDtypeStruct((b, ns, LANE), BF16), jax.ShapeDtypeStruct((b, 2 * V_AUG, ns), BF16)],
        compiler_params=pltpu.CompilerParams(
            dimension_semantics=("arbitrary",), vmem_limit_bytes=VMEM_LIMIT),
        name="compress",
    )(xk, xv, *wk, *wv, bd, kg)


def _topk_bias_t(imp_ts, s0):
    nj, nq = imp_ts[0].shape
    jidx = lax.broadcasted_iota(jnp.int32, (nj, nq), 0)
    tq = s0 + lax.broadcasted_iota(jnp.int32, (nj, nq), 1)
    jcur = lax.shift_right_logical(tq, 6)
    forced = (jidx == 0) | (jidx == jcur) | (jidx == jcur - 1)
    future = jidx > jcur
    vs = [jnp.where(forced, -2.0, jnp.where(future, -1.0, imp_t)) for imp_t in imp_ts]
    jf = jidx.astype(F32)
    for _ in range(N_SELECT - 3):
        for g in range(len(vs)):
            m = jnp.max(vs[g], axis=0, keepdims=True)
            jm = jnp.min(jnp.where(vs[g] == m, jf, float(nj)), axis=0, keepdims=True)
            vs[g] = jnp.where(jf == jm, -2.0, vs[g])
    return [jnp.where(future, NEG, jnp.where(v < -1.5, 0.0, NEG)) for v in vs]


def _nsa_kernel(qt_ref, qtn_ref, ksa_ref, vst_ref, kw_ref, vwt_ref, kc_ref, vct_ref, gt_ref, gtn_ref, ovt_ref, o_ref,
                w_ref, wn_ref, bias_ref, cmp_ref, sc_ref, sw_ref, ss_ref, mx_ref, e_ref, ew_ref,
                m_ref, acc_ref, out_ref):
    qb = pl.program_id(1)
    s0 = qb * Q_BLOCK
    nc = kc_ref.shape[1]
    two = 2 * Q_BLOCK
    lane_q = lax.broadcasted_iota(jnp.int32, (1, two), 1) & (Q_BLOCK - 1)
    t_lane = s0 + lane_q
    gt = gt_ref[0]

    def stationary_q(q_ref, dst):
        zero = jnp.zeros((HEAD_DIM, Q_BLOCK), BF16)
        for p in range(N_PAIRS):
            g = (2 * p) // NSA_REP
            halves = []
            for h in (2 * p, 2 * p + 1):
                qh = q_ref[0, h * HEAD_DIM:(h + 1) * HEAD_DIM, :]
                halves.append(jnp.concatenate([qh, zero] if g == 0 else [zero, qh], axis=0))
            dst[p, 0:LANE, :] = jnp.concatenate(halves, axis=1)

    def store_scores(dst, s, valid):
        if valid is not None:
            s = jnp.where(valid, s, NEG)
        dst[...] = s
        return jnp.max(s, axis=0, keepdims=True)

    def exp_tiles(load, n_tiles, m, e_dst):
        for c in range(n_tiles):
            rows = slice(c * KEY_TILE, (c + 1) * KEY_TILE)
            e_dst[rows, :] = jnp.exp2(load(rows) - m).astype(BF16)

    def values_t(ref, g, cols):
        return ref[0, g * V_AUG:(g + 1) * V_AUG, cols]

    def emit(dst, gates, p, branch, o_aug, first, guard=None):
        inv = 1.0 / o_aug[HEAD_DIM:HEAD_DIM + 1, :]
        if guard is not None:
            inv = jnp.where(guard, inv, 0.0)
        for hh in range(2):
            h = 2 * p + hh
            ls = slice(hh * Q_BLOCK, (hh + 1) * Q_BLOCK)
            o = o_aug[0:HEAD_DIM, ls] * (inv[:, ls] * gates[3 * h + branch:3 * h + branch + 1, :])
            rs = slice(h * HEAD_DIM, (h + 1) * HEAD_DIM)
            if first:
                dst[rs, :] = o
            else:
                dst[rs, :] += o
        return inv

    def key_rows(n):
        return lax.broadcasted_iota(jnp.int32, (n, two), 0)

    def cmp_scores(q_ref, s0x):
        stationary_q(q_ref, wn_ref)
        kc = kc_ref[0]
        valid = key_rows(nc) <= lax.shift_right_arithmetic(s0x + lane_q - (CMP_LEN - 1), 4)
        return [store_scores(sc_ref.at[p], _dot(kc, wn_ref[p]), valid) for p in range(N_PAIRS)]

    def cmp_finish(gates, m_cmp):
        imp_t = [jnp.zeros((LANE, Q_BLOCK), F32) for _ in range(NSA_GROUPS)]
        for p in range(N_PAIRS):
            exp_tiles(lambda rows: sc_ref[p, rows, :], nc // KEY_TILE, m_cmp[p], e_ref.at[p])
        lhs = [jnp.concatenate([values_t(vct_ref, g, slice(None)), ovt_ref[...]], axis=0) for g in range(NSA_GROUPS)]
        for p in range(N_PAIRS):
            g = (2 * p) // NSA_REP
            both = _dot(lhs[g], e_ref[p, 0:nc, :])
            inv = emit(cmp_ref, gates, p, 0, both[0:V_AUG], True, guard=m_cmp[p] > 0.5 * NEG)
            imp_p = both[V_AUG:V_AUG + LANE] * inv
            imp_t[g] = imp_t[g] + imp_p[:, 0:Q_BLOCK] + imp_p[:, Q_BLOCK:two]
        return imp_t

    def select(imp_t, s0x):
        for g, bias_t in enumerate(_topk_bias_t(imp_t, s0x)):
            bias_ref[g] = bias_t.astype(BF16)

    @pl.when(qb == 0)
    def _():
        select(cmp_finish(gt, cmp_scores(qt_ref, s0)), s0)

    w0 = pl.multiple_of(jnp.maximum(s0 - WINDOW, 0), KEY_TILE)
    wlen = WINDOW + Q_BLOCK
    kwin = kw_ref[0, pl.ds(w0, wlen), :]
    vwin = [values_t(vwt_ref, g, pl.ds(w0, wlen)) for g in range(NSA_GROUPS)]

    stationary_q(qt_ref, w_ref)
    for p in range(N_PAIRS):
        bias_t = bias_ref[(2 * p) // NSA_REP]
        w_ref[p, LANE:2 * LANE, :] = jnp.concatenate([bias_t, bias_t], axis=1)
    out_ref[...] = cmp_ref[...]

    imp_t = cmp_finish(gtn_ref[0], cmp_scores(qtn_ref, s0 + Q_BLOCK))
    diff = (t_lane - w0) - key_rows(wlen)
    win_valid = lax.shift_right_arithmetic(diff, WINDOW.bit_length() - 1) == 0
    m_win = [store_scores(sw_ref.at[p], _dot(kwin, w_ref[p, 0:LANE, :]), win_valid) for p in range(N_PAIRS)]
    for p in range(N_PAIRS):
        exp_tiles(lambda rows: sw_ref[p, rows, :], wlen // KEY_TILE, m_win[p], ew_ref.at[p])
    for p in range(N_PAIRS):
        emit(out_ref, gt, p, 2, _dot(vwin[(2 * p) // NSA_REP], ew_ref[p]), False)

    m_ref[...] = jnp.full(m_ref.shape, NEG, F32)
    acc_ref[...] = jnp.zeros(acc_ref.shape, F32)

    def sel_scores(u, buf, p, causal):
        k0 = pl.multiple_of(u * KEY_UNIT, KEY_UNIT)
        ku = ksa_ref[0, pl.ds(k0, KEY_UNIT), :]
        valid = (k0 + key_rows(KEY_UNIT)) <= t_lane if causal else None
        mx_ref[buf, p] = store_scores(ss_ref.at[buf, p], _dot(ku, w_ref[p]), valid)

    def sel_update(u, buf, p):
        k0 = pl.multiple_of(u * KEY_UNIT, KEY_UNIT)
        g = (2 * p) // NSA_REP
        m_old = m_ref[p]
        m = jnp.maximum(m_old, mx_ref[buf, p])
        exp_tiles(lambda rows: ss_ref[buf, p, rows, :], KEY_UNIT // KEY_TILE, m, e_ref.at[p])
        m_ref[p] = m
        acc_ref[p] = jnp.exp2(m_old - m) * acc_ref[p] + _dot(
            values_t(vst_ref, g, pl.ds(k0, KEY_UNIT)), e_ref[p, 0:KEY_UNIT, :])

    def sel_step(u, buf, next_causal):
        for p in range(N_PAIRS):
            sel_scores(u + 1, 1 - buf, p, next_causal)
            sel_update(u, buf, p)

    def sel_last(u, buf):
        for p in range(N_PAIRS):
            sel_update(u, buf, p)

    n_before = qb // (KEY_UNIT // Q_BLOCK)
    n_trips = jnp.maximum(n_before - 1, 0) // 2
    rest = 2 * n_trips

    for p in range(N_PAIRS):
        sel_scores(0, 0, p, False)
    select(imp_t, s0 + Q_BLOCK)

    def body(i, carry):
        sel_step(2 * i, 0, False)
        sel_step(2 * i + 1, 1, False)
        return carry

    def body2(i, carry):
        body(2 * i, carry)
        return body(2 * i + 1, carry)

    lax.fori_loop(0, n_trips // 2, body2, 0)
    lax.fori_loop(2 * (n_trips // 2), n_trips, body, 0)

    @pl.when(n_before == 0)
    def _():
        for p in range(N_PAIRS):
            sel_scores(0, 0, p, True)
        sel_last(0, 0)

    @pl.when(n_before == rest + 1)
    def _():
        sel_step(rest, 0, True)
        sel_last(rest + 1, 1)

    @pl.when(n_before == rest + 2)
    def _():
        sel_step(rest, 0, False)
        sel_step(rest + 1, 1, True)
        sel_last(rest + 2, 0)

    for p in range(N_PAIRS):
        emit(out_ref, gt, p, 1, acc_ref[p], False)

    o_ref[0] = out_ref[...].T.astype(BF16)


def _nsa(qt, ksa, vst, kw, vwt, kc, vct, gt, ovt):
    b, _, t = qt.shape
    nc = kc.shape[1]
    n_qb = t // Q_BLOCK
    qcol = lambda i, j: (i, 0, j)
    qnext = lambda i, j: (i, 0, jnp.minimum(j + 1, n_qb - 1))
    full = lambda i, j: (i, 0, 0)
    const = lambda i, j: (0, 0)
    return pl.pallas_call(
        _nsa_kernel,
        grid=(b, n_qb),
        in_specs=[
            pl.BlockSpec((1, NSA_WIDTH, Q_BLOCK), qcol),
            pl.BlockSpec((1, NSA_WIDTH, Q_BLOCK), qnext),
            pl.BlockSpec((1, t, 2 * LANE), full),
            pl.BlockSpec((1, 2 * V_AUG, t), full),
            pl.BlockSpec((1, t, LANE), full),
            pl.BlockSpec((1, 2 * V_AUG, t), full),
            pl.BlockSpec((1, nc, LANE), full),
            pl.BlockSpec((1, 2 * V_AUG, nc), full),
            pl.BlockSpec((1, LANE, Q_BLOCK), qcol),
            pl.BlockSpec((1, LANE, Q_BLOCK), qnext),
            pl.BlockSpec(ovt.shape, const),
        ],
        out_specs=pl.BlockSpec((1, Q_BLOCK, NSA_WIDTH), lambda i, j: (i, j, 0)),
        out_shape=jax.ShapeDtypeStruct((b, t, NSA_WIDTH), BF16),
        scratch_shapes=[
            pltpu.VMEM((N_PAIRS, 2 * LANE, 2 * Q_BLOCK), BF16),
            pltpu.VMEM((N_PAIRS, LANE, 2 * Q_BLOCK), BF16),
            pltpu.VMEM((NSA_GROUPS, LANE, Q_BLOCK), BF16),
            pltpu.VMEM((NSA_WIDTH, Q_BLOCK), F32),
            pltpu.VMEM((N_PAIRS, nc, 2 * Q_BLOCK), F32),
            pltpu.VMEM((N_PAIRS, WINDOW + Q_BLOCK, 2 * Q_BLOCK), F32),
            pltpu.VMEM((2, N_PAIRS, KEY_UNIT, 2 * Q_BLOCK), F32),
            pltpu.VMEM((2, N_PAIRS, 1, 2 * Q_BLOCK), F32),
            pltpu.VMEM((N_PAIRS, max(nc, KEY_UNIT), 2 * Q_BLOCK), BF16),
            pltpu.VMEM((N_PAIRS, WINDOW + Q_BLOCK, 2 * Q_BLOCK), BF16),
            pltpu.VMEM((N_PAIRS, 1, 2 * Q_BLOCK), F32),
            pltpu.VMEM((N_PAIRS, V_AUG, 2 * Q_BLOCK), F32),
            pltpu.VMEM((NSA_WIDTH, Q_BLOCK), F32),
        ],
        compiler_params=pltpu.CompilerParams(
            dimension_semantics=("arbitrary", "arbitrary"), vmem_limit_bytes=VMEM_LIMIT),
        name="nsa",
    )(qt, qt, ksa, vst, kw, vwt, kc, vct, gt, gt, ovt)


def _ff_chunks(d_ff, width):
    return tuple((s, min(width, d_ff - s)) for s in range(0, d_ff, width))


def _out_ffn_kernel(x_ref, on_ref, oh_ref, won_ref, woh_ref, g_ref, wgu_ref, wd_ref, o_ref, *, chunks):
    d_ff = wd_ref.shape[0]
    x1 = x_ref[...] + _dot(on_ref[...], won_ref[...]) + _dot(oh_ref[...], woh_ref[...])
    ms = jnp.mean(x1 * x1, axis=-1, keepdims=True)
    h = (x1 * lax.rsqrt(ms + RMS_EPS) * g_ref[...]).astype(BF16)
    o_ref[...] = x1
    for s, n in chunks:
        gate = _dot(h, wgu_ref[:, s:s + n])
        up = _dot(h, wgu_ref[:, d_ff + s:d_ff + s + n])
        act = (jax.nn.silu(gate) * up).astype(BF16)
        o_ref[...] += _dot(act, wd_ref[s:s + n, :])


def _out_ffn(x2, o_nsa, o_hg, won, woh, g, wgu, wd, tm=512, ff_width=4 * MXU_WIDTH):
    n, d = x2.shape
    d_ff = wd.shape[0]
    row = lambda i: (i, 0)
    const = lambda i: (0, 0)
    resident = lambda a: pl.BlockSpec(a.shape, const, pipeline_mode=pl.Buffered(1))
    return pl.pallas_call(
        functools.partial(_out_ffn_kernel, chunks=_ff_chunks(d_ff, ff_width)),
        grid=(n // tm,),
        in_specs=[
            pl.BlockSpec((tm, d), row),
            pl.BlockSpec((tm, NSA_WIDTH), row),
            pl.BlockSpec((tm, HG_WIDTH), row),
            resident(won),
            resident(woh),
            pl.BlockSpec((1, d), const),
            resident(wgu),
            resident(wd),
        ],
        out_specs=pl.BlockSpec((tm, d), row),
        out_shape=jax.ShapeDtypeStruct((n, d), F32),
        compiler_params=pltpu.CompilerParams(
            dimension_semantics=("arbitrary",), vmem_limit_bytes=VMEM_LIMIT),
        name="out_ffn",
    )(x2, o_nsa, o_hg, won, woh, g, wgu, wd)


def _expand_cmp_weights(pos, w1, w2):
    eye = jnp.eye(NSA_GROUPS, dtype=F32)
    w1r = w1.reshape(CMP_LEN, HEAD_DIM, CMP_HIDDEN)

    def lift(wpart):
        return jnp.einsum('ldc,gk->lgdkc', wpart, eye).reshape(
            CMP_STRIDE * NSA_GROUPS * HEAD_DIM, NSA_GROUPS * CMP_HIDDEN)

    wlo = lift(w1r[:CMP_STRIDE]).astype(BF16)
    whi = lift(w1r[CMP_STRIDE:]).astype(BF16)
    w2x = jnp.einsum('cd,gk->gckd', w2, eye).reshape(NSA_GROUPS * CMP_HIDDEN, NSA_GROUPS * HEAD_DIM).astype(BF16)
    pbias = jnp.tile(pos.reshape(1, CMP_LEN * HEAD_DIM) @ w1, (1, NSA_GROUPS))
    return wlo, whi, w2x, pbias


def _mixers(x, norm_mix, w_in, q_norm, k_norm, cmp_pos_k, cmp_pos_v, cmp_k_w1, cmp_k_w2, cmp_v_w1, cmp_v_w2,
            hgrn_lb_logits, hgrn_o_norm):
    b, t, d = x.shape
    depth = norm_mix.shape[0]
    assert depth == 1 and hgrn_lb_logits.shape[0] == 2
    assert t % KEY_UNIT == 0 and t >= WINDOW + Q_BLOCK and t // SLC_LEN <= LANE
    assert (t // CMP_STRIDE) % KEY_TILE == 0
    l = 0

    bd = jnp.asarray(np.kron(np.eye(LANE // HEAD_DIM), np.ones((HEAD_DIM, HEAD_DIM))), BF16)
    tile2 = lambda v: jnp.tile(v.reshape(1, HEAD_DIM), (1, LANE // HEAD_DIM)).astype(F32)
    qg = tile2(q_norm[l]) * (HEAD_DIM ** -0.5 * math.log2(math.e))

    qt, kc_raw, vc_raw, ksa, vst, kw, vwt, gt, o_hg = _in_proj(
        x, norm_mix[l].reshape(1, d), w_in[l], bd, qg, tile2(k_norm[l, 1]), tile2(k_norm[l, 2]),
        hgrn_lb_logits, hgrn_o_norm[l].reshape(1, HGRN_DIM))

    ns = t // CMP_STRIDE
    seg_w = CMP_STRIDE * KV_WIDTH
    kc, vct = _compress(
        kc_raw.reshape(b, ns, seg_w), vc_raw.reshape(b, ns, seg_w),
        _expand_cmp_weights(cmp_pos_k[l], cmp_k_w1[l], cmp_k_w2[l]),
        _expand_cmp_weights(cmp_pos_v[l], cmp_v_w1[l], cmp_v_w2[l]),
        bd, tile2(k_norm[l, 0]))

    cs = np.arange(ns)[None, :] * CMP_STRIDE
    ss = np.arange(LANE)[:, None] * SLC_LEN
    ovt = jnp.asarray(((cs < ss + SLC_LEN) & (cs + CMP_LEN > ss)).astype(np.float32), BF16)

    o_nsa = _nsa(qt, ksa, vst, kw, vwt, kc, vct, gt, ovt)
    return o_nsa, o_hg


def kernel(x, norm_mix, w_in, q_norm, k_norm, cmp_pos_k, cmp_pos_v, cmp_k_w1, cmp_k_w2, cmp_v_w1, cmp_v_w2,
           hgrn_lb_logits, hgrn_o_norm, w_out, norm_ffn, w_gate_up, w_down):
    b, t, d = x.shape
    l = 0
    o_nsa, o_hg = _mixers(x, norm_mix, w_in, q_norm, k_norm, cmp_pos_k, cmp_pos_v, cmp_k_w1, cmp_k_w2,
                          cmp_v_w1, cmp_v_w2, hgrn_lb_logits, hgrn_o_norm)
    wo = w_out[l]
    out = _out_ffn(
        x.reshape(b * t, d), o_nsa.reshape(b * t, NSA_WIDTH), o_hg.reshape(b * t, HG_WIDTH),
        wo[:NSA_WIDTH].astype(BF16), wo[NSA_WIDTH:].astype(BF16),
        norm_ffn[l].reshape(1, d), w_gate_up[l].astype(BF16), w_down[l].astype(BF16))
    return out.reshape(b, t, d)
```

```python
import functools
import math

import numpy as np
import jax
import jax.numpy as jnp
from jax import lax
from jax.experimental import pallas as pl
from jax.experimental.pallas import tpu as pltpu

F32 = jnp.float32
BF16 = jnp.bfloat16

LANE = 128
MXU_WIDTH = 256

NSA_HEADS = 8
NSA_GROUPS = 2
NSA_REP = NSA_HEADS // NSA_GROUPS
HEAD_DIM = 64
CMP_LEN = 32
CMP_STRIDE = 16
CMP_HIDDEN = 128
SLC_LEN = 64
N_SELECT = 16
WINDOW = 512
Q_BLOCK = 128
HGRN_HEADS = 4
HGRN_DIM = 128
HGRN_CHUNK = 64
HGRN_SUB = 16
RMS_EPS = 1e-6
NEG = -1e30

NSA_WIDTH = NSA_HEADS * HEAD_DIM
KV_WIDTH = NSA_GROUPS * HEAD_DIM
HG_WIDTH = HGRN_HEADS * HGRN_DIM
N_GATES = 3 * NSA_HEADS

KEY_TILE = 128
KEY_UNIT = 4 * KEY_TILE
N_PAIRS = NSA_HEADS // 2
V_AUG = HEAD_DIM + 16
VMEM_LIMIT = 56 * 1024 * 1024


def _dot(a, b):
    return jnp.dot(a, b, preferred_element_type=F32)


def _dot_t(a, b):
    return lax.dot_general(a, b, (((1,), (1,)), ((), ())), preferred_element_type=F32)


def _group_rms(y, bd, gain):
    ss = _dot((y * y).astype(BF16), bd)
    return y * lax.rsqrt(ss * (1.0 / HEAD_DIM) + RMS_EPS) * gain


def _values_t_aug(v):
    vt = v.T
    ones = jnp.ones((V_AUG - HEAD_DIM, v.shape[0]), F32)
    return jnp.concatenate([vt[0:HEAD_DIM], ones, vt[HEAD_DIM:2 * HEAD_DIM], ones], axis=0).astype(BF16)


def _hgrn_stages(ph_ref, lb_ref, og_ref, o_ref, st_ref, chunks):
    c_len = HGRN_CHUNK
    n_sub = c_len // HGRN_SUB
    lg = lb_ref[...]
    lmax = jnp.max(lg, axis=0, keepdims=True)
    le = jnp.exp(lg - lmax)
    lb_all = le[0:1] / jnp.sum(le, axis=0, keepdims=True)

    ri = lax.broadcasted_iota(jnp.int32, (c_len, c_len), 0)
    ci = lax.broadcasted_iota(jnp.int32, (c_len, c_len), 1)
    tri = (ci <= ri).astype(BF16)
    causal = ci <= ri
    gain = og_ref[...]
    heads = [slice(h * HGRN_DIM, (h + 1) * HGRN_DIM) for h in range(HGRN_HEADS)]
    rows = [slice(c * c_len, (c + 1) * c_len) for c in range(chunks)]

    pre = []
    for rs in rows:
        qv = jax.nn.silu(ph_ref[rs, 0:HG_WIDTH])
        f = lb_all + (1.0 - lb_all) * jax.nn.sigmoid(ph_ref[rs, HG_WIDTH:2 * HG_WIDTH])
        logf = jnp.log(f)
        g1 = logf.astype(BF16)
        g2 = (logf - g1.astype(F32)).astype(BF16)
        pre.append((qv, 1.0 - f, g1, g2))
    his = [ph_ref[rs, 2 * HG_WIDTH:3 * HG_WIDTH] for rs in rows]
    gates = [jax.nn.silu(ph_ref[rs, 3 * HG_WIDTH:4 * HG_WIDTH]) for rs in rows]
    yield

    bcums = [_dot(tri, g1) + _dot(tri, g2) for _, _, g1, g2 in pre]
    yield

    mid = []
    for c in range(chunks):
        qv, kv = pre[c][0], pre[c][1]
        bcum = bcums[c]
        b_last = bcum[c_len - 1:c_len]
        starts = [jnp.zeros((1, HG_WIDTH), F32)] + [bcum[i * HGRN_SUB - 1:i * HGRN_SUB] for i in range(1, n_sub)]
        ends = starts[1:] + [b_last]

        def per_sub(vals):
            return jnp.concatenate([jnp.broadcast_to(r, (HGRN_SUB, HG_WIDTH)) for r in vals], axis=0)

        start_full = per_sub(starts)
        qd = qv * jnp.exp(bcum - start_full)
        kb = kv * jnp.exp(per_sub(ends) - bcum)
        kdiag = kv * jnp.exp(start_full - bcum)
        qs = (qd * per_sub([jnp.exp(s) for s in starts])).astype(BF16)
        kdec = (kb * per_sub([jnp.exp(b_last - e) for e in ends])).astype(BF16)
        kds = []
        for i in range(n_sub):
            blocks = [kb[j * HGRN_SUB:(j + 1) * HGRN_SUB] * jnp.exp(starts[i] - ends[j]) for j in range(i - 1)]
            if i > 0:
                blocks.append(kb[(i - 1) * HGRN_SUB:i * HGRN_SUB])
            blocks.append(kdiag[i * HGRN_SUB:(i + 1) * HGRN_SUB])
            if i + 1 < n_sub:
                blocks.append(jnp.zeros(((n_sub - 1 - i) * HGRN_SUB, HG_WIDTH), F32))
            kds.append(jnp.concatenate(blocks, axis=0).astype(BF16))
        mid.append((qd.astype(BF16), kds, qs, kdec, jnp.exp(b_last)))
    attn = {}
    upd = {}
    for c in range(chunks):
        qd, kds, qs, kdec, dec = mid[c]
        for h, hs in enumerate(heads):
            both = _dot_t(qd[:, hs], jnp.concatenate([kd[:, hs] for kd in kds], axis=0))
            own = jnp.concatenate([both[i * HGRN_SUB:(i + 1) * HGRN_SUB, i * c_len:(i + 1) * c_len]
                                   for i in range(n_sub)], axis=0)
            attn[c, h] = jnp.where(causal, own, 0.0).astype(BF16)
            upd[c, h] = _dot(his[c][:, hs].T.astype(BF16), kdec[:, hs])
    yield

    state = {}
    for h, hs in enumerate(heads):
        st = st_ref[h]
        for c in range(chunks):
            state[c, h] = st.astype(BF16)
            st = st * mid[c][4][:, hs] + upd[c, h]
        st_ref[h] = st
    for c, rs in enumerate(rows):
        qs = mid[c][2]
        for h, hs in enumerate(heads):
            o = _dot(attn[c, h], his[c][:, hs].astype(BF16)) + _dot_t(qs[:, hs], state[c, h])
            ms = jnp.mean(o * o, axis=-1, keepdims=True)
            o = o * lax.rsqrt(ms + RMS_EPS) * gain * gates[c][:, hs]
            o_ref[0, rs, hs] = o.astype(BF16)
    yield


def _in_proj_kernel(x_ref, g_ref, w_ref, bd_ref, qg_ref, ksg_ref, kwg_ref, lb_ref, og_ref,
                    qt_ref, kc_ref, vc_ref, ksa_ref, vst_ref, kw_ref, vwt_ref, gt_ref, ohg_ref,
                    wp_ref, ph_ref, st_ref):
    tm = x_ref.shape[1]
    d = x_ref.shape[2]
    j = pl.program_id(1)
    tile = jnp.minimum(j, pl.num_programs(1) - 2)
    o_g = NSA_WIDTH + 6 * KV_WIDTH

    @pl.when((pl.program_id(0) == 0) & (j == 0))
    def _():
        o_h = o_g + N_GATES
        rows_per = 128

        def copy_rows(i, carry):
            rs = pl.ds(pl.multiple_of(i * rows_per, rows_per), rows_per)
            wp_ref[rs, 0:o_g] = w_ref[rs, 0:o_g].astype(BF16)
            wp_ref[rs, o_g:o_g + 4 * HG_WIDTH] = w_ref[rs, o_h:o_h + 4 * HG_WIDTH].astype(BF16)
            tail = jnp.concatenate(
                [w_ref[rs, o_g:o_h], jnp.zeros((rows_per, LANE - N_GATES), F32)], axis=1)
            wp_ref[rs, o_g + 4 * HG_WIDTH:o_g + 4 * HG_WIDTH + LANE] = tail.astype(BF16)
            return carry

        lax.fori_loop(0, d // rows_per, copy_rows, 0)
        ph_ref[...] = jnp.zeros(ph_ref.shape, F32)
        st_ref[...] = jnp.zeros(st_ref.shape, F32)

    x = x_ref[0]
    ms = jnp.mean(x * x, axis=-1, keepdims=True)
    h = (x * lax.rsqrt(ms + RMS_EPS) * g_ref[...]).astype(BF16)

    mixer = _hgrn_stages(ph_ref, lb_ref, og_ref, ohg_ref, st_ref, tm // HGRN_CHUNK)
    next(mixer)
    y_nsa = _dot(h, wp_ref[:, 0:o_g])
    next(mixer)

    def hgrn_cols(k):
        return _dot(h, wp_ref[:, o_g + k * HG_WIDTH:o_g + (k + 1) * HG_WIDTH])

    y_h = [hgrn_cols(0), hgrn_cols(1)]
    next(mixer)

    bd = bd_ref[...]
    for r in range(NSA_WIDTH // LANE):
        sl = slice(r * LANE, (r + 1) * LANE)
        qt_ref[0, sl, :] = _group_rms(y_nsa[:, sl], bd, qg_ref[...]).T.astype(BF16)
    o = NSA_WIDTH
    kc_ref[0] = y_nsa[:, o:o + LANE].astype(BF16)
    vc_ref[0] = y_nsa[:, o + LANE:o + 2 * LANE].astype(BF16)
    ksa_ref[0, :, 0:LANE] = _group_rms(y_nsa[:, o + 2 * LANE:o + 3 * LANE], bd, ksg_ref[...]).astype(BF16)
    key = tile * tm + lax.broadcasted_iota(jnp.int32, (tm, LANE), 0)
    blk = lax.broadcasted_iota(jnp.int32, (tm, LANE), 1)
    ksa_ref[0, :, LANE:2 * LANE] = jnp.where(lax.shift_right_logical(key, 6) == blk, 1.0, 0.0).astype(BF16)
    vst_ref[0] = _values_t_aug(y_nsa[:, o + 3 * LANE:o + 4 * LANE])
    kw_ref[0] = _group_rms(y_nsa[:, o + 4 * LANE:o + 5 * LANE], bd, kwg_ref[...]).astype(BF16)
    vwt_ref[0] = _values_t_aug(y_nsa[:, o + 5 * LANE:o + 6 * LANE])
    y_gate = _dot(h, wp_ref[:, o_g + 4 * HG_WIDTH:o_g + 4 * HG_WIDTH + LANE])
    gt_ref[0] = jax.nn.sigmoid(y_gate).T
    y_h.append(hgrn_cols(2))
    next(mixer)
    y_h.append(hgrn_cols(3))
    for k in range(4):
        ph_ref[:, k * HG_WIDTH:(k + 1) * HG_WIDTH] = y_h[k]

    @pl.when(j == 0)
    def _():
        st_ref[...] = jnp.zeros(st_ref.shape, F32)


def _in_proj(x, g, w, bd, qg, ksg, kwg, lb_logits, o_gain, tm=256):
    b, t, d = x.shape
    n_t = t // tm
    nw = NSA_WIDTH + 6 * KV_WIDTH + 4 * HG_WIDTH + LANE
    rows = lambda i, j: (i, jnp.minimum(j, n_t - 1), 0)
    cols = lambda i, j: (i, 0, jnp.minimum(j, n_t - 1))
    prev = lambda i, j: (i, jnp.maximum(j - 1, 0), 0)
    const = lambda i, j: (0, 0)
    outs = [
        (jax.ShapeDtypeStruct((b, NSA_WIDTH, t), BF16), pl.BlockSpec((1, NSA_WIDTH, tm), cols)),
        (jax.ShapeDtypeStruct((b, t, LANE), BF16), pl.BlockSpec((1, tm, LANE), rows)),
        (jax.ShapeDtypeStruct((b, t, LANE), BF16), pl.BlockSpec((1, tm, LANE), rows)),
        (jax.ShapeDtypeStruct((b, t, 2 * LANE), BF16), pl.BlockSpec((1, tm, 2 * LANE), rows)),
        (jax.ShapeDtypeStruct((b, 2 * V_AUG, t), BF16), pl.BlockSpec((1, 2 * V_AUG, tm), cols)),
        (jax.ShapeDtypeStruct((b, t, LANE), BF16), pl.BlockSpec((1, tm, LANE), rows)),
        (jax.ShapeDtypeStruct((b, 2 * V_AUG, t), BF16), pl.BlockSpec((1, 2 * V_AUG, tm), cols)),
        (jax.ShapeDtypeStruct((b, LANE, t), F32), pl.BlockSpec((1, LANE, tm), cols)),
        (jax.ShapeDtypeStruct((b, t, HG_WIDTH), BF16), pl.BlockSpec((1, tm, HG_WIDTH), prev)),
    ]
    return pl.pallas_call(
        _in_proj_kernel,
        grid=(b, n_t + 1),
        in_specs=[
            pl.BlockSpec((1, tm, d), rows),
            pl.BlockSpec((1, d), const),
            pl.BlockSpec(w.shape, const, pipeline_mode=pl.Buffered(1)),
            pl.BlockSpec((LANE, LANE), const),
            pl.BlockSpec((1, LANE), const),
            pl.BlockSpec((1, LANE), const),
            pl.BlockSpec((1, LANE), const),
            pl.BlockSpec(lb_logits.shape, const),
            pl.BlockSpec((1, HGRN_DIM), const),
        ],
        out_specs=[s for _, s in outs],
        out_shape=[s for s, _ in outs],
        scratch_shapes=[
            pltpu.VMEM((d, nw), BF16),
            pltpu.VMEM((tm, 4 * HG_WIDTH), F32),
            pltpu.VMEM((HGRN_HEADS, HGRN_DIM, HGRN_DIM), F32),
        ],
        compiler_params=pltpu.CompilerParams(
            dimension_semantics=("arbitrary", "arbitrary"), vmem_limit_bytes=VMEM_LIMIT),
        name="in_proj",
    )(x, g, w, bd, qg, ksg, kwg, lb_logits, o_gain)


def _compress_kernel(xk_ref, xv_ref, wklo_ref, wkhi_ref, wk2_ref, pk_ref,
                     wvlo_ref, wvhi_ref, wv2_ref, pv_ref, bd_ref, kg_ref, kc_ref, vct_ref):
    ns = xk_ref.shape[1]

    def mlp(x_ref, wlo_ref, whi_ref, w2_ref, p_ref):
        xb = x_ref[0]
        a = _dot(xb, wlo_ref[...])
        b = _dot(xb, whi_ref[...])
        h = a + pltpu.roll(b, ns - 1, axis=0) + p_ref[...]
        return _dot(jax.nn.gelu(h).astype(BF16), w2_ref[...])

    kc = mlp(xk_ref, wklo_ref, wkhi_ref, wk2_ref, pk_ref)
    kc_ref[0] = _group_rms(kc, bd_ref[...], kg_ref[...]).astype(BF16)
    vct_ref[0] = _values_t_aug(mlp(xv_ref, wvlo_ref, wvhi_ref, wv2_ref, pv_ref))


def _compress(xk, xv, wk, wv, bd, kg):
    b, ns, wd = xk.shape
    const2 = lambda i: (0, 0)
    bat = lambda i: (i, 0, 0)
    wspecs = [pl.BlockSpec(w.shape, const2) for w in wk] + [pl.BlockSpec(w.shape, const2) for w in wv]
    return pl.pallas_call(
        _compress_kernel,
        grid=(b,),
        in_specs=[pl.BlockSpec((1, ns, wd), bat), pl.BlockSpec((1, ns, wd), bat)] + wspecs + [
            pl.BlockSpec((LANE, LANE), const2), pl.BlockSpec((1, LANE), const2)],
        out_specs=[pl.BlockSpec((1, ns, LANE), bat), pl.BlockSpec((1, 2 * V_AUG, ns), bat)],
        out_shape=[jax.ShapeDtypeStruct((b, ns, LANE), BF16), jax.ShapeDtypeStruct((b, 2 * V_AUG, ns), BF16)],
        compiler_params=pltpu.CompilerParams(
            dimension_semantics=("arbitrary",), vmem_limit_bytes=VMEM_LIMIT),
        name="compress",
    )(xk, xv, *wk, *wv, bd, kg)


def _topk_bias_t(imp_ts, s0):
    nj, nq = imp_ts[0].shape
    jidx = lax.broadcasted_iota(jnp.int32, (nj, nq), 0)
    tq = s0 + lax.broadcasted_iota(jnp.int32, (nj, nq), 1)
    jcur = lax.shift_right_logical(tq, 6)
    forced = (jidx == 0) | (jidx == jcur) | (jidx == jcur - 1)
    future = jidx > jcur
    vs = [jnp.where(forced, -2.0, jnp.where(future, -1.0, imp_t)) for imp_t in imp_ts]
    jf = jidx.astype(F32)
    for _ in range(N_SELECT - 3):
        for g in range(len(vs)):
            m = jnp.max(vs[g], axis=0, keepdims=True)
            jm = jnp.min(jnp.where(vs[g] == m, jf, float(nj)), axis=0, keepdims=True)
            vs[g] = jnp.where(jf == jm, -2.0, vs[g])
    return [jnp.where(future, NEG, jnp.where(v < -1.5, 0.0, NEG)) for v in vs]


def _nsa_kernel(qt_ref, qtn_ref, ksa_ref, vst_ref, kw_ref, vwt_ref, kc_ref, vct_ref, gt_ref, gtn_ref, ovt_ref, o_ref,
                w_ref, wn_ref, bias_ref, cmp_ref, sc_ref, sw_ref, ss_ref, mx_ref, e_ref, ew_ref,
                m_ref, acc_ref, out_ref):
    qb = pl.program_id(1)
    s0 = qb * Q_BLOCK
    nc = kc_ref.shape[1]
    two = 2 * Q_BLOCK
    lane_q = lax.broadcasted_iota(jnp.int32, (1, two), 1) & (Q_BLOCK - 1)
    t_lane = s0 + lane_q
    gt = gt_ref[0]

    def stationary_q(q_ref, dst):
        zero = jnp.zeros((HEAD_DIM, Q_BLOCK), BF16)
        for p in range(N_PAIRS):
            g = (2 * p) // NSA_REP
            halves = []
            for h in (2 * p, 2 * p + 1):
                qh = q_ref[0, h * HEAD_DIM:(h + 1) * HEAD_DIM, :]
                halves.append(jnp.concatenate([qh, zero] if g == 0 else [zero, qh], axis=0))
            dst[p, 0:LANE, :] = jnp.concatenate(halves, axis=1)

    def store_scores(dst, s, valid):
        if valid is not None:
            s = jnp.where(valid, s, NEG)
        dst[...] = s
        return jnp.max(s, axis=0, keepdims=True)

    def exp_tiles(load, n_tiles, m, e_dst):
        for c in range(n_tiles):
            rows = slice(c * KEY_TILE, (c + 1) * KEY_TILE)
            e_dst[rows, :] = jnp.exp2(load(rows) - m).astype(BF16)

    def values_t(ref, g, cols):
        return ref[0, g * V_AUG:(g + 1) * V_AUG, cols]

    def emit(dst, gates, p, branch, o_aug, first, guard=None):
        inv = 1.0 / o_aug[HEAD_DIM:HEAD_DIM + 1, :]
        if guard is not None:
            inv = jnp.where(guard, inv, 0.0)
        for hh in range(2):
            h = 2 * p + hh
            ls = slice(hh * Q_BLOCK, (hh + 1) * Q_BLOCK)
            o = o_aug[0:HEAD_DIM, ls] * (inv[:, ls] * gates[3 * h + branch:3 * h + branch + 1, :])
            rs = slice(h * HEAD_DIM, (h + 1) * HEAD_DIM)
            if first:
                dst[rs, :] = o
            else:
                dst[rs, :] += o
        return inv

    def key_rows(n):
        return lax.broadcasted_iota(jnp.int32, (n, two), 0)

    def cmp_scores(q_ref, s0x):
        stationary_q(q_ref, wn_ref)
        kc = kc_ref[0]
        valid = key_rows(nc) <= lax.shift_right_arithmetic(s0x + lane_q - (CMP_LEN - 1), 4)
        return [store_scores(sc_ref.at[p], _dot(kc, wn_ref[p]), valid) for p in range(N_PAIRS)]

    def cmp_finish(gates, m_cmp):
        imp_t = [jnp.zeros((LANE, Q_BLOCK), F32) for _ in range(NSA_GROUPS)]
        for p in range(N_PAIRS):
            exp_tiles(lambda rows: sc_ref[p, rows, :], nc // KEY_TILE, m_cmp[p], e_ref.at[p])
        for p in range(N_PAIRS):
            g = (2 * p) // NSA_REP
            inv = emit(cmp_ref, gates, p, 0, _dot(values_t(vct_ref, g, slice(None)), e_ref[p, 0:nc, :]), True,
                       guard=m_cmp[p] > 0.5 * NEG)
            imp_p = _dot(ovt_ref[...], e_ref[p, 0:nc, :]) * inv
            imp_t[g] = imp_t[g] + imp_p[:, 0:Q_BLOCK] + imp_p[:, Q_BLOCK:two]
        return imp_t

    def select(imp_t, s0x):
        for g, bias_t in enumerate(_topk_bias_t(imp_t, s0x)):
            bias_ref[g] = bias_t.astype(BF16)

    @pl.when(qb == 0)
    def _():
        select(cmp_finish(gt, cmp_scores(qt_ref, s0)), s0)

    w0 = pl.multiple_of(jnp.maximum(s0 - WINDOW, 0), KEY_TILE)
    wlen = WINDOW + Q_BLOCK
    kwin = kw_ref[0, pl.ds(w0, wlen), :]
    vwin = [values_t(vwt_ref, g, pl.ds(w0, wlen)) for g in range(NSA_GROUPS)]

    stationary_q(qt_ref, w_ref)
    for p in range(N_PAIRS):
        bias_t = bias_ref[(2 * p) // NSA_REP]
        w_ref[p, LANE:2 * LANE, :] = jnp.concatenate([bias_t, bias_t], axis=1)
    out_ref[...] = cmp_ref[...]

    imp_t = cmp_finish(gtn_ref[0], cmp_scores(qtn_ref, s0 + Q_BLOCK))
    diff = (t_lane - w0) - key_rows(wlen)
    win_valid = lax.shift_right_arithmetic(diff, WINDOW.bit_length() - 1) == 0
    m_win = [store_scores(sw_ref.at[p], _dot(kwin, w_ref[p, 0:LANE, :]), win_valid) for p in range(N_PAIRS)]
    for p in range(N_PAIRS):
        exp_tiles(lambda rows: sw_ref[p, rows, :], wlen // KEY_TILE, m_win[p], ew_ref.at[p])
    for p in range(N_PAIRS):
        emit(out_ref, gt, p, 2, _dot(vwin[(2 * p) // NSA_REP], ew_ref[p]), False)

    m_ref[...] = jnp.full(m_ref.shape, NEG, F32)
    acc_ref[...] = jnp.zeros(acc_ref.shape, F32)

    def sel_scores(u, buf, p, causal):
        k0 = pl.multiple_of(u * KEY_UNIT, KEY_UNIT)
        ku = ksa_ref[0, pl.ds(k0, KEY_UNIT), :]
        valid = (k0 + key_rows(KEY_UNIT)) <= t_lane if causal else None
        mx_ref[buf, p] = store_scores(ss_ref.at[buf, p], _dot(ku, w_ref[p]), valid)

    def sel_update(u, buf, p):
        k0 = pl.multiple_of(u * KEY_UNIT, KEY_UNIT)
        g = (2 * p) // NSA_REP
        m_old = m_ref[p]
        m = jnp.maximum(m_old, mx_ref[buf, p])
        exp_tiles(lambda rows: ss_ref[buf, p, rows, :], KEY_UNIT // KEY_TILE, m, e_ref.at[p])
        m_ref[p] = m
        acc_ref[p] = jnp.exp2(m_old - m) * acc_ref[p] + _dot(
            values_t(vst_ref, g, pl.ds(k0, KEY_UNIT)), e_ref[p, 0:KEY_UNIT, :])

    def sel_step(u, buf, next_causal):
        for p in range(N_PAIRS):
            sel_scores(u + 1, 1 - buf, p, next_causal)
            sel_update(u, buf, p)

    def sel_last(u, buf):
        for p in range(N_PAIRS):
            sel_update(u, buf, p)

    n_before = qb // (KEY_UNIT // Q_BLOCK)
    n_trips = jnp.maximum(n_before - 1, 0) // 2
    rest = 2 * n_trips

    for p in range(N_PAIRS):
        sel_scores(0, 0, p, False)
    select(imp_t, s0 + Q_BLOCK)

    def body(i, carry):
        sel_step(2 * i, 0, False)
        sel_step(2 * i + 1, 1, False)
        return carry

    def body2(i, carry):
        body(2 * i, carry)
        return body(2 * i + 1, carry)

    lax.fori_loop(0, n_trips // 2, body2, 0)
    lax.fori_loop(2 * (n_trips // 2), n_trips, body, 0)

    @pl.when(n_before == 0)
    def _():
        for p in range(N_PAIRS):
            sel_scores(0, 0, p, True)
        sel_last(0, 0)

    @pl.when(n_before == rest + 1)
    def _():
        sel_step(rest, 0, True)
        sel_last(rest + 1, 1)

    @pl.when(n_before == rest + 2)
    def _():
        sel_step(rest, 0, False)
        sel_step(rest + 1, 1, True)
        sel_last(rest + 2, 0)

    for p in range(N_PAIRS):
        emit(out_ref, gt, p, 1, acc_ref[p], False)

    o_ref[0] = out_ref[...].T.astype(BF16)


def _nsa(qt, ksa, vst, kw, vwt, kc, vct, gt, ovt):
    b, _, t = qt.shape
    nc = kc.shape[1]
    n_qb = t // Q_BLOCK
    qcol = lambda i, j: (i, 0, j)
    qnext = lambda i, j: (i, 0, jnp.minimum(j + 1, n_qb - 1))
    full = lambda i, j: (i, 0, 0)
    const = lambda i, j: (0, 0)
    return pl.pallas_call(
        _nsa_kernel,
        grid=(b, n_qb),
        in_specs=[
            pl.BlockSpec((1, NSA_WIDTH, Q_BLOCK), qcol),
            pl.BlockSpec((1, NSA_WIDTH, Q_BLOCK), qnext),
            pl.BlockSpec((1, t, 2 * LANE), full),
            pl.BlockSpec((1, 2 * V_AUG, t), full),
            pl.BlockSpec((1, t, LANE), full),
            pl.BlockSpec((1, 2 * V_AUG, t), full),
            pl.BlockSpec((1, nc, LANE), full),
            pl.BlockSpec((1, 2 * V_AUG, nc), full),
            pl.BlockSpec((1, LANE, Q_BLOCK), qcol),
            pl.BlockSpec((1, LANE, Q_BLOCK), qnext),
            pl.BlockSpec(ovt.shape, const),
        ],
        out_specs=pl.BlockSpec((1, Q_BLOCK, NSA_WIDTH), lambda i, j: (i, j, 0)),
        out_shape=jax.ShapeDtypeStruct((b, t, NSA_WIDTH), BF16),
        scratch_shapes=[
            pltpu.VMEM((N_PAIRS, 2 * LANE, 2 * Q_BLOCK), BF16),
            pltpu.VMEM((N_PAIRS, LANE, 2 * Q_BLOCK), BF16),
            pltpu.VMEM((NSA_GROUPS, LANE, Q_BLOCK), BF16),
            pltpu.VMEM((NSA_WIDTH, Q_BLOCK), F32),
            pltpu.VMEM((N_PAIRS, nc, 2 * Q_BLOCK), F32),
            pltpu.VMEM((N_PAIRS, WINDOW + Q_BLOCK, 2 * Q_BLOCK), F32),
            pltpu.VMEM((2, N_PAIRS, KEY_UNIT, 2 * Q_BLOCK), F32),
            pltpu.VMEM((2, N_PAIRS, 1, 2 * Q_BLOCK), F32),
            pltpu.VMEM((N_PAIRS, max(nc, KEY_UNIT), 2 * Q_BLOCK), BF16),
            pltpu.VMEM((N_PAIRS, WINDOW + Q_BLOCK, 2 * Q_BLOCK), BF16),
            pltpu.VMEM((N_PAIRS, 1, 2 * Q_BLOCK), F32),
            pltpu.VMEM((N_PAIRS, V_AUG, 2 * Q_BLOCK), F32),
            pltpu.VMEM((NSA_WIDTH, Q_BLOCK), F32),
        ],
        compiler_params=pltpu.CompilerParams(
            dimension_semantics=("arbitrary", "arbitrary"), vmem_limit_bytes=VMEM_LIMIT),
        name="nsa",
    )(qt, qt, ksa, vst, kw, vwt, kc, vct, gt, gt, ovt)


def _ff_chunks(d_ff, width):
    return tuple((s, min(width, d_ff - s)) for s in range(0, d_ff, width))


def _out_ffn_kernel(x_ref, on_ref, oh_ref, won_ref, woh_ref, g_ref, wgu_ref, wd_ref, o_ref, *, chunks):
    d_ff = wd_ref.shape[0]
    x1 = x_ref[...] + _dot(on_ref[...], won_ref[...]) + _dot(oh_ref[...], woh_ref[...])
    ms = jnp.mean(x1 * x1, axis=-1, keepdims=True)
    h = (x1 * lax.rsqrt(ms + RMS_EPS) * g_ref[...]).astype(BF16)
    o_ref[...] = x1
    for s, n in chunks:
        gate = _dot(h, wgu_ref[:, s:s + n])
        up = _dot(h, wgu_ref[:, d_ff + s:d_ff + s + n])
        act = (jax.nn.silu(gate) * up).astype(BF16)
        o_ref[...] += _dot(act, wd_ref[s:s + n, :])


def _out_ffn(x2, o_nsa, o_hg, won, woh, g, wgu, wd, tm=512, ff_width=4 * MXU_WIDTH):
    n, d = x2.shape
    d_ff = wd.shape[0]
    row = lambda i: (i, 0)
    const = lambda i: (0, 0)
    resident = lambda a: pl.BlockSpec(a.shape, const, pipeline_mode=pl.Buffered(1))
    return pl.pallas_call(
        functools.partial(_out_ffn_kernel, chunks=_ff_chunks(d_ff, ff_width)),
        grid=(n // tm,),
        in_specs=[
            pl.BlockSpec((tm, d), row),
            pl.BlockSpec((tm, NSA_WIDTH), row),
            pl.BlockSpec((tm, HG_WIDTH), row),
            resident(won),
            resident(woh),
            pl.BlockSpec((1, d), const),
            resident(wgu),
            resident(wd),
        ],
        out_specs=pl.BlockSpec((tm, d), row),
        out_shape=jax.ShapeDtypeStruct((n, d), F32),
        compiler_params=pltpu.CompilerParams(
            dimension_semantics=("arbitrary",), vmem_limit_bytes=VMEM_LIMIT),
        name="out_ffn",
    )(x2, o_nsa, o_hg, won, woh, g, wgu, wd)


def _expand_cmp_weights(pos, w1, w2):
    eye = jnp.eye(NSA_GROUPS, dtype=F32)
    w1r = w1.reshape(CMP_LEN, HEAD_DIM, CMP_HIDDEN)

    def lift(wpart):
        return jnp.einsum('ldc,gk->lgdkc', wpart, eye).reshape(
            CMP_STRIDE * NSA_GROUPS * HEAD_DIM, NSA_GROUPS * CMP_HIDDEN)

    wlo = lift(w1r[:CMP_STRIDE]).astype(BF16)
    whi = lift(w1r[CMP_STRIDE:]).astype(BF16)
    w2x = jnp.einsum('cd,gk->gckd', w2, eye).reshape(NSA_GROUPS * CMP_HIDDEN, NSA_GROUPS * HEAD_DIM).astype(BF16)
    pbias = jnp.tile(pos.reshape(1, CMP_LEN * HEAD_DIM) @ w1, (1, NSA_GROUPS))
    return wlo, whi, w2x, pbias


def _mixers(x, norm_mix, w_in, q_norm, k_norm, cmp_pos_k, cmp_pos_v, cmp_k_w1, cmp_k_w2, cmp_v_w1, cmp_v_w2,
            hgrn_lb_logits, hgrn_o_norm):
    b, t, d = x.shape
    depth = norm_mix.shape[0]
    assert depth == 1 and hgrn_lb_logits.shape[0] == 2
    assert t % KEY_UNIT == 0 and t >= WINDOW + Q_BLOCK and t // SLC_LEN <= LANE
    assert (t // CMP_STRIDE) % KEY_TILE == 0
    l = 0

    bd = jnp.asarray(np.kron(np.eye(LANE // HEAD_DIM), np.ones((HEAD_DIM, HEAD_DIM))), BF16)
    tile2 = lambda v: jnp.tile(v.reshape(1, HEAD_DIM), (1, LANE // HEAD_DIM)).astype(F32)
    qg = tile2(q_norm[l]) * (HEAD_DIM ** -0.5 * math.log2(math.e))

    qt, kc_raw, vc_raw, ksa, vst, kw, vwt, gt, o_hg = _in_proj(
        x, norm_mix[l].reshape(1, d), w_in[l], bd, qg, tile2(k_norm[l, 1]), tile2(k_norm[l, 2]),
        hgrn_lb_logits, hgrn_o_norm[l].reshape(1, HGRN_DIM))

    ns = t // CMP_STRIDE
    seg_w = CMP_STRIDE * KV_WIDTH
    kc, vct = _compress(
        kc_raw.reshape(b, ns, seg_w), vc_raw.reshape(b, ns, seg_w),
        _expand_cmp_weights(cmp_pos_k[l], cmp_k_w1[l], cmp_k_w2[l]),
        _expand_cmp_weights(cmp_pos_v[l], cmp_v_w1[l], cmp_v_w2[l]),
        bd, tile2(k_norm[l, 0]))

    cs = np.arange(ns)[None, :] * CMP_STRIDE
    ss = np.arange(LANE)[:, None] * SLC_LEN
    ovt = jnp.asarray(((cs < ss + SLC_LEN) & (cs + CMP_LEN > ss)).astype(np.float32), BF16)

    o_nsa = _nsa(qt, ksa, vst, kw, vwt, kc, vct, gt, ovt)
    return o_nsa, o_hg


def kernel(x, norm_mix, w_in, q_norm, k_norm, cmp_pos_k, cmp_pos_v, cmp_k_w1, cmp_k_w2, cmp_v_w1, cmp_v_w2,
           hgrn_lb_logits, hgrn_o_norm, w_out, norm_ffn, w_gate_up, w_down):
    b, t, d = x.shape
    l = 0
    o_nsa, o_hg = _mixers(x, norm_mix, w_in, q_norm, k_norm, cmp_pos_k, cmp_pos_v, cmp_k_w1, cmp_k_w2,
                          cmp_v_w1, cmp_v_w2, hgrn_lb_logits, hgrn_o_norm)
    wo = w_out[l]
    out = _out_ffn(
        x.reshape(b * t, d), o_nsa.reshape(b * t, NSA_WIDTH), o_hg.reshape(b * t, HG_WIDTH),
        wo[:NSA_WIDTH].astype(BF16), wo[NSA_WIDTH:].astype(BF16),
        norm_ffn[l].reshape(1, d), w_gate_up[l].astype(BF16), w_down[l].astype(BF16))
    return out.reshape(b, t, d)
```

```python
import functools
import math

import numpy as np
import jax
import jax.numpy as jnp
from jax import lax
from jax.experimental import pallas as pl
from jax.experimental.pallas import tpu as pltpu

F32 = jnp.float32
BF16 = jnp.bfloat16

LANE = 128
BF16_SUBLANES = 16
MXU_WIDTH = 256

NSA_HEADS = 8
NSA_GROUPS = 2
NSA_REP = NSA_HEADS // NSA_GROUPS
HEAD_DIM = 64
CMP_LEN = 32
CMP_STRIDE = 16
CMP_HIDDEN = 128
SLC_LEN = 64
N_SELECT = 16
WINDOW = 512
Q_BLOCK = 128
HGRN_HEADS = 4
HGRN_DIM = 128
HGRN_CHUNK = 64
HGRN_SUB = 16
RMS_EPS = 1e-6
NEG = -1e30

NSA_WIDTH = NSA_HEADS * HEAD_DIM
KV_WIDTH = NSA_GROUPS * HEAD_DIM
HG_WIDTH = HGRN_HEADS * HGRN_DIM
N_GATES = 3 * NSA_HEADS

KEY_TILE = 128
KEY_UNIT = 4 * KEY_TILE
N_PAIRS = NSA_HEADS // 2
V_AUG = HEAD_DIM + 16
VMEM_LIMIT = 56 * 1024 * 1024


def _dot(a, b):
    return jnp.dot(a, b, preferred_element_type=F32)


def _dot_t(a, b):
    return lax.dot_general(a, b, (((1,), (1,)), ((), ())), preferred_element_type=F32)


def _group_rms(y, bd, gain):
    ss = _dot((y * y).astype(BF16), bd)
    return y * lax.rsqrt(ss * (1.0 / HEAD_DIM) + RMS_EPS) * gain


def _values_t_aug(v):
    vt = v.T
    ones = jnp.ones((V_AUG - HEAD_DIM, v.shape[0]), F32)
    return jnp.concatenate([vt[0:HEAD_DIM], ones, vt[HEAD_DIM:2 * HEAD_DIM], ones], axis=0).astype(BF16)


def _hgrn_stages(ph_ref, lb_ref, og_ref, o_ref, st_ref, chunks):
    c_len = HGRN_CHUNK
    n_sub = c_len // HGRN_SUB
    lg = lb_ref[...]
    lmax = jnp.max(lg, axis=0, keepdims=True)
    le = jnp.exp(lg - lmax)
    lb_all = le[0:1] / jnp.sum(le, axis=0, keepdims=True)

    ri = lax.broadcasted_iota(jnp.int32, (c_len, c_len), 0)
    ci = lax.broadcasted_iota(jnp.int32, (c_len, c_len), 1)
    tri = (ci <= ri).astype(BF16)
    causal = ci <= ri
    gain = og_ref[...]
    heads = [slice(h * HGRN_DIM, (h + 1) * HGRN_DIM) for h in range(HGRN_HEADS)]
    rows = [slice(c * c_len, (c + 1) * c_len) for c in range(chunks)]

    pre = []
    for rs in rows:
        qv = jax.nn.silu(ph_ref[rs, 0:HG_WIDTH])
        f = lb_all + (1.0 - lb_all) * jax.nn.sigmoid(ph_ref[rs, HG_WIDTH:2 * HG_WIDTH])
        logf = jnp.log(f)
        g1 = logf.astype(BF16)
        g2 = (logf - g1.astype(F32)).astype(BF16)
        pre.append((qv, 1.0 - f, g1, g2))
    his = [ph_ref[rs, 2 * HG_WIDTH:3 * HG_WIDTH] for rs in rows]
    gates = [jax.nn.silu(ph_ref[rs, 3 * HG_WIDTH:4 * HG_WIDTH]) for rs in rows]
    yield

    bcums = [_dot(tri, g1) + _dot(tri, g2) for _, _, g1, g2 in pre]
    yield

    mid = []
    for c in range(chunks):
        qv, kv = pre[c][0], pre[c][1]
        bcum = bcums[c]
        b_last = bcum[c_len - 1:c_len]
        starts = [jnp.zeros((1, HG_WIDTH), F32)] + [bcum[i * HGRN_SUB - 1:i * HGRN_SUB] for i in range(1, n_sub)]
        ends = starts[1:] + [b_last]

        def per_sub(vals):
            return jnp.concatenate([jnp.broadcast_to(r, (HGRN_SUB, HG_WIDTH)) for r in vals], axis=0)

        start_full = per_sub(starts)
        qd = qv * jnp.exp(bcum - start_full)
        kb = kv * jnp.exp(per_sub(ends) - bcum)
        kdiag = kv * jnp.exp(start_full - bcum)
        qs = (qd * per_sub([jnp.exp(s) for s in starts])).astype(BF16)
        kdec = (kb * per_sub([jnp.exp(b_last - e) for e in ends])).astype(BF16)
        kds = []
        for i in range(n_sub):
            blocks = [kb[j * HGRN_SUB:(j + 1) * HGRN_SUB] * jnp.exp(starts[i] - ends[j]) for j in range(i - 1)]
            if i > 0:
                blocks.append(kb[(i - 1) * HGRN_SUB:i * HGRN_SUB])
            blocks.append(kdiag[i * HGRN_SUB:(i + 1) * HGRN_SUB])
            if i + 1 < n_sub:
                blocks.append(jnp.zeros(((n_sub - 1 - i) * HGRN_SUB, HG_WIDTH), F32))
            kds.append(jnp.concatenate(blocks, axis=0).astype(BF16))
        mid.append((qd.astype(BF16), kds, qs, kdec, jnp.exp(b_last)))
    attn = {}
    upd = {}
    for c in range(chunks):
        qd, kds, qs, kdec, dec = mid[c]
        for h, hs in enumerate(heads):
            both = _dot_t(qd[:, hs], jnp.concatenate([kd[:, hs] for kd in kds], axis=0))
            own = jnp.concatenate([both[i * HGRN_SUB:(i + 1) * HGRN_SUB, i * c_len:(i + 1) * c_len]
                                   for i in range(n_sub)], axis=0)
            attn[c, h] = jnp.where(causal, own, 0.0).astype(BF16)
            upd[c, h] = _dot(his[c][:, hs].T.astype(BF16), kdec[:, hs])
    yield

    state = {}
    for h, hs in enumerate(heads):
        st = st_ref[h]
        for c in range(chunks):
            state[c, h] = st.astype(BF16)
            st = st * mid[c][4][:, hs] + upd[c, h]
        st_ref[h] = st
    for c, rs in enumerate(rows):
        qs = mid[c][2]
        for h, hs in enumerate(heads):
            o = _dot(attn[c, h], his[c][:, hs].astype(BF16)) + _dot_t(qs[:, hs], state[c, h])
            ms = jnp.mean(o * o, axis=-1, keepdims=True)
            o = o * lax.rsqrt(ms + RMS_EPS) * gain * gates[c][:, hs]
            o_ref[0, rs, hs] = o.astype(BF16)
    yield


def _in_proj_kernel(x_ref, g_ref, w_ref, bd_ref, qg_ref, ksg_ref, kwg_ref, lb_ref, og_ref, *refs, n_cast):
    cast_in, refs = refs[:n_cast], refs[n_cast:]
    qt_ref, kc_ref, vc_ref, ksa_ref, vst_ref, kw_ref, vwt_ref, gt_ref, ohg_ref = refs[:9]
    cast_out, (wp_ref, ph_ref, st_ref) = refs[9:9 + n_cast], refs[9 + n_cast:]
    for src_ref, dst_ref in zip(cast_in, cast_out):
        dst_ref[...] = src_ref[...].astype(BF16)

    tm = x_ref.shape[1]
    d = x_ref.shape[2]
    j = pl.program_id(1)
    tile = jnp.minimum(j, pl.num_programs(1) - 2)
    o_g = NSA_WIDTH + 6 * KV_WIDTH

    @pl.when((pl.program_id(0) == 0) & (j == 0))
    def _():
        o_h = o_g + N_GATES
        rows_per = 128

        def copy_rows(i, carry):
            rs = pl.ds(pl.multiple_of(i * rows_per, rows_per), rows_per)
            wp_ref[rs, 0:o_g] = w_ref[rs, 0:o_g].astype(BF16)
            wp_ref[rs, o_g:o_g + 4 * HG_WIDTH] = w_ref[rs, o_h:o_h + 4 * HG_WIDTH].astype(BF16)
            tail = jnp.concatenate(
                [w_ref[rs, o_g:o_h], jnp.zeros((rows_per, LANE - N_GATES), F32)], axis=1)
            wp_ref[rs, o_g + 4 * HG_WIDTH:o_g + 4 * HG_WIDTH + LANE] = tail.astype(BF16)
            return carry

        lax.fori_loop(0, d // rows_per, copy_rows, 0)
        ph_ref[...] = jnp.zeros(ph_ref.shape, F32)
        st_ref[...] = jnp.zeros(st_ref.shape, F32)

    x = x_ref[0]
    ms = jnp.mean(x * x, axis=-1, keepdims=True)
    h = (x * lax.rsqrt(ms + RMS_EPS) * g_ref[...]).astype(BF16)

    mixer = _hgrn_stages(ph_ref, lb_ref, og_ref, ohg_ref, st_ref, tm // HGRN_CHUNK)
    next(mixer)
    y_nsa = _dot(h, wp_ref[:, 0:o_g])
    next(mixer)

    def hgrn_cols(k):
        return _dot(h, wp_ref[:, o_g + k * HG_WIDTH:o_g + (k + 1) * HG_WIDTH])

    y_h = [hgrn_cols(0), hgrn_cols(1)]
    next(mixer)

    bd = bd_ref[...]
    for r in range(NSA_WIDTH // LANE):
        sl = slice(r * LANE, (r + 1) * LANE)
        qt_ref[0, sl, :] = _group_rms(y_nsa[:, sl], bd, qg_ref[...]).T.astype(BF16)
    o = NSA_WIDTH
    kc_ref[0] = y_nsa[:, o:o + LANE].astype(BF16)
    vc_ref[0] = y_nsa[:, o + LANE:o + 2 * LANE].astype(BF16)
    ksa_ref[0, :, 0:LANE] = _group_rms(y_nsa[:, o + 2 * LANE:o + 3 * LANE], bd, ksg_ref[...]).astype(BF16)
    key = tile * tm + lax.broadcasted_iota(jnp.int32, (tm, LANE), 0)
    blk = lax.broadcasted_iota(jnp.int32, (tm, LANE), 1)
    ksa_ref[0, :, LANE:2 * LANE] = jnp.where(lax.shift_right_logical(key, 6) == blk, 1.0, 0.0).astype(BF16)
    vst_ref[0] = _values_t_aug(y_nsa[:, o + 3 * LANE:o + 4 * LANE])
    kw_ref[0] = _group_rms(y_nsa[:, o + 4 * LANE:o + 5 * LANE], bd, kwg_ref[...]).astype(BF16)
    vwt_ref[0] = _values_t_aug(y_nsa[:, o + 5 * LANE:o + 6 * LANE])
    y_gate = _dot(h, wp_ref[:, o_g + 4 * HG_WIDTH:o_g + 4 * HG_WIDTH + LANE])
    gt_ref[0] = jax.nn.sigmoid(y_gate).T
    y_h.append(hgrn_cols(2))
    next(mixer)
    y_h.append(hgrn_cols(3))
    for k in range(4):
        ph_ref[:, k * HG_WIDTH:(k + 1) * HG_WIDTH] = y_h[k]

    @pl.when(j == 0)
    def _():
        st_ref[...] = jnp.zeros(st_ref.shape, F32)


def _slab_rows(rows, n_steps):
    return next(r for r in range(BF16_SUBLANES, rows + 1, BF16_SUBLANES) if rows % r == 0 and rows // r <= n_steps)


def _in_proj(x, g, w, bd, qg, ksg, kwg, lb_logits, o_gain, to_cast, tm=256):
    b, t, d = x.shape
    n_t = t // tm
    slabs = [_slab_rows(a.shape[0], b * (n_t + 1)) for a in to_cast]

    def slab_spec(a, r):
        return pl.BlockSpec((r, a.shape[1]), lambda i, j: (jnp.minimum(i * (n_t + 1) + j, a.shape[0] // r - 1), 0))

    cast_specs = [slab_spec(a, r) for a, r in zip(to_cast, slabs)]
    nw = NSA_WIDTH + 6 * KV_WIDTH + 4 * HG_WIDTH + LANE
    rows = lambda i, j: (i, jnp.minimum(j, n_t - 1), 0)
    cols = lambda i, j: (i, 0, jnp.minimum(j, n_t - 1))
    prev = lambda i, j: (i, jnp.maximum(j - 1, 0), 0)
    const = lambda i, j: (0, 0)
    outs = [
        (jax.ShapeDtypeStruct((b, NSA_WIDTH, t), BF16), pl.BlockSpec((1, NSA_WIDTH, tm), cols)),
        (jax.ShapeDtypeStruct((b, t, LANE), BF16), pl.BlockSpec((1, tm, LANE), rows)),
        (jax.ShapeDtypeStruct((b, t, LANE), BF16), pl.BlockSpec((1, tm, LANE), rows)),
        (jax.ShapeDtypeStruct((b, t, 2 * LANE), BF16), pl.BlockSpec((1, tm, 2 * LANE), rows)),
        (jax.ShapeDtypeStruct((b, 2 * V_AUG, t), BF16), pl.BlockSpec((1, 2 * V_AUG, tm), cols)),
        (jax.ShapeDtypeStruct((b, t, LANE), BF16), pl.BlockSpec((1, tm, LANE), rows)),
        (jax.ShapeDtypeStruct((b, 2 * V_AUG, t), BF16), pl.BlockSpec((1, 2 * V_AUG, tm), cols)),
        (jax.ShapeDtypeStruct((b, LANE, t), F32), pl.BlockSpec((1, LANE, tm), cols)),
        (jax.ShapeDtypeStruct((b, t, HG_WIDTH), BF16), pl.BlockSpec((1, tm, HG_WIDTH), prev)),
    ]
    outs += [(jax.ShapeDtypeStruct(a.shape, BF16), s) for a, s in zip(to_cast, cast_specs)]
    return pl.pallas_call(
        functools.partial(_in_proj_kernel, n_cast=len(to_cast)),
        grid=(b, n_t + 1),
        in_specs=[
            pl.BlockSpec((1, tm, d), rows),
            pl.BlockSpec((1, d), const),
            pl.BlockSpec(w.shape, const, pipeline_mode=pl.Buffered(1)),
            pl.BlockSpec((LANE, LANE), const),
            pl.BlockSpec((1, LANE), const),
            pl.BlockSpec((1, LANE), const),
            pl.BlockSpec((1, LANE), const),
            pl.BlockSpec(lb_logits.shape, const),
            pl.BlockSpec((1, HGRN_DIM), const),
        ] + cast_specs,
        out_specs=[s for _, s in outs],
        out_shape=[s for s, _ in outs],
        scratch_shapes=[
            pltpu.VMEM((d, nw), BF16),
            pltpu.VMEM((tm, 4 * HG_WIDTH), F32),
            pltpu.VMEM((HGRN_HEADS, HGRN_DIM, HGRN_DIM), F32),
        ],
        compiler_params=pltpu.CompilerParams(
            dimension_semantics=("arbitrary", "arbitrary"), vmem_limit_bytes=VMEM_LIMIT),
        name="in_proj",
    )(x, g, w, bd, qg, ksg, kwg, lb_logits, o_gain, *to_cast)


def _compress_kernel(xk_ref, xv_ref, wklo_ref, wkhi_ref, wk2_ref, pk_ref,
                     wvlo_ref, wvhi_ref, wv2_ref, pv_ref, bd_ref, kg_ref, kc_ref, vct_ref):
    ns = xk_ref.shape[1]

    def mlp(x_ref, wlo_ref, whi_ref, w2_ref, p_ref):
        xb = x_ref[0]
        a = _dot(xb, wlo_ref[...])
        b = _dot(xb, whi_ref[...])
        h = a + pltpu.roll(b, ns - 1, axis=0) + p_ref[...]
        return _dot(jax.nn.gelu(h).astype(BF16), w2_ref[...])

    kc = mlp(xk_ref, wklo_ref, wkhi_ref, wk2_ref, pk_ref)
    kc_ref[0] = _group_rms(kc, bd_ref[...], kg_ref[...]).astype(BF16)
    vct_ref[0] = _values_t_aug(mlp(xv_ref, wvlo_ref, wvhi_ref, wv2_ref, pv_ref))


def _compress(xk, xv, wk, wv, bd, kg):
    b, ns, wd = xk.shape
    const2 = lambda i: (0, 0)
    bat = lambda i: (i, 0, 0)
    wspecs = [pl.BlockSpec(w.shape, const2) for w in wk] + [pl.BlockSpec(w.shape, const2) for w in wv]
    return pl.pallas_call(
        _compress_kernel,
        grid=(b,),
        in_specs=[pl.BlockSpec((1, ns, wd), bat), pl.BlockSpec((1, ns, wd), bat)] + wspecs + [
            pl.BlockSpec((LANE, LANE), const2), pl.BlockSpec((1, LANE), const2)],
        out_specs=[pl.BlockSpec((1, ns, LANE), bat), pl.BlockSpec((1, 2 * V_AUG, ns), bat)],
        out_shape=[jax.ShapeDtypeStruct((b, ns, LANE), BF16), jax.ShapeDtypeStruct((b, 2 * V_AUG, ns), BF16)],
        compiler_params=pltpu.CompilerParams(
            dimension_semantics=("arbitrary",), vmem_limit_bytes=VMEM_LIMIT),
        name="compress",
    )(xk, xv, *wk, *wv, bd, kg)


def _topk_bias_t(imp_ts, s0):
    nj, nq = imp_ts[0].shape
    jidx = lax.broadcasted_iota(jnp.int32, (nj, nq), 0)
    tq = s0 + lax.broadcasted_iota(jnp.int32, (nj, nq), 1)
    jcur = lax.shift_right_logical(tq, 6)
    forced = (jidx == 0) | (jidx == jcur) | (jidx == jcur - 1)
    future = jidx > jcur
    vs = [jnp.where(forced, -2.0, jnp.where(future, -1.0, imp_t)) for imp_t in imp_ts]
    jf = jidx.astype(F32)
    for _ in range(N_SELECT - 3):
        for g in range(len(vs)):
            m = jnp.max(vs[g], axis=0, keepdims=True)
            jm = jnp.min(jnp.where(vs[g] == m, jf, float(nj)), axis=0, keepdims=True)
            vs[g] = jnp.where(jf == jm, -2.0, vs[g])
    return [jnp.where(future, NEG, jnp.where(v < -1.5, 0.0, NEG)) for v in vs]


def _nsa_kernel(qt_ref, qtn_ref, ksa_ref, vst_ref, kw_ref, vwt_ref, kc_ref, vct_ref, gt_ref, gtn_ref, ovt_ref, o_ref,
                w_ref, wn_ref, bias_ref, cmp_ref, sc_ref, sw_ref, ss_ref, mx_ref, e_ref, ew_ref,
                m_ref, acc_ref, out_ref):
    qb = pl.program_id(1)
    s0 = qb * Q_BLOCK
    nc = kc_ref.shape[1]
    two = 2 * Q_BLOCK
    lane_q = lax.broadcasted_iota(jnp.int32, (1, two), 1) & (Q_BLOCK - 1)
    t_lane = s0 + lane_q
    gt = gt_ref[0]

    def stationary_q(q_ref, dst):
        zero = jnp.zeros((HEAD_DIM, Q_BLOCK), BF16)
        for p in range(N_PAIRS):
            g = (2 * p) // NSA_REP
            halves = []
            for h in (2 * p, 2 * p + 1):
                qh = q_ref[0, h * HEAD_DIM:(h + 1) * HEAD_DIM, :]
                halves.append(jnp.concatenate([qh, zero] if g == 0 else [zero, qh], axis=0))
            dst[p, 0:LANE, :] = jnp.concatenate(halves, axis=1)

    def store_scores(dst, s, valid):
        if valid is not None:
            s = jnp.where(valid, s, NEG)
        dst[...] = s
        return jnp.max(s, axis=0, keepdims=True)

    def exp_tiles(load, n_tiles, m, e_dst):
        for c in range(n_tiles):
            rows = slice(c * KEY_TILE, (c + 1) * KEY_TILE)
            e_dst[rows, :] = jnp.exp2(load(rows) - m).astype(BF16)

    def values_t(ref, g, cols):
        return ref[0, g * V_AUG:(g + 1) * V_AUG, cols]

    def emit(dst, gates, p, branch, o_aug, first, guard=None):
        inv = 1.0 / o_aug[HEAD_DIM:HEAD_DIM + 1, :]
        if guard is not None:
            inv = jnp.where(guard, inv, 0.0)
        for hh in range(2):
            h = 2 * p + hh
            ls = slice(hh * Q_BLOCK, (hh + 1) * Q_BLOCK)
            o = o_aug[0:HEAD_DIM, ls] * (inv[:, ls] * gates[3 * h + branch:3 * h + branch + 1, :])
            rs = slice(h * HEAD_DIM, (h + 1) * HEAD_DIM)
            if first:
                dst[rs, :] = o
            else:
                dst[rs, :] += o
        return inv

    def key_rows(n):
        return lax.broadcasted_iota(jnp.int32, (n, two), 0)

    def cmp_scores(q_ref, s0x):
        stationary_q(q_ref, wn_ref)
        kc = kc_ref[0]
        valid = key_rows(nc) <= lax.shift_right_arithmetic(s0x + lane_q - (CMP_LEN - 1), 4)
        return [store_scores(sc_ref.at[p], _dot(kc, wn_ref[p]), valid) for p in range(N_PAIRS)]

    def cmp_finish(gates, m_cmp):
        imp_t = [jnp.zeros((LANE, Q_BLOCK), F32) for _ in range(NSA_GROUPS)]
        for p in range(N_PAIRS):
            exp_tiles(lambda rows: sc_ref[p, rows, :], nc // KEY_TILE, m_cmp[p], e_ref.at[p])
        for p in range(N_PAIRS):
            g = (2 * p) // NSA_REP
            inv = emit(cmp_ref, gates, p, 0, _dot(values_t(vct_ref, g, slice(None)), e_ref[p, 0:nc, :]), True,
                       guard=m_cmp[p] > 0.5 * NEG)
            imp_p = _dot(ovt_ref[...], e_ref[p, 0:nc, :]) * inv
            imp_t[g] = imp_t[g] + imp_p[:, 0:Q_BLOCK] + imp_p[:, Q_BLOCK:two]
        return imp_t

    def select(imp_t, s0x):
        for g, bias_t in enumerate(_topk_bias_t(imp_t, s0x)):
            bias_ref[g] = bias_t.astype(BF16)

    @pl.when(qb == 0)
    def _():
        select(cmp_finish(gt, cmp_scores(qt_ref, s0)), s0)

    w0 = pl.multiple_of(jnp.maximum(s0 - WINDOW, 0), KEY_TILE)
    wlen = WINDOW + Q_BLOCK
    kwin = kw_ref[0, pl.ds(w0, wlen), :]
    vwin = [values_t(vwt_ref, g, pl.ds(w0, wlen)) for g in range(NSA_GROUPS)]

    stationary_q(qt_ref, w_ref)
    for p in range(N_PAIRS):
        bias_t = bias_ref[(2 * p) // NSA_REP]
        w_ref[p, LANE:2 * LANE, :] = jnp.concatenate([bias_t, bias_t], axis=1)
    out_ref[...] = cmp_ref[...]

    imp_t = cmp_finish(gtn_ref[0], cmp_scores(qtn_ref, s0 + Q_BLOCK))
    diff = (t_lane - w0) - key_rows(wlen)
    win_valid = lax.shift_right_arithmetic(diff, WINDOW.bit_length() - 1) == 0
    m_win = [store_scores(sw_ref.at[p], _dot(kwin, w_ref[p, 0:LANE, :]), win_valid) for p in range(N_PAIRS)]
    for p in range(N_PAIRS):
        exp_tiles(lambda rows: sw_ref[p, rows, :], wlen // KEY_TILE, m_win[p], ew_ref.at[p])
    for p in range(N_PAIRS):
        emit(out_ref, gt, p, 2, _dot(vwin[(2 * p) // NSA_REP], ew_ref[p]), False)

    m_ref[...] = jnp.full(m_ref.shape, NEG, F32)
    acc_ref[...] = jnp.zeros(acc_ref.shape, F32)

    def sel_scores(u, buf, p, causal):
        k0 = pl.multiple_of(u * KEY_UNIT, KEY_UNIT)
        ku = ksa_ref[0, pl.ds(k0, KEY_UNIT), :]
        valid = (k0 + key_rows(KEY_UNIT)) <= t_lane if causal else None
        mx_ref[buf, p] = store_scores(ss_ref.at[buf, p], _dot(ku, w_ref[p]), valid)

    def sel_update(u, buf, p):
        k0 = pl.multiple_of(u * KEY_UNIT, KEY_UNIT)
        g = (2 * p) // NSA_REP
        m_old = m_ref[p]
        m = jnp.maximum(m_old, mx_ref[buf, p])
        exp_tiles(lambda rows: ss_ref[buf, p, rows, :], KEY_UNIT // KEY_TILE, m, e_ref.at[p])
        m_ref[p] = m
        acc_ref[p] = jnp.exp2(m_old - m) * acc_ref[p] + _dot(
            values_t(vst_ref, g, pl.ds(k0, KEY_UNIT)), e_ref[p, 0:KEY_UNIT, :])

    def sel_step(u, buf, next_causal):
        for p in range(N_PAIRS):
            sel_scores(u + 1, 1 - buf, p, next_causal)
            sel_update(u, buf, p)

    def sel_last(u, buf):
        for p in range(N_PAIRS):
            sel_update(u, buf, p)

    n_before = qb // (KEY_UNIT // Q_BLOCK)
    n_trips = jnp.maximum(n_before - 1, 0) // 2
    rest = 2 * n_trips

    for p in range(N_PAIRS):
        sel_scores(0, 0, p, False)
    select(imp_t, s0 + Q_BLOCK)

    def body(i, carry):
        sel_step(2 * i, 0, False)
        sel_step(2 * i + 1, 1, False)
        return carry

    def body2(i, carry):
        body(2 * i, carry)
        return body(2 * i + 1, carry)

    lax.fori_loop(0, n_trips // 2, body2, 0)
    lax.fori_loop(2 * (n_trips // 2), n_trips, body, 0)

    @pl.when(n_before == 0)
    def _():
        for p in range(N_PAIRS):
            sel_scores(0, 0, p, True)
        sel_last(0, 0)

    @pl.when(n_before == rest + 1)
    def _():
        sel_step(rest, 0, True)
        sel_last(rest + 1, 1)

    @pl.when(n_before == rest + 2)
    def _():
        sel_step(rest, 0, False)
        sel_step(rest + 1, 1, True)
        sel_last(rest + 2, 0)

    for p in range(N_PAIRS):
        emit(out_ref, gt, p, 1, acc_ref[p], False)

    o_ref[0] = out_ref[...].T.astype(BF16)


def _nsa(qt, ksa, vst, kw, vwt, kc, vct, gt, ovt):
    b, _, t = qt.shape
    nc = kc.shape[1]
    n_qb = t // Q_BLOCK
    qcol = lambda i, j: (i, 0, j)
    qnext = lambda i, j: (i, 0, jnp.minimum(j + 1, n_qb - 1))
    full = lambda i, j: (i, 0, 0)
    const = lambda i, j: (0, 0)
    return pl.pallas_call(
        _nsa_kernel,
        grid=(b, n_qb),
        in_specs=[
            pl.BlockSpec((1, NSA_WIDTH, Q_BLOCK), qcol),
            pl.BlockSpec((1, NSA_WIDTH, Q_BLOCK), qnext),
            pl.BlockSpec((1, t, 2 * LANE), full),
            pl.BlockSpec((1, 2 * V_AUG, t), full),
            pl.BlockSpec((1, t, LANE), full),
            pl.BlockSpec((1, 2 * V_AUG, t), full),
            pl.BlockSpec((1, nc, LANE), full),
            pl.BlockSpec((1, 2 * V_AUG, nc), full),
            pl.BlockSpec((1, LANE, Q_BLOCK), qcol),
            pl.BlockSpec((1, LANE, Q_BLOCK), qnext),
            pl.BlockSpec(ovt.shape, const),
        ],
        out_specs=pl.BlockSpec((1, Q_BLOCK, NSA_WIDTH), lambda i, j: (i, j, 0)),
        out_shape=jax.ShapeDtypeStruct((b, t, NSA_WIDTH), BF16),
        scratch_shapes=[
            pltpu.VMEM((N_PAIRS, 2 * LANE, 2 * Q_BLOCK), BF16),
            pltpu.VMEM((N_PAIRS, LANE, 2 * Q_BLOCK), BF16),
            pltpu.VMEM((NSA_GROUPS, LANE, Q_BLOCK), BF16),
            pltpu.VMEM((NSA_WIDTH, Q_BLOCK), F32),
            pltpu.VMEM((N_PAIRS, nc, 2 * Q_BLOCK), F32),
            pltpu.VMEM((N_PAIRS, WINDOW + Q_BLOCK, 2 * Q_BLOCK), F32),
            pltpu.VMEM((2, N_PAIRS, KEY_UNIT, 2 * Q_BLOCK), F32),
            pltpu.VMEM((2, N_PAIRS, 1, 2 * Q_BLOCK), F32),
            pltpu.VMEM((N_PAIRS, max(nc, KEY_UNIT), 2 * Q_BLOCK), BF16),
            pltpu.VMEM((N_PAIRS, WINDOW + Q_BLOCK, 2 * Q_BLOCK), BF16),
            pltpu.VMEM((N_PAIRS, 1, 2 * Q_BLOCK), F32),
            pltpu.VMEM((N_PAIRS, V_AUG, 2 * Q_BLOCK), F32),
            pltpu.VMEM((NSA_WIDTH, Q_BLOCK), F32),
        ],
        compiler_params=pltpu.CompilerParams(
            dimension_semantics=("arbitrary", "arbitrary"), vmem_limit_bytes=VMEM_LIMIT),
        name="nsa",
    )(qt, qt, ksa, vst, kw, vwt, kc, vct, gt, gt, ovt)


def _ff_chunks(d_ff, width):
    return tuple((s, min(width, d_ff - s)) for s in range(0, d_ff, width))


def _out_ffn_kernel(x_ref, on_ref, oh_ref, wo_ref, g_ref, wgu_ref, wd_ref, o_ref, *, chunks):
    d_ff = wd_ref.shape[0]
    x1 = (x_ref[...] + _dot(on_ref[...], wo_ref[0:NSA_WIDTH, :])
          + _dot(oh_ref[...], wo_ref[NSA_WIDTH:NSA_WIDTH + HG_WIDTH, :]))
    ms = jnp.mean(x1 * x1, axis=-1, keepdims=True)
    h = (x1 * lax.rsqrt(ms + RMS_EPS) * g_ref[...]).astype(BF16)
    o_ref[...] = x1
    for s, n in chunks:
        gate = _dot(h, wgu_ref[:, s:s + n])
        up = _dot(h, wgu_ref[:, d_ff + s:d_ff + s + n])
        act = (jax.nn.silu(gate) * up).astype(BF16)
        o_ref[...] += _dot(act, wd_ref[s:s + n, :])


def _out_ffn(x2, o_nsa, o_hg, wo, g, wgu, wd, tm=512, ff_width=4 * MXU_WIDTH):
    n, d = x2.shape
    d_ff = wd.shape[0]
    row = lambda i: (i, 0)
    const = lambda i: (0, 0)
    resident = lambda a: pl.BlockSpec(a.shape, const, pipeline_mode=pl.Buffered(1))
    return pl.pallas_call(
        functools.partial(_out_ffn_kernel, chunks=_ff_chunks(d_ff, ff_width)),
        grid=(n // tm,),
        in_specs=[
            pl.BlockSpec((tm, d), row),
            pl.BlockSpec((tm, NSA_WIDTH), row),
            pl.BlockSpec((tm, HG_WIDTH), row),
            resident(wo),
            pl.BlockSpec((1, d), const),
            resident(wgu),
            resident(wd),
        ],
        out_specs=pl.BlockSpec((tm, d), row),
        out_shape=jax.ShapeDtypeStruct((n, d), F32),
        compiler_params=pltpu.CompilerParams(
            dimension_semantics=("arbitrary",), vmem_limit_bytes=VMEM_LIMIT),
        name="out_ffn",
    )(x2, o_nsa, o_hg, wo, g, wgu, wd)


def _expand_cmp_weights(pos, w1, w2):
    eye = jnp.eye(NSA_GROUPS, dtype=F32)
    w1r = w1.reshape(CMP_LEN, HEAD_DIM, CMP_HIDDEN)

    def lift(wpart):
        return jnp.einsum('ldc,gk->lgdkc', wpart, eye).reshape(
            CMP_STRIDE * NSA_GROUPS * HEAD_DIM, NSA_GROUPS * CMP_HIDDEN)

    wlo = lift(w1r[:CMP_STRIDE]).astype(BF16)
    whi = lift(w1r[CMP_STRIDE:]).astype(BF16)
    w2x = jnp.einsum('cd,gk->gckd', w2, eye).reshape(NSA_GROUPS * CMP_HIDDEN, NSA_GROUPS * HEAD_DIM).astype(BF16)
    pbias = jnp.tile(pos.reshape(1, CMP_LEN * HEAD_DIM) @ w1, (1, NSA_GROUPS))
    return wlo, whi, w2x, pbias


def _mixers(x, norm_mix, w_in, q_norm, k_norm, cmp_pos_k, cmp_pos_v, cmp_k_w1, cmp_k_w2, cmp_v_w1, cmp_v_w2,
            hgrn_lb_logits, hgrn_o_norm, to_cast=()):
    b, t, d = x.shape
    depth = norm_mix.shape[0]
    assert depth == 1 and hgrn_lb_logits.shape[0] == 2
    assert t % KEY_UNIT == 0 and t >= WINDOW + Q_BLOCK and t // SLC_LEN <= LANE
    assert (t // CMP_STRIDE) % KEY_TILE == 0
    l = 0

    bd = jnp.asarray(np.kron(np.eye(LANE // HEAD_DIM), np.ones((HEAD_DIM, HEAD_DIM))), BF16)
    tile2 = lambda v: jnp.tile(v.reshape(1, HEAD_DIM), (1, LANE // HEAD_DIM)).astype(F32)
    qg = tile2(q_norm[l]) * (HEAD_DIM ** -0.5 * math.log2(math.e))

    qt, kc_raw, vc_raw, ksa, vst, kw, vwt, gt, o_hg, *casted = _in_proj(
        x, norm_mix[l].reshape(1, d), w_in[l], bd, qg, tile2(k_norm[l, 1]), tile2(k_norm[l, 2]),
        hgrn_lb_logits, hgrn_o_norm[l].reshape(1, HGRN_DIM), list(to_cast))

    ns = t // CMP_STRIDE
    seg_w = CMP_STRIDE * KV_WIDTH
    kc, vct = _compress(
        kc_raw.reshape(b, ns, seg_w), vc_raw.reshape(b, ns, seg_w),
        _expand_cmp_weights(cmp_pos_k[l], cmp_k_w1[l], cmp_k_w2[l]),
        _expand_cmp_weights(cmp_pos_v[l], cmp_v_w1[l], cmp_v_w2[l]),
        bd, tile2(k_norm[l, 0]))

    cs = np.arange(ns)[None, :] * CMP_STRIDE
    ss = np.arange(LANE)[:, None] * SLC_LEN
    ovt = jnp.asarray(((cs < ss + SLC_LEN) & (cs + CMP_LEN > ss)).astype(np.float32), BF16)

    o_nsa = _nsa(qt, ksa, vst, kw, vwt, kc, vct, gt, ovt)
    return o_nsa, o_hg, casted


def kernel(x, norm_mix, w_in, q_norm, k_norm, cmp_pos_k, cmp_pos_v, cmp_k_w1, cmp_k_w2, cmp_v_w1, cmp_v_w2,
           hgrn_lb_logits, hgrn_o_norm, w_out, norm_ffn, w_gate_up, w_down):
    b, t, d = x.shape
    l = 0
    o_nsa, o_hg, (wo, wgu, wd) = _mixers(
        x, norm_mix, w_in, q_norm, k_norm, cmp_pos_k, cmp_pos_v, cmp_k_w1, cmp_k_w2, cmp_v_w1, cmp_v_w2,
        hgrn_lb_logits, hgrn_o_norm, to_cast=(w_out[l], w_gate_up[l], w_down[l]))
    out = _out_ffn(
        x.reshape(b * t, d), o_nsa.reshape(b * t, NSA_WIDTH), o_hg.reshape(b * t, HG_WIDTH),
        wo, norm_ffn[l].reshape(1, d), wgu, wd)
    return out.reshape(b, t, d)
```

```python
import functools
import math

import numpy as np
import jax
import jax.numpy as jnp
from jax import lax
from jax.experimental import pallas as pl
from jax.experimental.pallas import tpu as pltpu

F32 = jnp.float32
BF16 = jnp.bfloat16

LANE = 128
BF16_SUBLANES = 16
MXU_WIDTH = 256

NSA_HEADS = 8
NSA_GROUPS = 2
NSA_REP = NSA_HEADS // NSA_GROUPS
HEAD_DIM = 64
CMP_LEN = 32
CMP_STRIDE = 16
CMP_HIDDEN = 128
SLC_LEN = 64
N_SELECT = 16
WINDOW = 512
Q_BLOCK = 128
HGRN_HEADS = 4
HGRN_DIM = 128
HGRN_CHUNK = 64
HGRN_SUB = 16
RMS_EPS = 1e-6
NEG = -1e30

NSA_WIDTH = NSA_HEADS * HEAD_DIM
KV_WIDTH = NSA_GROUPS * HEAD_DIM
HG_WIDTH = HGRN_HEADS * HGRN_DIM
N_GATES = 3 * NSA_HEADS

KEY_TILE = 128
KEY_UNIT = 4 * KEY_TILE
N_PAIRS = NSA_HEADS // 2
V_AUG = HEAD_DIM + 16
VMEM_LIMIT = 56 * 1024 * 1024


def _dot(a, b):
    return jnp.dot(a, b, preferred_element_type=F32)


def _dot_t(a, b):
    return lax.dot_general(a, b, (((1,), (1,)), ((), ())), preferred_element_type=F32)


def _group_rms(y, bd, gain):
    ss = _dot((y * y).astype(BF16), bd)
    return y * lax.rsqrt(ss * (1.0 / HEAD_DIM) + RMS_EPS) * gain


def _values_t_aug(v):
    vt = v.T
    ones = jnp.ones((V_AUG - HEAD_DIM, v.shape[0]), F32)
    return jnp.concatenate([vt[0:HEAD_DIM], ones, vt[HEAD_DIM:2 * HEAD_DIM], ones], axis=0).astype(BF16)


def _hgrn_stages(ph_ref, lb_ref, og_ref, o_ref, st_ref, chunks):
    c_len = HGRN_CHUNK
    n_sub = c_len // HGRN_SUB
    lg = lb_ref[...]
    lmax = jnp.max(lg, axis=0, keepdims=True)
    le = jnp.exp(lg - lmax)
    lb_all = le[0:1] / jnp.sum(le, axis=0, keepdims=True)

    ri = lax.broadcasted_iota(jnp.int32, (c_len, c_len), 0)
    ci = lax.broadcasted_iota(jnp.int32, (c_len, c_len), 1)
    tri = (ci <= ri).astype(BF16)
    causal = ci <= ri
    gain = og_ref[...]
    heads = [slice(h * HGRN_DIM, (h + 1) * HGRN_DIM) for h in range(HGRN_HEADS)]
    rows = [slice(c * c_len, (c + 1) * c_len) for c in range(chunks)]

    pre = []
    for rs in rows:
        qv = jax.nn.silu(ph_ref[rs, 0:HG_WIDTH])
        f = lb_all + (1.0 - lb_all) * jax.nn.sigmoid(ph_ref[rs, HG_WIDTH:2 * HG_WIDTH])
        logf = jnp.log(f)
        g1 = logf.astype(BF16)
        g2 = (logf - g1.astype(F32)).astype(BF16)
        pre.append((qv, 1.0 - f, g1, g2))
    his = [ph_ref[rs, 2 * HG_WIDTH:3 * HG_WIDTH] for rs in rows]
    gates = [jax.nn.silu(ph_ref[rs, 3 * HG_WIDTH:4 * HG_WIDTH]) for rs in rows]
    yield

    bcums = [_dot(tri, g1) + _dot(tri, g2) for _, _, g1, g2 in pre]
    yield

    mid = []
    for c in range(chunks):
        qv, kv = pre[c][0], pre[c][1]
        bcum = bcums[c]
        b_last = bcum[c_len - 1:c_len]
        starts = [jnp.zeros((1, HG_WIDTH), F32)] + [bcum[i * HGRN_SUB - 1:i * HGRN_SUB] for i in range(1, n_sub)]
        ends = starts[1:] + [b_last]

        def per_sub(vals):
            return jnp.concatenate([jnp.broadcast_to(r, (HGRN_SUB, HG_WIDTH)) for r in vals], axis=0)

        start_full = per_sub(starts)
        qd = qv * jnp.exp(bcum - start_full)
        kb = kv * jnp.exp(per_sub(ends) - bcum)
        kdiag = kv * jnp.exp(start_full - bcum)
        qs = (qd * per_sub([jnp.exp(s) for s in starts])).astype(BF16)
        kdec = (kb * per_sub([jnp.exp(b_last - e) for e in ends])).astype(BF16)
        kds = []
        for i in range(n_sub):
            blocks = [kb[j * HGRN_SUB:(j + 1) * HGRN_SUB] * jnp.exp(starts[i] - ends[j]) for j in range(i - 1)]
            if i > 0:
                blocks.append(kb[(i - 1) * HGRN_SUB:i * HGRN_SUB])
            blocks.append(kdiag[i * HGRN_SUB:(i + 1) * HGRN_SUB])
            if i + 1 < n_sub:
                blocks.append(jnp.zeros(((n_sub - 1 - i) * HGRN_SUB, HG_WIDTH), F32))
            kds.append(jnp.concatenate(blocks, axis=0).astype(BF16))
        mid.append((qd.astype(BF16), kds, qs, kdec, jnp.exp(b_last)))
    attn = {}
    upd = {}
    for c in range(chunks):
        qd, kds, qs, kdec, dec = mid[c]
        for h, hs in enumerate(heads):
            both = _dot_t(qd[:, hs], jnp.concatenate([kd[:, hs] for kd in kds], axis=0))
            own = jnp.concatenate([both[i * HGRN_SUB:(i + 1) * HGRN_SUB, i * c_len:(i + 1) * c_len]
                                   for i in range(n_sub)], axis=0)
            attn[c, h] = jnp.where(causal, own, 0.0).astype(BF16)
            upd[c, h] = _dot(his[c][:, hs].T.astype(BF16), kdec[:, hs])
    yield

    state = {}
    for h, hs in enumerate(heads):
        st = st_ref[h]
        for c in range(chunks):
            state[c, h] = st.astype(BF16)
            st = st * mid[c][4][:, hs] + upd[c, h]
        st_ref[h] = st
    for c, rs in enumerate(rows):
        qs = mid[c][2]
        for h, hs in enumerate(heads):
            o = _dot(attn[c, h], his[c][:, hs].astype(BF16)) + _dot_t(qs[:, hs], state[c, h])
            ms = jnp.mean(o * o, axis=-1, keepdims=True)
            o = o * lax.rsqrt(ms + RMS_EPS) * gain * gates[c][:, hs]
            o_ref[0, rs, hs] = o.astype(BF16)
    yield


def _in_proj_kernel(x_ref, g_ref, wt_ref, bd_ref, qg_ref, ksg_ref, kwg_ref, lb_ref, og_ref, *refs, n_cast):
    cast_in, refs = refs[:n_cast], refs[n_cast:]
    qt_ref, kc_ref, vc_ref, ksa_ref, vst_ref, kw_ref, vwt_ref, gt_ref, ohg_ref = refs[:9]
    cast_out, (wp_ref, ph_ref, st_ref) = refs[9:9 + n_cast], refs[9 + n_cast:]
    for src_ref, dst_ref in zip(cast_in, cast_out):
        dst_ref[...] = src_ref[...].astype(BF16)

    tm = x_ref.shape[1]
    d = x_ref.shape[2]
    j = pl.program_id(1)
    tile = jnp.minimum(j, pl.num_programs(1) - 2)
    o_g = NSA_WIDTH + 6 * KV_WIDTH

    @pl.when((pl.program_id(0) == 0) & (j == 0))
    def _():
        o_h = o_g + N_GATES
        sources = [c * LANE for c in range(o_g // LANE)] + [o_h + c * LANE for c in range(4 * HG_WIDTH // LANE)]

        def copy_rows(i, carry):
            rs = pl.ds(pl.multiple_of(i * LANE, LANE), LANE)
            for c, src in enumerate(sources):
                wp_ref[rs, c * LANE:(c + 1) * LANE] = wt_ref[src:src + LANE, rs].T.astype(BF16)
            tail = jnp.concatenate([wt_ref[o_g:o_h, rs], jnp.zeros((LANE - N_GATES, LANE), F32)], axis=0)
            wp_ref[rs, o_g + 4 * HG_WIDTH:o_g + 4 * HG_WIDTH + LANE] = tail.T.astype(BF16)
            return carry

        lax.fori_loop(0, d // LANE, copy_rows, 0)
        ph_ref[...] = jnp.zeros(ph_ref.shape, F32)
        st_ref[...] = jnp.zeros(st_ref.shape, F32)

    x = x_ref[0]
    ms = jnp.mean(x * x, axis=-1, keepdims=True)
    h = (x * lax.rsqrt(ms + RMS_EPS) * g_ref[...]).astype(BF16)

    mixer = _hgrn_stages(ph_ref, lb_ref, og_ref, ohg_ref, st_ref, tm // HGRN_CHUNK)
    next(mixer)
    y_nsa = _dot(h, wp_ref[:, 0:o_g])
    next(mixer)

    def hgrn_cols(k):
        return _dot(h, wp_ref[:, o_g + k * HG_WIDTH:o_g + (k + 1) * HG_WIDTH])

    y_h = [hgrn_cols(0), hgrn_cols(1)]
    next(mixer)

    bd = bd_ref[...]
    for r in range(NSA_WIDTH // LANE):
        sl = slice(r * LANE, (r + 1) * LANE)
        qt_ref[0, sl, :] = _group_rms(y_nsa[:, sl], bd, qg_ref[...]).T.astype(BF16)
    o = NSA_WIDTH
    kc_ref[0] = y_nsa[:, o:o + LANE].astype(BF16)
    vc_ref[0] = y_nsa[:, o + LANE:o + 2 * LANE].astype(BF16)
    ksa_ref[0, :, 0:LANE] = _group_rms(y_nsa[:, o + 2 * LANE:o + 3 * LANE], bd, ksg_ref[...]).astype(BF16)
    key = tile * tm + lax.broadcasted_iota(jnp.int32, (tm, LANE), 0)
    blk = lax.broadcasted_iota(jnp.int32, (tm, LANE), 1)
    ksa_ref[0, :, LANE:2 * LANE] = jnp.where(lax.shift_right_logical(key, 6) == blk, 1.0, 0.0).astype(BF16)
    vst_ref[0] = _values_t_aug(y_nsa[:, o + 3 * LANE:o + 4 * LANE])
    kw_ref[0] = _group_rms(y_nsa[:, o + 4 * LANE:o + 5 * LANE], bd, kwg_ref[...]).astype(BF16)
    vwt_ref[0] = _values_t_aug(y_nsa[:, o + 5 * LANE:o + 6 * LANE])
    y_gate = _dot(h, wp_ref[:, o_g + 4 * HG_WIDTH:o_g + 4 * HG_WIDTH + LANE])
    gt_ref[0] = jax.nn.sigmoid(y_gate).T
    y_h.append(hgrn_cols(2))
    next(mixer)
    y_h.append(hgrn_cols(3))
    for k in range(4):
        ph_ref[:, k * HG_WIDTH:(k + 1) * HG_WIDTH] = y_h[k]

    @pl.when(j == 0)
    def _():
        st_ref[...] = jnp.zeros(st_ref.shape, F32)


def _slab_rows(rows, n_steps):
    return next(r for r in range(BF16_SUBLANES, rows + 1, BF16_SUBLANES) if rows % r == 0 and rows // r <= n_steps)


def _in_proj(x, g, wt, bd, qg, ksg, kwg, lb_logits, o_gain, to_cast, tm=256):
    b, t, d = x.shape
    n_t = t // tm
    slabs = [_slab_rows(a.shape[0], b * (n_t + 1)) for a in to_cast]

    def slab_spec(a, r):
        return pl.BlockSpec((r, a.shape[1]), lambda i, j: (jnp.minimum(i * (n_t + 1) + j, a.shape[0] // r - 1), 0))

    cast_specs = [slab_spec(a, r) for a, r in zip(to_cast, slabs)]
    nw = NSA_WIDTH + 6 * KV_WIDTH + 4 * HG_WIDTH + LANE
    rows = lambda i, j: (i, jnp.minimum(j, n_t - 1), 0)
    cols = lambda i, j: (i, 0, jnp.minimum(j, n_t - 1))
    prev = lambda i, j: (i, jnp.maximum(j - 1, 0), 0)
    const = lambda i, j: (0, 0)
    outs = [
        (jax.ShapeDtypeStruct((b, NSA_WIDTH, t), BF16), pl.BlockSpec((1, NSA_WIDTH, tm), cols)),
        (jax.ShapeDtypeStruct((b, t, LANE), BF16), pl.BlockSpec((1, tm, LANE), rows)),
        (jax.ShapeDtypeStruct((b, t, LANE), BF16), pl.BlockSpec((1, tm, LANE), rows)),
        (jax.ShapeDtypeStruct((b, t, 2 * LANE), BF16), pl.BlockSpec((1, tm, 2 * LANE), rows)),
        (jax.ShapeDtypeStruct((b, 2 * V_AUG, t), BF16), pl.BlockSpec((1, 2 * V_AUG, tm), cols)),
        (jax.ShapeDtypeStruct((b, t, LANE), BF16), pl.BlockSpec((1, tm, LANE), rows)),
        (jax.ShapeDtypeStruct((b, 2 * V_AUG, t), BF16), pl.BlockSpec((1, 2 * V_AUG, tm), cols)),
        (jax.ShapeDtypeStruct((b, LANE, t), F32), pl.BlockSpec((1, LANE, tm), cols)),
        (jax.ShapeDtypeStruct((b, t, HG_WIDTH), BF16), pl.BlockSpec((1, tm, HG_WIDTH), prev)),
    ]
    outs += [(jax.ShapeDtypeStruct(a.shape, BF16), s) for a, s in zip(to_cast, cast_specs)]
    return pl.pallas_call(
        functools.partial(_in_proj_kernel, n_cast=len(to_cast)),
        grid=(b, n_t + 1),
        in_specs=[
            pl.BlockSpec((1, tm, d), rows),
            pl.BlockSpec((1, d), const),
            pl.BlockSpec(wt.shape, const, pipeline_mode=pl.Buffered(1)),
            pl.BlockSpec((LANE, LANE), const),
            pl.BlockSpec((1, LANE), const),
            pl.BlockSpec((1, LANE), const),
            pl.BlockSpec((1, LANE), const),
            pl.BlockSpec(lb_logits.shape, const),
            pl.BlockSpec((1, HGRN_DIM), const),
        ] + cast_specs,
        out_specs=[s for _, s in outs],
        out_shape=[s for s, _ in outs],
        scratch_shapes=[
            pltpu.VMEM((d, nw), BF16),
            pltpu.VMEM((tm, 4 * HG_WIDTH), F32),
            pltpu.VMEM((HGRN_HEADS, HGRN_DIM, HGRN_DIM), F32),
        ],
        compiler_params=pltpu.CompilerParams(
            dimension_semantics=("arbitrary", "arbitrary"), vmem_limit_bytes=VMEM_LIMIT),
        name="in_proj",
    )(x, g, wt, bd, qg, ksg, kwg, lb_logits, o_gain, *to_cast)


def _compress_kernel(xk_ref, xv_ref, wklo_ref, wkhi_ref, wk2_ref, pk_ref,
                     wvlo_ref, wvhi_ref, wv2_ref, pv_ref, bd_ref, kg_ref, kc_ref, vct_ref):
    ns = xk_ref.shape[1]

    def mlp(x_ref, wlo_ref, whi_ref, w2_ref, p_ref):
        xb = x_ref[0]
        a = _dot(xb, wlo_ref[...])
        b = _dot(xb, whi_ref[...])
        h = a + pltpu.roll(b, ns - 1, axis=0) + p_ref[...]
        return _dot(jax.nn.gelu(h).astype(BF16), w2_ref[...])

    kc = mlp(xk_ref, wklo_ref, wkhi_ref, wk2_ref, pk_ref)
    kc_ref[0] = _group_rms(kc, bd_ref[...], kg_ref[...]).astype(BF16)
    vct_ref[0] = _values_t_aug(mlp(xv_ref, wvlo_ref, wvhi_ref, wv2_ref, pv_ref))


def _compress(xk, xv, wk, wv, bd, kg):
    b, ns, wd = xk.shape
    const2 = lambda i: (0, 0)
    bat = lambda i: (i, 0, 0)
    wspecs = [pl.BlockSpec(w.shape, const2) for w in wk] + [pl.BlockSpec(w.shape, const2) for w in wv]
    return pl.pallas_call(
        _compress_kernel,
        grid=(b,),
        in_specs=[pl.BlockSpec((1, ns, wd), bat), pl.BlockSpec((1, ns, wd), bat)] + wspecs + [
            pl.BlockSpec((LANE, LANE), const2), pl.BlockSpec((1, LANE), const2)],
        out_specs=[pl.BlockSpec((1, ns, LANE), bat), pl.BlockSpec((1, 2 * V_AUG, ns), bat)],
        out_shape=[jax.ShapeDtypeStruct((b, ns, LANE), BF16), jax.ShapeDtypeStruct((b, 2 * V_AUG, ns), BF16)],
        compiler_params=pltpu.CompilerParams(
            dimension_semantics=("arbitrary",), vmem_limit_bytes=VMEM_LIMIT),
        name="compress",
    )(xk, xv, *wk, *wv, bd, kg)


def _topk_bias_t(imp_ts, s0):
    nj, nq = imp_ts[0].shape
    jidx = lax.broadcasted_iota(jnp.int32, (nj, nq), 0)
    tq = s0 + lax.broadcasted_iota(jnp.int32, (nj, nq), 1)
    jcur = lax.shift_right_logical(tq, 6)
    forced = (jidx == 0) | (jidx == jcur) | (jidx == jcur - 1)
    future = jidx > jcur
    vs = [jnp.where(forced, -2.0, jnp.where(future, -1.0, imp_t)) for imp_t in imp_ts]
    jf = jidx.astype(F32)
    for _ in range(N_SELECT - 3):
        for g in range(len(vs)):
            m = jnp.max(vs[g], axis=0, keepdims=True)
            jm = jnp.min(jnp.where(vs[g] == m, jf, float(nj)), axis=0, keepdims=True)
            vs[g] = jnp.where(jf == jm, -2.0, vs[g])
    return [jnp.where(future, NEG, jnp.where(v < -1.5, 0.0, NEG)) for v in vs]


def _nsa_kernel(qt_ref, qtn_ref, ksa_ref, vst_ref, kw_ref, vwt_ref, kc_ref, vct_ref, gt_ref, gtn_ref, ovt_ref, o_ref,
                w_ref, wn_ref, bias_ref, cmp_ref, sc_ref, sw_ref, ss_ref, mx_ref, e_ref, ew_ref,
                m_ref, acc_ref, out_ref):
    qb = pl.program_id(1)
    s0 = qb * Q_BLOCK
    nc = kc_ref.shape[1]
    two = 2 * Q_BLOCK
    lane_q = lax.broadcasted_iota(jnp.int32, (1, two), 1) & (Q_BLOCK - 1)
    t_lane = s0 + lane_q
    gt = gt_ref[0]

    def stationary_q(q_ref, dst):
        zero = jnp.zeros((HEAD_DIM, Q_BLOCK), BF16)
        for p in range(N_PAIRS):
            g = (2 * p) // NSA_REP
            halves = []
            for h in (2 * p, 2 * p + 1):
                qh = q_ref[0, h * HEAD_DIM:(h + 1) * HEAD_DIM, :]
                halves.append(jnp.concatenate([qh, zero] if g == 0 else [zero, qh], axis=0))
            dst[p, 0:LANE, :] = jnp.concatenate(halves, axis=1)

    def store_scores(dst, s, valid):
        if valid is not None:
            s = jnp.where(valid, s, NEG)
        dst[...] = s
        return jnp.max(s, axis=0, keepdims=True)

    def exp_tiles(load, n_tiles, m, e_dst):
        for c in range(n_tiles):
            rows = slice(c * KEY_TILE, (c + 1) * KEY_TILE)
            e_dst[rows, :] = jnp.exp2(load(rows) - m).astype(BF16)

    def values_t(ref, g, cols):
        return ref[0, g * V_AUG:(g + 1) * V_AUG, cols]

    def emit(dst, gates, p, branch, o_aug, first, guard=None):
        inv = 1.0 / o_aug[HEAD_DIM:HEAD_DIM + 1, :]
        if guard is not None:
            inv = jnp.where(guard, inv, 0.0)
        for hh in range(2):
            h = 2 * p + hh
            ls = slice(hh * Q_BLOCK, (hh + 1) * Q_BLOCK)
            o = o_aug[0:HEAD_DIM, ls] * (inv[:, ls] * gates[3 * h + branch:3 * h + branch + 1, :])
            rs = slice(h * HEAD_DIM, (h + 1) * HEAD_DIM)
            if first:
                dst[rs, :] = o
            else:
                dst[rs, :] += o
        return inv

    def key_rows(n):
        return lax.broadcasted_iota(jnp.int32, (n, two), 0)

    def cmp_scores(q_ref, s0x):
        stationary_q(q_ref, wn_ref)
        kc = kc_ref[0]
        valid = key_rows(nc) <= lax.shift_right_arithmetic(s0x + lane_q - (CMP_LEN - 1), 4)
        return [store_scores(sc_ref.at[p], _dot(kc, wn_ref[p]), valid) for p in range(N_PAIRS)]

    def cmp_finish(gates, m_cmp):
        imp_t = [jnp.zeros((LANE, Q_BLOCK), F32) for _ in range(NSA_GROUPS)]
        for p in range(N_PAIRS):
            exp_tiles(lambda rows: sc_ref[p, rows, :], nc // KEY_TILE, m_cmp[p], e_ref.at[p])
        for p in range(N_PAIRS):
            g = (2 * p) // NSA_REP
            inv = emit(cmp_ref, gates, p, 0, _dot(values_t(vct_ref, g, slice(None)), e_ref[p, 0:nc, :]), True,
                       guard=m_cmp[p] > 0.5 * NEG)
            imp_p = _dot(ovt_ref[...], e_ref[p, 0:nc, :]) * inv
            imp_t[g] = imp_t[g] + imp_p[:, 0:Q_BLOCK] + imp_p[:, Q_BLOCK:two]
        return imp_t

    def select(imp_t, s0x):
        for g, bias_t in enumerate(_topk_bias_t(imp_t, s0x)):
            bias_ref[g] = bias_t.astype(BF16)

    @pl.when(qb == 0)
    def _():
        select(cmp_finish(gt, cmp_scores(qt_ref, s0)), s0)

    w0 = pl.multiple_of(jnp.maximum(s0 - WINDOW, 0), KEY_TILE)
    wlen = WINDOW + Q_BLOCK
    kwin = kw_ref[0, pl.ds(w0, wlen), :]
    vwin = [values_t(vwt_ref, g, pl.ds(w0, wlen)) for g in range(NSA_GROUPS)]

    stationary_q(qt_ref, w_ref)
    for p in range(N_PAIRS):
        bias_t = bias_ref[(2 * p) // NSA_REP]
        w_ref[p, LANE:2 * LANE, :] = jnp.concatenate([bias_t, bias_t], axis=1)
    out_ref[...] = cmp_ref[...]

    imp_t = cmp_finish(gtn_ref[0], cmp_scores(qtn_ref, s0 + Q_BLOCK))
    diff = (t_lane - w0) - key_rows(wlen)
    win_valid = lax.shift_right_arithmetic(diff, WINDOW.bit_length() - 1) == 0
    m_win = [store_scores(sw_ref.at[p], _dot(kwin, w_ref[p, 0:LANE, :]), win_valid) for p in range(N_PAIRS)]
    for p in range(N_PAIRS):
        exp_tiles(lambda rows: sw_ref[p, rows, :], wlen // KEY_TILE, m_win[p], ew_ref.at[p])
    for p in range(N_PAIRS):
        emit(out_ref, gt, p, 2, _dot(vwin[(2 * p) // NSA_REP], ew_ref[p]), False)

    m_ref[...] = jnp.full(m_ref.shape, NEG, F32)
    acc_ref[...] = jnp.zeros(acc_ref.shape, F32)

    def sel_scores(u, buf, p, causal):
        k0 = pl.multiple_of(u * KEY_UNIT, KEY_UNIT)
        ku = ksa_ref[0, pl.ds(k0, KEY_UNIT), :]
        valid = (k0 + key_rows(KEY_UNIT)) <= t_lane if causal else None
        mx_ref[buf, p] = store_scores(ss_ref.at[buf, p], _dot(ku, w_ref[p]), valid)

    def sel_update(u, buf, p):
        k0 = pl.multiple_of(u * KEY_UNIT, KEY_UNIT)
        g = (2 * p) // NSA_REP
        m_old = m_ref[p]
        m = jnp.maximum(m_old, mx_ref[buf, p])
        exp_tiles(lambda rows: ss_ref[buf, p, rows, :], KEY_UNIT // KEY_TILE, m, e_ref.at[p])
        m_ref[p] = m
        acc_ref[p] = jnp.exp2(m_old - m) * acc_ref[p] + _dot(
            values_t(vst_ref, g, pl.ds(k0, KEY_UNIT)), e_ref[p, 0:KEY_UNIT, :])

    def sel_step(u, buf, next_causal):
        for p in range(N_PAIRS):
            sel_scores(u + 1, 1 - buf, p, next_causal)
            sel_update(u, buf, p)

    def sel_last(u, buf):
        for p in range(N_PAIRS):
            sel_update(u, buf, p)

    n_before = qb // (KEY_UNIT // Q_BLOCK)
    n_trips = jnp.maximum(n_before - 1, 0) // 2
    rest = 2 * n_trips

    for p in range(N_PAIRS):
        sel_scores(0, 0, p, False)
    select(imp_t, s0 + Q_BLOCK)

    def body(i, carry):
        sel_step(2 * i, 0, False)
        sel_step(2 * i + 1, 1, False)
        return carry

    def body2(i, carry):
        body(2 * i, carry)
        return body(2 * i + 1, carry)

    lax.fori_loop(0, n_trips // 2, body2, 0)
    lax.fori_loop(2 * (n_trips // 2), n_trips, body, 0)

    @pl.when(n_before == 0)
    def _():
        for p in range(N_PAIRS):
            sel_scores(0, 0, p, True)
        sel_last(0, 0)

    @pl.when(n_before == rest + 1)
    def _():
        sel_step(rest, 0, True)
        sel_last(rest + 1, 1)

    @pl.when(n_before == rest + 2)
    def _():
        sel_step(rest, 0, False)
        sel_step(rest + 1, 1, True)
        sel_last(rest + 2, 0)

    for p in range(N_PAIRS):
        emit(out_ref, gt, p, 1, acc_ref[p], False)

    o_ref[0] = out_ref[...].T.astype(BF16)


def _nsa(qt, ksa, vst, kw, vwt, kc, vct, gt, ovt):
    b, _, t = qt.shape
    nc = kc.shape[1]
    n_qb = t // Q_BLOCK
    qcol = lambda i, j: (i, 0, j)
    qnext = lambda i, j: (i, 0, jnp.minimum(j + 1, n_qb - 1))
    full = lambda i, j: (i, 0, 0)
    const = lambda i, j: (0, 0)
    return pl.pallas_call(
        _nsa_kernel,
        grid=(b, n_qb),
        in_specs=[
            pl.BlockSpec((1, NSA_WIDTH, Q_BLOCK), qcol),
            pl.BlockSpec((1, NSA_WIDTH, Q_BLOCK), qnext),
            pl.BlockSpec((1, t, 2 * LANE), full),
            pl.BlockSpec((1, 2 * V_AUG, t), full),
            pl.BlockSpec((1, t, LANE), full),
            pl.BlockSpec((1, 2 * V_AUG, t), full),
            pl.BlockSpec((1, nc, LANE), full),
            pl.BlockSpec((1, 2 * V_AUG, nc), full),
            pl.BlockSpec((1, LANE, Q_BLOCK), qcol),
            pl.BlockSpec((1, LANE, Q_BLOCK), qnext),
            pl.BlockSpec(ovt.shape, const),
        ],
        out_specs=pl.BlockSpec((1, Q_BLOCK, NSA_WIDTH), lambda i, j: (i, j, 0)),
        out_shape=jax.ShapeDtypeStruct((b, t, NSA_WIDTH), BF16),
        scratch_shapes=[
            pltpu.VMEM((N_PAIRS, 2 * LANE, 2 * Q_BLOCK), BF16),
            pltpu.VMEM((N_PAIRS, LANE, 2 * Q_BLOCK), BF16),
            pltpu.VMEM((NSA_GROUPS, LANE, Q_BLOCK), BF16),
            pltpu.VMEM((NSA_WIDTH, Q_BLOCK), F32),
            pltpu.VMEM((N_PAIRS, nc, 2 * Q_BLOCK), F32),
            pltpu.VMEM((N_PAIRS, WINDOW + Q_BLOCK, 2 * Q_BLOCK), F32),
            pltpu.VMEM((2, N_PAIRS, KEY_UNIT, 2 * Q_BLOCK), F32),
            pltpu.VMEM((2, N_PAIRS, 1, 2 * Q_BLOCK), F32),
            pltpu.VMEM((N_PAIRS, max(nc, KEY_UNIT), 2 * Q_BLOCK), BF16),
            pltpu.VMEM((N_PAIRS, WINDOW + Q_BLOCK, 2 * Q_BLOCK), BF16),
            pltpu.VMEM((N_PAIRS, 1, 2 * Q_BLOCK), F32),
            pltpu.VMEM((N_PAIRS, V_AUG, 2 * Q_BLOCK), F32),
            pltpu.VMEM((NSA_WIDTH, Q_BLOCK), F32),
        ],
        compiler_params=pltpu.CompilerParams(
            dimension_semantics=("arbitrary", "arbitrary"), vmem_limit_bytes=VMEM_LIMIT),
        name="nsa",
    )(qt, qt, ksa, vst, kw, vwt, kc, vct, gt, gt, ovt)


def _ff_chunks(d_ff, width):
    return tuple((s, min(width, d_ff - s)) for s in range(0, d_ff, width))


def _out_ffn_kernel(x_ref, on_ref, oh_ref, wo_ref, g_ref, wgu_ref, wd_ref, o_ref, *, chunks):
    d_ff = wd_ref.shape[0]
    x1 = (x_ref[...] + _dot(on_ref[...], wo_ref[0:NSA_WIDTH, :])
          + _dot(oh_ref[...], wo_ref[NSA_WIDTH:NSA_WIDTH + HG_WIDTH, :]))
    ms = jnp.mean(x1 * x1, axis=-1, keepdims=True)
    h = (x1 * lax.rsqrt(ms + RMS_EPS) * g_ref[...]).astype(BF16)
    o_ref[...] = x1
    for s, n in chunks:
        gate = _dot(h, wgu_ref[:, s:s + n])
        up = _dot(h, wgu_ref[:, d_ff + s:d_ff + s + n])
        act = (jax.nn.silu(gate) * up).astype(BF16)
        o_ref[...] += _dot(act, wd_ref[s:s + n, :])


def _out_ffn(x2, o_nsa, o_hg, wo, g, wgu, wd, tm=512, ff_width=4 * MXU_WIDTH):
    n, d = x2.shape
    d_ff = wd.shape[0]
    row = lambda i: (i, 0)
    const = lambda i: (0, 0)
    resident = lambda a: pl.BlockSpec(a.shape, const, pipeline_mode=pl.Buffered(1))
    return pl.pallas_call(
        functools.partial(_out_ffn_kernel, chunks=_ff_chunks(d_ff, ff_width)),
        grid=(n // tm,),
        in_specs=[
            pl.BlockSpec((tm, d), row),
            pl.BlockSpec((tm, NSA_WIDTH), row),
            pl.BlockSpec((tm, HG_WIDTH), row),
            resident(wo),
            pl.BlockSpec((1, d), const),
            resident(wgu),
            resident(wd),
        ],
        out_specs=pl.BlockSpec((tm, d), row),
        out_shape=jax.ShapeDtypeStruct((n, d), F32),
        compiler_params=pltpu.CompilerParams(
            dimension_semantics=("arbitrary",), vmem_limit_bytes=VMEM_LIMIT),
        name="out_ffn",
    )(x2, o_nsa, o_hg, wo, g, wgu, wd)


def _expand_cmp_weights(pos, w1, w2):
    eye = jnp.eye(NSA_GROUPS, dtype=F32)
    w1r = w1.reshape(CMP_LEN, HEAD_DIM, CMP_HIDDEN)

    def lift(wpart):
        return jnp.einsum('ldc,gk->lgdkc', wpart, eye).reshape(
            CMP_STRIDE * NSA_GROUPS * HEAD_DIM, NSA_GROUPS * CMP_HIDDEN)

    wlo = lift(w1r[:CMP_STRIDE]).astype(BF16)
    whi = lift(w1r[CMP_STRIDE:]).astype(BF16)
    w2x = jnp.einsum('cd,gk->gckd', w2, eye).reshape(NSA_GROUPS * CMP_HIDDEN, NSA_GROUPS * HEAD_DIM).astype(BF16)
    pbias = jnp.tile(pos.reshape(1, CMP_LEN * HEAD_DIM) @ w1, (1, NSA_GROUPS))
    return wlo, whi, w2x, pbias


def _mixers(x, norm_mix, w_in, q_norm, k_norm, cmp_pos_k, cmp_pos_v, cmp_k_w1, cmp_k_w2, cmp_v_w1, cmp_v_w2,
            hgrn_lb_logits, hgrn_o_norm, to_cast=()):
    b, t, d = x.shape
    depth = norm_mix.shape[0]
    assert depth == 1 and hgrn_lb_logits.shape[0] == 2
    assert t % KEY_UNIT == 0 and t >= WINDOW + Q_BLOCK and t // SLC_LEN <= LANE
    assert (t // CMP_STRIDE) % KEY_TILE == 0
    l = 0

    bd = jnp.asarray(np.kron(np.eye(LANE // HEAD_DIM), np.ones((HEAD_DIM, HEAD_DIM))), BF16)
    tile2 = lambda v: jnp.tile(v.reshape(1, HEAD_DIM), (1, LANE // HEAD_DIM)).astype(F32)
    qg = tile2(q_norm[l]) * (HEAD_DIM ** -0.5 * math.log2(math.e))

    qt, kc_raw, vc_raw, ksa, vst, kw, vwt, gt, o_hg, *casted = _in_proj(
        x, norm_mix[l].reshape(1, d), w_in[l].T, bd, qg, tile2(k_norm[l, 1]), tile2(k_norm[l, 2]),
        hgrn_lb_logits, hgrn_o_norm[l].reshape(1, HGRN_DIM), list(to_cast))

    ns = t // CMP_STRIDE
    seg_w = CMP_STRIDE * KV_WIDTH
    kc, vct = _compress(
        kc_raw.reshape(b, ns, seg_w), vc_raw.reshape(b, ns, seg_w),
        _expand_cmp_weights(cmp_pos_k[l], cmp_k_w1[l], cmp_k_w2[l]),
        _expand_cmp_weights(cmp_pos_v[l], cmp_v_w1[l], cmp_v_w2[l]),
        bd, tile2(k_norm[l, 0]))

    cs = np.arange(ns)[None, :] * CMP_STRIDE
    ss = np.arange(LANE)[:, None] * SLC_LEN
    ovt = jnp.asarray(((cs < ss + SLC_LEN) & (cs + CMP_LEN > ss)).astype(np.float32), BF16)

    o_nsa = _nsa(qt, ksa, vst, kw, vwt, kc, vct, gt, ovt)
    return o_nsa, o_hg, casted


def kernel(x, norm_mix, w_in, q_norm, k_norm, cmp_pos_k, cmp_pos_v, cmp_k_w1, cmp_k_w2, cmp_v_w1, cmp_v_w2,
           hgrn_lb_logits, hgrn_o_norm, w_out, norm_ffn, w_gate_up, w_down):
    b, t, d = x.shape
    l = 0
    o_nsa, o_hg, (wo, wgu, wd) = _mixers(
        x, norm_mix, w_in, q_norm, k_norm, cmp_pos_k, cmp_pos_v, cmp_k_w1, cmp_k_w2, cmp_v_w1, cmp_v_w2,
        hgrn_lb_logits, hgrn_o_norm, to_cast=(w_out[l], w_gate_up[l], w_down[l]))
    out = _out_ffn(
        x.reshape(b * t, d), o_nsa.reshape(b * t, NSA_WIDTH), o_hg.reshape(b * t, HG_WIDTH),
        wo, norm_ffn[l].reshape(1, d), wgu, wd)
    return out.reshape(b, t, d)
```

```python
import functools
import math

import numpy as np
import jax
import jax.numpy as jnp
from jax import lax
from jax.experimental import pallas as pl
from jax.experimental.pallas import tpu as pltpu

F32 = jnp.float32
BF16 = jnp.bfloat16

LANE = 128
BF16_SUBLANES = 16
MXU_WIDTH = 256

NSA_HEADS = 8
NSA_GROUPS = 2
NSA_REP = NSA_HEADS // NSA_GROUPS
HEAD_DIM = 64
CMP_LEN = 32
CMP_STRIDE = 16
CMP_HIDDEN = 128
SLC_LEN = 64
N_SELECT = 16
WINDOW = 512
Q_BLOCK = 128
HGRN_HEADS = 4
HGRN_DIM = 128
HGRN_CHUNK = 64
HGRN_SUB = 16
RMS_EPS = 1e-6
NEG = -1e30

NSA_WIDTH = NSA_HEADS * HEAD_DIM
KV_WIDTH = NSA_GROUPS * HEAD_DIM
HG_WIDTH = HGRN_HEADS * HGRN_DIM
N_GATES = 3 * NSA_HEADS

KEY_TILE = 128
KEY_UNIT = 4 * KEY_TILE
N_PAIRS = NSA_HEADS // 2
V_AUG = HEAD_DIM + 16
VMEM_LIMIT = 56 * 1024 * 1024


def _dot(a, b):
    return jnp.dot(a, b, preferred_element_type=F32)


def _dot_t(a, b):
    return lax.dot_general(a, b, (((1,), (1,)), ((), ())), preferred_element_type=F32)


def _group_rms(y, bd, gain):
    ss = _dot((y * y).astype(BF16), bd)
    return y * lax.rsqrt(ss * (1.0 / HEAD_DIM) + RMS_EPS) * gain


def _values_t_aug(v):
    vt = v.T
    ones = jnp.ones((V_AUG - HEAD_DIM, v.shape[0]), F32)
    return jnp.concatenate([vt[0:HEAD_DIM], ones, vt[HEAD_DIM:2 * HEAD_DIM], ones], axis=0).astype(BF16)


def _hgrn_stages(ph_ref, lb_ref, og_ref, o_ref, st_ref, chunks):
    c_len = HGRN_CHUNK
    n_sub = c_len // HGRN_SUB
    lg = lb_ref[...]
    lmax = jnp.max(lg, axis=0, keepdims=True)
    le = jnp.exp(lg - lmax)
    lb_all = le[0:1] / jnp.sum(le, axis=0, keepdims=True)

    ri = lax.broadcasted_iota(jnp.int32, (c_len, c_len), 0)
    ci = lax.broadcasted_iota(jnp.int32, (c_len, c_len), 1)
    tri = (ci <= ri).astype(BF16)
    causal = ci <= ri
    gain = og_ref[...]
    heads = [slice(h * HGRN_DIM, (h + 1) * HGRN_DIM) for h in range(HGRN_HEADS)]
    rows = [slice(c * c_len, (c + 1) * c_len) for c in range(chunks)]

    pre = []
    for rs in rows:
        qv = jax.nn.silu(ph_ref[rs, 0:HG_WIDTH])
        f = lb_all + (1.0 - lb_all) * jax.nn.sigmoid(ph_ref[rs, HG_WIDTH:2 * HG_WIDTH])
        logf = jnp.log(f)
        g1 = logf.astype(BF16)
        g2 = (logf - g1.astype(F32)).astype(BF16)
        pre.append((qv, 1.0 - f, g1, g2))
    his = [ph_ref[rs, 2 * HG_WIDTH:3 * HG_WIDTH] for rs in rows]
    gates = [jax.nn.silu(ph_ref[rs, 3 * HG_WIDTH:4 * HG_WIDTH]) for rs in rows]
    yield

    bcums = [_dot(tri, g1) + _dot(tri, g2) for _, _, g1, g2 in pre]
    yield

    mid = []
    for c in range(chunks):
        qv, kv = pre[c][0], pre[c][1]
        bcum = bcums[c]
        b_last = bcum[c_len - 1:c_len]
        starts = [jnp.zeros((1, HG_WIDTH), F32)] + [bcum[i * HGRN_SUB - 1:i * HGRN_SUB] for i in range(1, n_sub)]
        ends = starts[1:] + [b_last]

        def per_sub(vals):
            return jnp.concatenate([jnp.broadcast_to(r, (HGRN_SUB, HG_WIDTH)) for r in vals], axis=0)

        start_full = per_sub(starts)
        qd = qv * jnp.exp(bcum - start_full)
        kb = kv * jnp.exp(per_sub(ends) - bcum)
        kdiag = kv * jnp.exp(start_full - bcum)
        qs = (qd * per_sub([jnp.exp(s) for s in starts])).astype(BF16)
        kdec = (kb * per_sub([jnp.exp(b_last - e) for e in ends])).astype(BF16)
        kds = []
        for i in range(n_sub):
            blocks = [kb[j * HGRN_SUB:(j + 1) * HGRN_SUB] * jnp.exp(starts[i] - ends[j]) for j in range(i - 1)]
            if i > 0:
                blocks.append(kb[(i - 1) * HGRN_SUB:i * HGRN_SUB])
            blocks.append(kdiag[i * HGRN_SUB:(i + 1) * HGRN_SUB])
            if i + 1 < n_sub:
                blocks.append(jnp.zeros(((n_sub - 1 - i) * HGRN_SUB, HG_WIDTH), F32))
            kds.append(jnp.concatenate(blocks, axis=0).astype(BF16))
        mid.append((qd.astype(BF16), kds, qs, kdec, jnp.exp(b_last)))
    attn = {}
    upd = {}
    for c in range(chunks):
        qd, kds, qs, kdec, dec = mid[c]
        for h, hs in enumerate(heads):
            both = _dot_t(qd[:, hs], jnp.concatenate([kd[:, hs] for kd in kds], axis=0))
            own = jnp.concatenate([both[i * HGRN_SUB:(i + 1) * HGRN_SUB, i * c_len:(i + 1) * c_len]
                                   for i in range(n_sub)], axis=0)
            attn[c, h] = jnp.where(causal, own, 0.0).astype(BF16)
            upd[c, h] = _dot(his[c][:, hs].T.astype(BF16), kdec[:, hs])
    yield

    state = {}
    for h, hs in enumerate(heads):
        st = st_ref[h]
        for c in range(chunks):
            state[c, h] = st.astype(BF16)
            st = st * mid[c][4][:, hs] + upd[c, h]
        st_ref[h] = st
    for c, rs in enumerate(rows):
        qs = mid[c][2]
        for h, hs in enumerate(heads):
            o = _dot(attn[c, h], his[c][:, hs].astype(BF16)) + _dot_t(qs[:, hs], state[c, h])
            ms = jnp.mean(o * o, axis=-1, keepdims=True)
            o = o * lax.rsqrt(ms + RMS_EPS) * gain * gates[c][:, hs]
            o_ref[0, rs, hs] = o.astype(BF16)
    yield


def _in_proj_kernel(x_ref, g_ref, wt_ref, bd_ref, qg_ref, ksg_ref, kwg_ref, lb_ref, og_ref, *refs, n_cast):
    cast_in, refs = refs[:n_cast], refs[n_cast:]
    qt_ref, kc_ref, vc_ref, ksa_ref, vst_ref, kw_ref, vwt_ref, gt_ref, ohg_ref = refs[:9]
    cast_out, (wp_ref, ph_ref, st_ref) = refs[9:9 + n_cast], refs[9 + n_cast:]
    for src_ref, dst_ref in zip(cast_in, cast_out):
        dst_ref[...] = src_ref[...].astype(BF16)

    tm = x_ref.shape[1]
    d = x_ref.shape[2]
    j = pl.program_id(1)
    tile = jnp.minimum(j, pl.num_programs(1) - 2)
    o_g = NSA_WIDTH + 6 * KV_WIDTH

    @pl.when((pl.program_id(0) == 0) & (j == 0))
    def _():
        o_h = o_g + N_GATES
        sources = [c * LANE for c in range(o_g // LANE)] + [o_h + c * LANE for c in range(4 * HG_WIDTH // LANE)]

        def copy_rows(i, carry):
            rs = pl.ds(pl.multiple_of(i * LANE, LANE), LANE)
            for c, src in enumerate(sources):
                wp_ref[rs, c * LANE:(c + 1) * LANE] = wt_ref[src:src + LANE, rs].T.astype(BF16)
            tail = jnp.concatenate([wt_ref[o_g:o_h, rs], jnp.zeros((LANE - N_GATES, LANE), F32)], axis=0)
            wp_ref[rs, o_g + 4 * HG_WIDTH:o_g + 4 * HG_WIDTH + LANE] = tail.T.astype(BF16)
            return carry

        lax.fori_loop(0, d // LANE, copy_rows, 0)
        ph_ref[...] = jnp.zeros(ph_ref.shape, F32)
        st_ref[...] = jnp.zeros(st_ref.shape, F32)

    x = x_ref[0]
    ms = jnp.mean(x * x, axis=-1, keepdims=True)
    h = (x * lax.rsqrt(ms + RMS_EPS) * g_ref[...]).astype(BF16)

    mixer = _hgrn_stages(ph_ref, lb_ref, og_ref, ohg_ref, st_ref, tm // HGRN_CHUNK)
    next(mixer)
    y_nsa = _dot(h, wp_ref[:, 0:o_g])
    next(mixer)

    def hgrn_cols(k):
        return _dot(h, wp_ref[:, o_g + k * HG_WIDTH:o_g + (k + 1) * HG_WIDTH])

    y_h = [hgrn_cols(0), hgrn_cols(1)]
    next(mixer)

    bd = bd_ref[...]
    for r in range(NSA_WIDTH // LANE):
        sl = slice(r * LANE, (r + 1) * LANE)
        qt_ref[0, sl, :] = _group_rms(y_nsa[:, sl], bd, qg_ref[...]).T.astype(BF16)
    o = NSA_WIDTH
    n_seg = tm // CMP_STRIDE
    r_out = lax.broadcasted_iota(jnp.int32, (tm, tm), 0)
    r_in = lax.broadcasted_iota(jnp.int32, (tm, tm), 1)
    perm = jnp.where(r_in == CMP_STRIDE * (r_out % n_seg) + r_out // n_seg, 1.0, 0.0).astype(BF16)
    raw = _dot(perm, y_nsa[:, o:o + 2 * LANE].astype(BF16))
    for i in range(CMP_STRIDE):
        kc_ref[0, :, i * LANE:(i + 1) * LANE] = raw[i * n_seg:(i + 1) * n_seg, 0:LANE].astype(BF16)
        vc_ref[0, :, i * LANE:(i + 1) * LANE] = raw[i * n_seg:(i + 1) * n_seg, LANE:2 * LANE].astype(BF16)
    ksa_ref[0, :, 0:LANE] = _group_rms(y_nsa[:, o + 2 * LANE:o + 3 * LANE], bd, ksg_ref[...]).astype(BF16)
    key = tile * tm + lax.broadcasted_iota(jnp.int32, (tm, LANE), 0)
    blk = lax.broadcasted_iota(jnp.int32, (tm, LANE), 1)
    ksa_ref[0, :, LANE:2 * LANE] = jnp.where(lax.shift_right_logical(key, 6) == blk, 1.0, 0.0).astype(BF16)
    vst_ref[0] = _values_t_aug(y_nsa[:, o + 3 * LANE:o + 4 * LANE])
    kw_ref[0] = _group_rms(y_nsa[:, o + 4 * LANE:o + 5 * LANE], bd, kwg_ref[...]).astype(BF16)
    vwt_ref[0] = _values_t_aug(y_nsa[:, o + 5 * LANE:o + 6 * LANE])
    y_gate = _dot(h, wp_ref[:, o_g + 4 * HG_WIDTH:o_g + 4 * HG_WIDTH + LANE])
    gt_ref[0] = jax.nn.sigmoid(y_gate).T
    y_h.append(hgrn_cols(2))
    next(mixer)
    y_h.append(hgrn_cols(3))
    for k in range(4):
        ph_ref[:, k * HG_WIDTH:(k + 1) * HG_WIDTH] = y_h[k]

    @pl.when(j == 0)
    def _():
        st_ref[...] = jnp.zeros(st_ref.shape, F32)


def _slab_rows(rows, n_steps):
    return next(r for r in range(BF16_SUBLANES, rows + 1, BF16_SUBLANES) if rows % r == 0 and rows // r <= n_steps)


def _in_proj(x, g, wt, bd, qg, ksg, kwg, lb_logits, o_gain, to_cast, tm=256):
    b, t, d = x.shape
    n_t = t // tm
    seg_w = CMP_STRIDE * KV_WIDTH
    slabs = [_slab_rows(a.shape[0], b * (n_t + 1)) for a in to_cast]

    def slab_spec(a, r):
        return pl.BlockSpec((r, a.shape[1]), lambda i, j: (jnp.minimum(i * (n_t + 1) + j, a.shape[0] // r - 1), 0))

    cast_specs = [slab_spec(a, r) for a, r in zip(to_cast, slabs)]
    nw = NSA_WIDTH + 6 * KV_WIDTH + 4 * HG_WIDTH + LANE
    rows = lambda i, j: (i, jnp.minimum(j, n_t - 1), 0)
    cols = lambda i, j: (i, 0, jnp.minimum(j, n_t - 1))
    prev = lambda i, j: (i, jnp.maximum(j - 1, 0), 0)
    const = lambda i, j: (0, 0)
    outs = [
        (jax.ShapeDtypeStruct((b, NSA_WIDTH, t), BF16), pl.BlockSpec((1, NSA_WIDTH, tm), cols)),
        (jax.ShapeDtypeStruct((b, t // CMP_STRIDE, seg_w), BF16), pl.BlockSpec((1, tm // CMP_STRIDE, seg_w), rows)),
        (jax.ShapeDtypeStruct((b, t // CMP_STRIDE, seg_w), BF16), pl.BlockSpec((1, tm // CMP_STRIDE, seg_w), rows)),
        (jax.ShapeDtypeStruct((b, t, 2 * LANE), BF16), pl.BlockSpec((1, tm, 2 * LANE), rows)),
        (jax.ShapeDtypeStruct((b, 2 * V_AUG, t), BF16), pl.BlockSpec((1, 2 * V_AUG, tm), cols)),
        (jax.ShapeDtypeStruct((b, t, LANE), BF16), pl.BlockSpec((1, tm, LANE), rows)),
        (jax.ShapeDtypeStruct((b, 2 * V_AUG, t), BF16), pl.BlockSpec((1, 2 * V_AUG, tm), cols)),
        (jax.ShapeDtypeStruct((b, LANE, t), F32), pl.BlockSpec((1, LANE, tm), cols)),
        (jax.ShapeDtypeStruct((b, t, HG_WIDTH), BF16), pl.BlockSpec((1, tm, HG_WIDTH), prev)),
    ]
    outs += [(jax.ShapeDtypeStruct(a.shape, BF16), s) for a, s in zip(to_cast, cast_specs)]
    return pl.pallas_call(
        functools.partial(_in_proj_kernel, n_cast=len(to_cast)),
        grid=(b, n_t + 1),
        in_specs=[
            pl.BlockSpec((1, tm, d), rows),
            pl.BlockSpec((1, d), const),
            pl.BlockSpec(wt.shape, const, pipeline_mode=pl.Buffered(1)),
            pl.BlockSpec((LANE, LANE), const),
            pl.BlockSpec((1, LANE), const),
            pl.BlockSpec((1, LANE), const),
            pl.BlockSpec((1, LANE), const),
            pl.BlockSpec(lb_logits.shape, const),
            pl.BlockSpec((1, HGRN_DIM), const),
        ] + cast_specs,
        out_specs=[s for _, s in outs],
        out_shape=[s for s, _ in outs],
        scratch_shapes=[
            pltpu.VMEM((d, nw), BF16),
            pltpu.VMEM((tm, 4 * HG_WIDTH), F32),
            pltpu.VMEM((HGRN_HEADS, HGRN_DIM, HGRN_DIM), F32),
        ],
        compiler_params=pltpu.CompilerParams(
            dimension_semantics=("arbitrary", "arbitrary"), vmem_limit_bytes=VMEM_LIMIT),
        name="in_proj",
    )(x, g, wt, bd, qg, ksg, kwg, lb_logits, o_gain, *to_cast)


def _compress_kernel(xk_ref, xv_ref, wk1_ref, wk2_ref, pk_ref, wv1_ref, wv2_ref, pv_ref, bd_ref, kg_ref,
                     kc_ref, vct_ref, lift_ref):
    ns = xk_ref.shape[1]
    zeros = jnp.zeros((HEAD_DIM, CMP_HIDDEN), BF16)

    def mlp(x_ref, w1_ref, w2_ref, p_ref, lifted):
        for half in range(CMP_LEN // CMP_STRIDE):
            for tok in range(CMP_STRIDE):
                r = (half * CMP_STRIDE + tok) * HEAD_DIM
                w = w1_ref[r:r + HEAD_DIM, :].astype(BF16)
                for g in range(NSA_GROUPS):
                    row = (tok * NSA_GROUPS + g) * HEAD_DIM
                    lifted[half, row:row + HEAD_DIM, :] = jnp.concatenate(
                        [w if k == g else zeros for k in range(NSA_GROUPS)], axis=1)
        xb = x_ref[0]
        a = _dot(xb, lifted[0])
        b = _dot(xb, lifted[1])
        h = a + pltpu.roll(b, ns - 1, axis=0) + p_ref[...]
        return _dot(jax.nn.gelu(h).astype(BF16), w2_ref[...])

    kc = mlp(xk_ref, wk1_ref, wk2_ref, pk_ref, lift_ref.at[0])
    kc_ref[0] = _group_rms(kc, bd_ref[...], kg_ref[...]).astype(BF16)
    vct_ref[0] = _values_t_aug(mlp(xv_ref, wv1_ref, wv2_ref, pv_ref, lift_ref.at[1]))


def _compress(xk, xv, wk, wv, bd, kg):
    b, ns, wd = xk.shape
    const2 = lambda i: (0, 0)
    bat = lambda i: (i, 0, 0)
    wspecs = [pl.BlockSpec(w.shape, const2) for w in wk] + [pl.BlockSpec(w.shape, const2) for w in wv]
    return pl.pallas_call(
        _compress_kernel,
        grid=(b,),
        in_specs=[pl.BlockSpec((1, ns, wd), bat), pl.BlockSpec((1, ns, wd), bat)] + wspecs + [
            pl.BlockSpec((LANE, LANE), const2), pl.BlockSpec((1, LANE), const2)],
        out_specs=[pl.BlockSpec((1, ns, LANE), bat), pl.BlockSpec((1, 2 * V_AUG, ns), bat)],
        out_shape=[jax.ShapeDtypeStruct((b, ns, LANE), BF16), jax.ShapeDtypeStruct((b, 2 * V_AUG, ns), BF16)],
        scratch_shapes=[pltpu.VMEM((2, CMP_LEN // CMP_STRIDE, wd, NSA_GROUPS * CMP_HIDDEN), BF16)],
        compiler_params=pltpu.CompilerParams(
            dimension_semantics=("arbitrary",), vmem_limit_bytes=VMEM_LIMIT),
        name="compress",
    )(xk, xv, *wk, *wv, bd, kg)


def _topk_bias_t(imp_ts, s0):
    nj, nq = imp_ts[0].shape
    jidx = lax.broadcasted_iota(jnp.int32, (nj, nq), 0)
    tq = s0 + lax.broadcasted_iota(jnp.int32, (nj, nq), 1)
    jcur = lax.shift_right_logical(tq, 6)
    forced = (jidx == 0) | (jidx == jcur) | (jidx == jcur - 1)
    future = jidx > jcur
    vs = [jnp.where(forced, -2.0, jnp.where(future, -1.0, imp_t)) for imp_t in imp_ts]
    jf = jidx.astype(F32)
    for _ in range(N_SELECT - 3):
        for g in range(len(vs)):
            m = jnp.max(vs[g], axis=0, keepdims=True)
            jm = jnp.min(jnp.where(vs[g] == m, jf, float(nj)), axis=0, keepdims=True)
            vs[g] = jnp.where(jf == jm, -2.0, vs[g])
    return [jnp.where(future, NEG, jnp.where(v < -1.5, 0.0, NEG)) for v in vs]


def _nsa_kernel(qt_ref, qtn_ref, ksa_ref, vst_ref, kw_ref, vwt_ref, kc_ref, vct_ref, gt_ref, gtn_ref, ovt_ref, o_ref,
                w_ref, wn_ref, bias_ref, cmp_ref, sc_ref, sw_ref, ss_ref, mx_ref, e_ref, ew_ref,
                m_ref, acc_ref, out_ref):
    qb = pl.program_id(1)
    s0 = qb * Q_BLOCK
    nc = kc_ref.shape[1]
    two = 2 * Q_BLOCK
    lane_q = lax.broadcasted_iota(jnp.int32, (1, two), 1) & (Q_BLOCK - 1)
    t_lane = s0 + lane_q
    gt = gt_ref[0]

    def stationary_q(q_ref, dst):
        zero = jnp.zeros((HEAD_DIM, Q_BLOCK), BF16)
        for p in range(N_PAIRS):
            g = (2 * p) // NSA_REP
            halves = []
            for h in (2 * p, 2 * p + 1):
                qh = q_ref[0, h * HEAD_DIM:(h + 1) * HEAD_DIM, :]
                halves.append(jnp.concatenate([qh, zero] if g == 0 else [zero, qh], axis=0))
            dst[p, 0:LANE, :] = jnp.concatenate(halves, axis=1)

    def store_scores(dst, s, valid):
        if valid is not None:
            s = jnp.where(valid, s, NEG)
        dst[...] = s
        return jnp.max(s, axis=0, keepdims=True)

    def exp_tiles(load, n_tiles, m, e_dst):
        for c in range(n_tiles):
            rows = slice(c * KEY_TILE, (c + 1) * KEY_TILE)
            e_dst[rows, :] = jnp.exp2(load(rows) - m).astype(BF16)

    def values_t(ref, g, cols):
        return ref[0, g * V_AUG:(g + 1) * V_AUG, cols]

    def emit(dst, gates, p, branch, o_aug, first, guard=None):
        inv = 1.0 / o_aug[HEAD_DIM:HEAD_DIM + 1, :]
        if guard is not None:
            inv = jnp.where(guard, inv, 0.0)
        for hh in range(2):
            h = 2 * p + hh
            ls = slice(hh * Q_BLOCK, (hh + 1) * Q_BLOCK)
            o = o_aug[0:HEAD_DIM, ls] * (inv[:, ls] * gates[3 * h + branch:3 * h + branch + 1, :])
            rs = slice(h * HEAD_DIM, (h + 1) * HEAD_DIM)
            if first:
                dst[rs, :] = o
            else:
                dst[rs, :] += o
        return inv

    def key_rows(n):
        return lax.broadcasted_iota(jnp.int32, (n, two), 0)

    def cmp_scores(q_ref, s0x):
        stationary_q(q_ref, wn_ref)
        kc = kc_ref[0]
        valid = key_rows(nc) <= lax.shift_right_arithmetic(s0x + lane_q - (CMP_LEN - 1), 4)
        return [store_scores(sc_ref.at[p], _dot(kc, wn_ref[p]), valid) for p in range(N_PAIRS)]

    def cmp_finish(gates, m_cmp):
        imp_t = [jnp.zeros((LANE, Q_BLOCK), F32) for _ in range(NSA_GROUPS)]
        for p in range(N_PAIRS):
            exp_tiles(lambda rows: sc_ref[p, rows, :], nc // KEY_TILE, m_cmp[p], e_ref.at[p])
        for p in range(N_PAIRS):
            g = (2 * p) // NSA_REP
            inv = emit(cmp_ref, gates, p, 0, _dot(values_t(vct_ref, g, slice(None)), e_ref[p, 0:nc, :]), True,
                       guard=m_cmp[p] > 0.5 * NEG)
            imp_p = _dot(ovt_ref[...], e_ref[p, 0:nc, :]) * inv
            imp_t[g] = imp_t[g] + imp_p[:, 0:Q_BLOCK] + imp_p[:, Q_BLOCK:two]
        return imp_t

    def select(imp_t, s0x):
        for g, bias_t in enumerate(_topk_bias_t(imp_t, s0x)):
            bias_ref[g] = bias_t.astype(BF16)

    @pl.when(qb == 0)
    def _():
        select(cmp_finish(gt, cmp_scores(qt_ref, s0)), s0)

    w0 = pl.multiple_of(jnp.maximum(s0 - WINDOW, 0), KEY_TILE)
    wlen = WINDOW + Q_BLOCK
    kwin = kw_ref[0, pl.ds(w0, wlen), :]
    vwin = [values_t(vwt_ref, g, pl.ds(w0, wlen)) for g in range(NSA_GROUPS)]

    stationary_q(qt_ref, w_ref)
    for p in range(N_PAIRS):
        bias_t = bias_ref[(2 * p) // NSA_REP]
        w_ref[p, LANE:2 * LANE, :] = jnp.concatenate([bias_t, bias_t], axis=1)
    out_ref[...] = cmp_ref[...]

    imp_t = cmp_finish(gtn_ref[0], cmp_scores(qtn_ref, s0 + Q_BLOCK))
    diff = (t_lane - w0) - key_rows(wlen)
    win_valid = lax.shift_right_arithmetic(diff, WINDOW.bit_length() - 1) == 0
    m_win = [store_scores(sw_ref.at[p], _dot(kwin, w_ref[p, 0:LANE, :]), win_valid) for p in range(N_PAIRS)]
    for p in range(N_PAIRS):
        exp_tiles(lambda rows: sw_ref[p, rows, :], wlen // KEY_TILE, m_win[p], ew_ref.at[p])
    for p in range(N_PAIRS):
        emit(out_ref, gt, p, 2, _dot(vwin[(2 * p) // NSA_REP], ew_ref[p]), False)

    m_ref[...] = jnp.full(m_ref.shape, NEG, F32)
    acc_ref[...] = jnp.zeros(acc_ref.shape, F32)

    def sel_scores(u, buf, p, causal):
        k0 = pl.multiple_of(u * KEY_UNIT, KEY_UNIT)
        ku = ksa_ref[0, pl.ds(k0, KEY_UNIT), :]
        valid = (k0 + key_rows(KEY_UNIT)) <= t_lane if causal else None
        mx_ref[buf, p] = store_scores(ss_ref.at[buf, p], _dot(ku, w_ref[p]), valid)

    def sel_update(u, buf, p):
        k0 = pl.multiple_of(u * KEY_UNIT, KEY_UNIT)
        g = (2 * p) // NSA_REP
        m_old = m_ref[p]
        m = jnp.maximum(m_old, mx_ref[buf, p])
        exp_tiles(lambda rows: ss_ref[buf, p, rows, :], KEY_UNIT // KEY_TILE, m, e_ref.at[p])
        m_ref[p] = m
        acc_ref[p] = jnp.exp2(m_old - m) * acc_ref[p] + _dot(
            values_t(vst_ref, g, pl.ds(k0, KEY_UNIT)), e_ref[p, 0:KEY_UNIT, :])

    def sel_step(u, buf, next_causal):
        for p in range(N_PAIRS):
            sel_scores(u + 1, 1 - buf, p, next_causal)
            sel_update(u, buf, p)

    def sel_last(u, buf):
        for p in range(N_PAIRS):
            sel_update(u, buf, p)

    n_before = qb // (KEY_UNIT // Q_BLOCK)
    n_trips = jnp.maximum(n_before - 1, 0) // 2
    rest = 2 * n_trips

    for p in range(N_PAIRS):
        sel_scores(0, 0, p, False)
    select(imp_t, s0 + Q_BLOCK)

    def body(i, carry):
        sel_step(2 * i, 0, False)
        sel_step(2 * i + 1, 1, False)
        return carry

    def body2(i, carry):
        body(2 * i, carry)
        return body(2 * i + 1, carry)

    lax.fori_loop(0, n_trips // 2, body2, 0)
    lax.fori_loop(2 * (n_trips // 2), n_trips, body, 0)

    @pl.when(n_before == 0)
    def _():
        for p in range(N_PAIRS):
            sel_scores(0, 0, p, True)
        sel_last(0, 0)

    @pl.when(n_before == rest + 1)
    def _():
        sel_step(rest, 0, True)
        sel_last(rest + 1, 1)

    @pl.when(n_before == rest + 2)
    def _():
        sel_step(rest, 0, False)
        sel_step(rest + 1, 1, True)
        sel_last(rest + 2, 0)

    for p in range(N_PAIRS):
        emit(out_ref, gt, p, 1, acc_ref[p], False)

    o_ref[0] = out_ref[...].T.astype(BF16)


def _nsa(qt, ksa, vst, kw, vwt, kc, vct, gt, ovt):
    b, _, t = qt.shape
    nc = kc.shape[1]
    n_qb = t // Q_BLOCK
    qcol = lambda i, j: (i, 0, j)
    qnext = lambda i, j: (i, 0, jnp.minimum(j + 1, n_qb - 1))
    full = lambda i, j: (i, 0, 0)
    const = lambda i, j: (0, 0)
    return pl.pallas_call(
        _nsa_kernel,
        grid=(b, n_qb),
        in_specs=[
            pl.BlockSpec((1, NSA_WIDTH, Q_BLOCK), qcol),
            pl.BlockSpec((1, NSA_WIDTH, Q_BLOCK), qnext),
            pl.BlockSpec((1, t, 2 * LANE), full),
            pl.BlockSpec((1, 2 * V_AUG, t), full),
            pl.BlockSpec((1, t, LANE), full),
            pl.BlockSpec((1, 2 * V_AUG, t), full),
            pl.BlockSpec((1, nc, LANE), full),
            pl.BlockSpec((1, 2 * V_AUG, nc), full),
            pl.BlockSpec((1, LANE, Q_BLOCK), qcol),
            pl.BlockSpec((1, LANE, Q_BLOCK), qnext),
            pl.BlockSpec(ovt.shape, const),
        ],
        out_specs=pl.BlockSpec((1, Q_BLOCK, NSA_WIDTH), lambda i, j: (i, j, 0)),
        out_shape=jax.ShapeDtypeStruct((b, t, NSA_WIDTH), BF16),
        scratch_shapes=[
            pltpu.VMEM((N_PAIRS, 2 * LANE, 2 * Q_BLOCK), BF16),
            pltpu.VMEM((N_PAIRS, LANE, 2 * Q_BLOCK), BF16),
            pltpu.VMEM((NSA_GROUPS, LANE, Q_BLOCK), BF16),
            pltpu.VMEM((NSA_WIDTH, Q_BLOCK), F32),
            pltpu.VMEM((N_PAIRS, nc, 2 * Q_BLOCK), F32),
            pltpu.VMEM((N_PAIRS, WINDOW + Q_BLOCK, 2 * Q_BLOCK), F32),
            pltpu.VMEM((2, N_PAIRS, KEY_UNIT, 2 * Q_BLOCK), F32),
            pltpu.VMEM((2, N_PAIRS, 1, 2 * Q_BLOCK), F32),
            pltpu.VMEM((N_PAIRS, max(nc, KEY_UNIT), 2 * Q_BLOCK), BF16),
            pltpu.VMEM((N_PAIRS, WINDOW + Q_BLOCK, 2 * Q_BLOCK), BF16),
            pltpu.VMEM((N_PAIRS, 1, 2 * Q_BLOCK), F32),
            pltpu.VMEM((N_PAIRS, V_AUG, 2 * Q_BLOCK), F32),
            pltpu.VMEM((NSA_WIDTH, Q_BLOCK), F32),
        ],
        compiler_params=pltpu.CompilerParams(
            dimension_semantics=("arbitrary", "arbitrary"), vmem_limit_bytes=VMEM_LIMIT),
        name="nsa",
    )(qt, qt, ksa, vst, kw, vwt, kc, vct, gt, gt, ovt)


def _ff_chunks(d_ff, width):
    return tuple((s, min(width, d_ff - s)) for s in range(0, d_ff, width))


def _out_ffn_kernel(x_ref, on_ref, oh_ref, wo_ref, g_ref, wgu_ref, wd_ref, o_ref, *, chunks):
    d_ff = wd_ref.shape[0]
    x1 = (x_ref[...] + _dot(on_ref[...], wo_ref[0:NSA_WIDTH, :])
          + _dot(oh_ref[...], wo_ref[NSA_WIDTH:NSA_WIDTH + HG_WIDTH, :]))
    ms = jnp.mean(x1 * x1, axis=-1, keepdims=True)
    h = (x1 * lax.rsqrt(ms + RMS_EPS) * g_ref[...]).astype(BF16)
    o_ref[...] = x1
    for s, n in chunks:
        gate = _dot(h, wgu_ref[:, s:s + n])
        up = _dot(h, wgu_ref[:, d_ff + s:d_ff + s + n])
        act = (jax.nn.silu(gate) * up).astype(BF16)
        o_ref[...] += _dot(act, wd_ref[s:s + n, :])


def _out_ffn(x2, o_nsa, o_hg, wo, g, wgu, wd, tm=1024, ff_width=4 * MXU_WIDTH):
    n, d = x2.shape
    d_ff = wd.shape[0]
    row = lambda i: (i, 0)
    const = lambda i: (0, 0)
    resident = lambda a: pl.BlockSpec(a.shape, const, pipeline_mode=pl.Buffered(1))
    return pl.pallas_call(
        functools.partial(_out_ffn_kernel, chunks=_ff_chunks(d_ff, ff_width)),
        grid=(n // tm,),
        in_specs=[
            pl.BlockSpec((tm, d), row),
            pl.BlockSpec((tm, NSA_WIDTH), row),
            pl.BlockSpec((tm, HG_WIDTH), row),
            resident(wo),
            pl.BlockSpec((1, d), const),
            resident(wgu),
            resident(wd),
        ],
        out_specs=pl.BlockSpec((tm, d), row),
        out_shape=jax.ShapeDtypeStruct((n, d), F32),
        compiler_params=pltpu.CompilerParams(
            dimension_semantics=("arbitrary",), vmem_limit_bytes=VMEM_LIMIT),
        name="out_ffn",
    )(x2, o_nsa, o_hg, wo, g, wgu, wd)


def _expand_cmp_weights(pos, w1, w2):
    eye = jnp.eye(NSA_GROUPS, dtype=F32)
    w2x = jnp.einsum('cd,gk->gckd', w2, eye).reshape(NSA_GROUPS * CMP_HIDDEN, NSA_GROUPS * HEAD_DIM).astype(BF16)
    pbias = jnp.tile(pos.reshape(1, CMP_LEN * HEAD_DIM) @ w1, (1, NSA_GROUPS))
    return w1, w2x, pbias


def _mixers(x, norm_mix, w_in, q_norm, k_norm, cmp_pos_k, cmp_pos_v, cmp_k_w1, cmp_k_w2, cmp_v_w1, cmp_v_w2,
            hgrn_lb_logits, hgrn_o_norm, to_cast=()):
    b, t, d = x.shape
    depth = norm_mix.shape[0]
    assert depth == 1 and hgrn_lb_logits.shape[0] == 2
    assert t % KEY_UNIT == 0 and t >= WINDOW + Q_BLOCK and t // SLC_LEN <= LANE
    assert (t // CMP_STRIDE) % KEY_TILE == 0
    l = 0

    bd = jnp.asarray(np.kron(np.eye(LANE // HEAD_DIM), np.ones((HEAD_DIM, HEAD_DIM))), BF16)
    tile2 = lambda v: jnp.tile(v.reshape(1, HEAD_DIM), (1, LANE // HEAD_DIM)).astype(F32)
    qg = tile2(q_norm[l]) * (HEAD_DIM ** -0.5 * math.log2(math.e))

    qt, kc_raw, vc_raw, ksa, vst, kw, vwt, gt, o_hg, *casted = _in_proj(
        x, norm_mix[l].reshape(1, d), w_in[l].T, bd, qg, tile2(k_norm[l, 1]), tile2(k_norm[l, 2]),
        hgrn_lb_logits, hgrn_o_norm[l].reshape(1, HGRN_DIM), list(to_cast))

    ns = t // CMP_STRIDE
    kc, vct = _compress(
        kc_raw, vc_raw,
        _expand_cmp_weights(cmp_pos_k[l], cmp_k_w1[l], cmp_k_w2[l]),
        _expand_cmp_weights(cmp_pos_v[l], cmp_v_w1[l], cmp_v_w2[l]),
        bd, tile2(k_norm[l, 0]))

    cs = np.arange(ns)[None, :] * CMP_STRIDE
    ss = np.arange(LANE)[:, None] * SLC_LEN
    ovt = jnp.asarray(((cs < ss + SLC_LEN) & (cs + CMP_LEN > ss)).astype(np.float32), BF16)

    o_nsa = _nsa(qt, ksa, vst, kw, vwt, kc, vct, gt, ovt)
    return o_nsa, o_hg, casted


def kernel(x, norm_mix, w_in, q_norm, k_norm, cmp_pos_k, cmp_pos_v, cmp_k_w1, cmp_k_w2, cmp_v_w1, cmp_v_w2,
           hgrn_lb_logits, hgrn_o_norm, w_out, norm_ffn, w_gate_up, w_down):
    b, t, d = x.shape
    l = 0
    o_nsa, o_hg, (wo, wgu, wd) = _mixers(
        x, norm_mix, w_in, q_norm, k_norm, cmp_pos_k, cmp_pos_v, cmp_k_w1, cmp_k_w2, cmp_v_w1, cmp_v_w2,
        hgrn_lb_logits, hgrn_o_norm, to_cast=(w_out[l], w_gate_up[l], w_down[l]))
    out = _out_ffn(
        x.reshape(b * t, d), o_nsa.reshape(b * t, NSA_WIDTH), o_hg.reshape(b * t, HG_WIDTH),
        wo, norm_ffn[l].reshape(1, d), wgu, wd)
    return out.reshape(b, t, d)
```

```python
import functools
import math

import numpy as np
import jax
import jax.numpy as jnp
from jax import lax
from jax.experimental import pallas as pl
from jax.experimental.pallas import tpu as pltpu

F32 = jnp.float32
BF16 = jnp.bfloat16

LANE = 128
BF16_SUBLANES = 16
MXU_WIDTH = 256

NSA_HEADS = 8
NSA_GROUPS = 2
NSA_REP = NSA_HEADS // NSA_GROUPS
HEAD_DIM = 64
CMP_LEN = 32
CMP_STRIDE = 16
CMP_HIDDEN = 128
SLC_LEN = 64
N_SELECT = 16
WINDOW = 512
Q_BLOCK = 128
HGRN_HEADS = 4
HGRN_DIM = 128
HGRN_CHUNK = 64
HGRN_SUB = 16
RMS_EPS = 1e-6
NEG = -1e30

NSA_WIDTH = NSA_HEADS * HEAD_DIM
KV_WIDTH = NSA_GROUPS * HEAD_DIM
HG_WIDTH = HGRN_HEADS * HGRN_DIM
N_GATES = 3 * NSA_HEADS

KEY_TILE = 128
KEY_UNIT = 4 * KEY_TILE
N_PAIRS = NSA_HEADS // 2
V_AUG = HEAD_DIM + 16
VMEM_LIMIT = 56 * 1024 * 1024


def _dot(a, b):
    return jnp.dot(a, b, preferred_element_type=F32)


def _dot_t(a, b):
    return lax.dot_general(a, b, (((1,), (1,)), ((), ())), preferred_element_type=F32)


def _group_rms(y, bd, gain):
    ss = _dot((y * y).astype(BF16), bd)
    return y * lax.rsqrt(ss * (1.0 / HEAD_DIM) + RMS_EPS) * gain


def _values_t_aug(v):
    vt = v.T
    ones = jnp.ones((V_AUG - HEAD_DIM, v.shape[0]), F32)
    return jnp.concatenate([vt[0:HEAD_DIM], ones, vt[HEAD_DIM:2 * HEAD_DIM], ones], axis=0).astype(BF16)


def _hgrn_stages(ph_ref, lb_ref, og_ref, o_ref, st_ref, chunks):
    c_len = HGRN_CHUNK
    n_sub = c_len // HGRN_SUB
    lg = lb_ref[...]
    lmax = jnp.max(lg, axis=0, keepdims=True)
    le = jnp.exp(lg - lmax)
    lb_all = le[0:1] / jnp.sum(le, axis=0, keepdims=True)

    ri = lax.broadcasted_iota(jnp.int32, (c_len, c_len), 0)
    ci = lax.broadcasted_iota(jnp.int32, (c_len, c_len), 1)
    tri = (ci <= ri).astype(BF16)
    causal = ci <= ri
    gain = og_ref[...]
    heads = [slice(h * HGRN_DIM, (h + 1) * HGRN_DIM) for h in range(HGRN_HEADS)]
    rows = [slice(c * c_len, (c + 1) * c_len) for c in range(chunks)]

    pre = []
    for rs in rows:
        qv = jax.nn.silu(ph_ref[rs, 0:HG_WIDTH])
        f = lb_all + (1.0 - lb_all) * jax.nn.sigmoid(ph_ref[rs, HG_WIDTH:2 * HG_WIDTH])
        logf = jnp.log(f)
        g1 = logf.astype(BF16)
        g2 = (logf - g1.astype(F32)).astype(BF16)
        pre.append((qv, 1.0 - f, g1, g2))
    his = [ph_ref[rs, 2 * HG_WIDTH:3 * HG_WIDTH] for rs in rows]
    gates = [jax.nn.silu(ph_ref[rs, 3 * HG_WIDTH:4 * HG_WIDTH]) for rs in rows]
    yield

    bcums = [_dot(tri, g1) + _dot(tri, g2) for _, _, g1, g2 in pre]
    yield

    mid = []
    for c in range(chunks):
        qv, kv = pre[c][0], pre[c][1]
        bcum = bcums[c]
        b_last = bcum[c_len - 1:c_len]
        starts = [jnp.zeros((1, HG_WIDTH), F32)] + [bcum[i * HGRN_SUB - 1:i * HGRN_SUB] for i in range(1, n_sub)]
        ends = starts[1:] + [b_last]

        def per_sub(vals):
            return jnp.concatenate([jnp.broadcast_to(r, (HGRN_SUB, HG_WIDTH)) for r in vals], axis=0)

        start_full = per_sub(starts)
        qd = qv * jnp.exp(bcum - start_full)
        kb = kv * jnp.exp(per_sub(ends) - bcum)
        kdiag = kv * jnp.exp(start_full - bcum)
        qs = (qd * per_sub([jnp.exp(s) for s in starts])).astype(BF16)
        kdec = (kb * per_sub([jnp.exp(b_last - e) for e in ends])).astype(BF16)
        kds = []
        for i in range(n_sub):
            blocks = [kb[j * HGRN_SUB:(j + 1) * HGRN_SUB] * jnp.exp(starts[i] - ends[j]) for j in range(i - 1)]
            if i > 0:
                blocks.append(kb[(i - 1) * HGRN_SUB:i * HGRN_SUB])
            blocks.append(kdiag[i * HGRN_SUB:(i + 1) * HGRN_SUB])
            if i + 1 < n_sub:
                blocks.append(jnp.zeros(((n_sub - 1 - i) * HGRN_SUB, HG_WIDTH), F32))
            kds.append(jnp.concatenate(blocks, axis=0).astype(BF16))
        mid.append((qd.astype(BF16), kds, qs, kdec, jnp.exp(b_last)))
    attn = {}
    upd = {}
    for c in range(chunks):
        qd, kds, qs, kdec, dec = mid[c]
        for h, hs in enumerate(heads):
            both = _dot_t(qd[:, hs], jnp.concatenate([kd[:, hs] for kd in kds], axis=0))
            own = jnp.concatenate([both[i * HGRN_SUB:(i + 1) * HGRN_SUB, i * c_len:(i + 1) * c_len]
                                   for i in range(n_sub)], axis=0)
            attn[c, h] = jnp.where(causal, own, 0.0).astype(BF16)
            upd[c, h] = _dot(his[c][:, hs].T.astype(BF16), kdec[:, hs])
    yield

    state = {}
    for h, hs in enumerate(heads):
        st = st_ref[h]
        for c in range(chunks):
            state[c, h] = st.astype(BF16)
            st = st * mid[c][4][:, hs] + upd[c, h]
        st_ref[h] = st
    for c, rs in enumerate(rows):
        qs = mid[c][2]
        for h, hs in enumerate(heads):
            o = _dot(attn[c, h], his[c][:, hs].astype(BF16)) + _dot_t(qs[:, hs], state[c, h])
            ms = jnp.mean(o * o, axis=-1, keepdims=True)
            o = o * lax.rsqrt(ms + RMS_EPS) * gain * gates[c][:, hs]
            o_ref[0, rs, hs] = o.astype(BF16)
    yield


def _in_proj_kernel(x_ref, g_ref, wt_ref, bd_ref, qg_ref, ksg_ref, kwg_ref, lb_ref, og_ref, *refs, n_cast):
    cast_in, refs = refs[:n_cast], refs[n_cast:]
    qt_ref, kc_ref, vc_ref, ksa_ref, vst_ref, kw_ref, vwt_ref, gt_ref, ohg_ref = refs[:9]
    cast_out, (wp_ref, ph_ref, st_ref) = refs[9:9 + n_cast], refs[9 + n_cast:]
    for src_ref, dst_ref in zip(cast_in, cast_out):
        dst_ref[...] = src_ref[...].astype(BF16)

    tm = x_ref.shape[1]
    d = x_ref.shape[2]
    j = pl.program_id(1)
    tile = jnp.minimum(j, pl.num_programs(1) - 2)
    o_g = NSA_WIDTH + 6 * KV_WIDTH

    @pl.when((pl.program_id(0) == 0) & (j == 0))
    def _():
        o_h = o_g + N_GATES
        sources = [c * LANE for c in range(o_g // LANE)] + [o_h + c * LANE for c in range(4 * HG_WIDTH // LANE)]

        def copy_rows(i, carry):
            rs = pl.ds(pl.multiple_of(i * LANE, LANE), LANE)
            for c, src in enumerate(sources):
                wp_ref[rs, c * LANE:(c + 1) * LANE] = wt_ref[src:src + LANE, rs].T.astype(BF16)
            tail = jnp.concatenate([wt_ref[o_g:o_h, rs], jnp.zeros((LANE - N_GATES, LANE), F32)], axis=0)
            wp_ref[rs, o_g + 4 * HG_WIDTH:o_g + 4 * HG_WIDTH + LANE] = tail.T.astype(BF16)
            return carry

        lax.fori_loop(0, d // LANE, copy_rows, 0)
        ph_ref[...] = jnp.zeros(ph_ref.shape, F32)
        st_ref[...] = jnp.zeros(st_ref.shape, F32)

    x = x_ref[0]
    ms = jnp.mean(x * x, axis=-1, keepdims=True)
    h = (x * lax.rsqrt(ms + RMS_EPS) * g_ref[...]).astype(BF16)

    mixer = _hgrn_stages(ph_ref, lb_ref, og_ref, ohg_ref, st_ref, tm // HGRN_CHUNK)
    next(mixer)
    y_nsa = _dot(h, wp_ref[:, 0:o_g])
    next(mixer)

    def hgrn_cols(k):
        return _dot(h, wp_ref[:, o_g + k * HG_WIDTH:o_g + (k + 1) * HG_WIDTH])

    y_h = [hgrn_cols(0), hgrn_cols(1)]
    next(mixer)

    bd = bd_ref[...]
    for r in range(NSA_WIDTH // LANE):
        sl = slice(r * LANE, (r + 1) * LANE)
        qt_ref[0, sl, :] = _group_rms(y_nsa[:, sl], bd, qg_ref[...]).T.astype(BF16)
    o = NSA_WIDTH
    n_seg = tm // CMP_STRIDE
    r_out = lax.broadcasted_iota(jnp.int32, (tm, tm), 0)
    r_in = lax.broadcasted_iota(jnp.int32, (tm, tm), 1)
    perm = jnp.where(r_in == CMP_STRIDE * (r_out % n_seg) + r_out // n_seg, 1.0, 0.0).astype(BF16)
    raw = _dot(perm, y_nsa[:, o:o + 2 * LANE].astype(BF16))
    for i in range(CMP_STRIDE):
        kc_ref[0, :, i * LANE:(i + 1) * LANE] = raw[i * n_seg:(i + 1) * n_seg, 0:LANE].astype(BF16)
        vc_ref[0, :, i * LANE:(i + 1) * LANE] = raw[i * n_seg:(i + 1) * n_seg, LANE:2 * LANE].astype(BF16)
    ksa_ref[0, :, 0:LANE] = _group_rms(y_nsa[:, o + 2 * LANE:o + 3 * LANE], bd, ksg_ref[...]).astype(BF16)
    key = tile * tm + lax.broadcasted_iota(jnp.int32, (tm, LANE), 0)
    blk = lax.broadcasted_iota(jnp.int32, (tm, LANE), 1)
    ksa_ref[0, :, LANE:2 * LANE] = jnp.where(lax.shift_right_logical(key, 6) == blk, 1.0, 0.0).astype(BF16)
    vst_ref[0] = _values_t_aug(y_nsa[:, o + 3 * LANE:o + 4 * LANE])
    kw_ref[0] = _group_rms(y_nsa[:, o + 4 * LANE:o + 5 * LANE], bd, kwg_ref[...]).astype(BF16)
    vwt_ref[0] = _values_t_aug(y_nsa[:, o + 5 * LANE:o + 6 * LANE])
    y_gate = _dot(h, wp_ref[:, o_g + 4 * HG_WIDTH:o_g + 4 * HG_WIDTH + LANE])
    gt_ref[0] = jax.nn.sigmoid(y_gate).T
    y_h.append(hgrn_cols(2))
    next(mixer)
    y_h.append(hgrn_cols(3))
    for k in range(4):
        ph_ref[:, k * HG_WIDTH:(k + 1) * HG_WIDTH] = y_h[k]

    @pl.when(j == 0)
    def _():
        st_ref[...] = jnp.zeros(st_ref.shape, F32)


def _slab_rows(rows, n_steps):
    return next(r for r in range(BF16_SUBLANES, rows + 1, BF16_SUBLANES) if rows % r == 0 and rows // r <= n_steps)


def _in_proj(x, g, wt, bd, qg, ksg, kwg, lb_logits, o_gain, to_cast, tm=256):
    b, t, d = x.shape
    n_t = t // tm
    seg_w = CMP_STRIDE * KV_WIDTH
    slabs = [_slab_rows(a.shape[0], b * (n_t + 1)) for a in to_cast]

    def slab_spec(a, r):
        return pl.BlockSpec((r, a.shape[1]), lambda i, j: (jnp.minimum(i * (n_t + 1) + j, a.shape[0] // r - 1), 0))

    cast_specs = [slab_spec(a, r) for a, r in zip(to_cast, slabs)]
    nw = NSA_WIDTH + 6 * KV_WIDTH + 4 * HG_WIDTH + LANE
    rows = lambda i, j: (i, jnp.minimum(j, n_t - 1), 0)
    cols = lambda i, j: (i, 0, jnp.minimum(j, n_t - 1))
    prev = lambda i, j: (i, jnp.maximum(j - 1, 0), 0)
    const = lambda i, j: (0, 0)
    outs = [
        (jax.ShapeDtypeStruct((b, NSA_WIDTH, t), BF16), pl.BlockSpec((1, NSA_WIDTH, tm), cols)),
        (jax.ShapeDtypeStruct((b, t // CMP_STRIDE, seg_w), BF16), pl.BlockSpec((1, tm // CMP_STRIDE, seg_w), rows)),
        (jax.ShapeDtypeStruct((b, t // CMP_STRIDE, seg_w), BF16), pl.BlockSpec((1, tm // CMP_STRIDE, seg_w), rows)),
        (jax.ShapeDtypeStruct((b, t, 2 * LANE), BF16), pl.BlockSpec((1, tm, 2 * LANE), rows)),
        (jax.ShapeDtypeStruct((b, 2 * V_AUG, t), BF16), pl.BlockSpec((1, 2 * V_AUG, tm), cols)),
        (jax.ShapeDtypeStruct((b, t, LANE), BF16), pl.BlockSpec((1, tm, LANE), rows)),
        (jax.ShapeDtypeStruct((b, 2 * V_AUG, t), BF16), pl.BlockSpec((1, 2 * V_AUG, tm), cols)),
        (jax.ShapeDtypeStruct((b, LANE, t), F32), pl.BlockSpec((1, LANE, tm), cols)),
        (jax.ShapeDtypeStruct((b, t, HG_WIDTH), BF16), pl.BlockSpec((1, tm, HG_WIDTH), prev)),
    ]
    outs += [(jax.ShapeDtypeStruct(a.shape, BF16), s) for a, s in zip(to_cast, cast_specs)]
    return pl.pallas_call(
        functools.partial(_in_proj_kernel, n_cast=len(to_cast)),
        grid=(b, n_t + 1),
        in_specs=[
            pl.BlockSpec((1, tm, d), rows),
            pl.BlockSpec((1, d), const),
            pl.BlockSpec(wt.shape, const, pipeline_mode=pl.Buffered(1)),
            pl.BlockSpec((LANE, LANE), const),
            pl.BlockSpec((1, LANE), const),
            pl.BlockSpec((1, LANE), const),
            pl.BlockSpec((1, LANE), const),
            pl.BlockSpec(lb_logits.shape, const),
            pl.BlockSpec((1, HGRN_DIM), const),
        ] + cast_specs,
        out_specs=[s for _, s in outs],
        out_shape=[s for s, _ in outs],
        scratch_shapes=[
            pltpu.VMEM((d, nw), BF16),
            pltpu.VMEM((tm, 4 * HG_WIDTH), F32),
            pltpu.VMEM((HGRN_HEADS, HGRN_DIM, HGRN_DIM), F32),
        ],
        compiler_params=pltpu.CompilerParams(
            dimension_semantics=("arbitrary", "arbitrary"), vmem_limit_bytes=VMEM_LIMIT),
        name="in_proj",
    )(x, g, wt, bd, qg, ksg, kwg, lb_logits, o_gain, *to_cast)


def _compress_kernel(xk_ref, xv_ref, wk1_ref, wk2_ref, pk_ref, wv1_ref, wv2_ref, pv_ref, bd_ref, kg_ref,
                     kc_ref, vct_ref, lift_ref):
    nb, ns, wd = xk_ref.shape
    zeros = jnp.zeros((HEAD_DIM, CMP_HIDDEN), BF16)

    def mlp(x_ref, w1_ref, w2_ref, p_ref, lifted):
        for half in range(CMP_LEN // CMP_STRIDE):
            for tok in range(CMP_STRIDE):
                r = (half * CMP_STRIDE + tok) * HEAD_DIM
                w = w1_ref[r:r + HEAD_DIM, :].astype(BF16)
                for g in range(NSA_GROUPS):
                    row = (tok * NSA_GROUPS + g) * HEAD_DIM
                    lifted[half, row:row + HEAD_DIM, :] = jnp.concatenate(
                        [w if k == g else zeros for k in range(NSA_GROUPS)], axis=1)
        xb = x_ref[...].reshape(nb * ns, wd)
        a = _dot(xb, lifted[0])
        b = _dot(xb, lifted[1])
        h = a + pltpu.roll(b, nb * ns - 1, axis=0) + p_ref[...]
        return _dot(jax.nn.gelu(h).astype(BF16), w2_ref[...])

    kc = mlp(xk_ref, wk1_ref, wk2_ref, pk_ref, lift_ref.at[0])
    kc_ref[...] = _group_rms(kc, bd_ref[...], kg_ref[...]).astype(BF16).reshape(nb, ns, LANE)
    vct = _values_t_aug(mlp(xv_ref, wv1_ref, wv2_ref, pv_ref, lift_ref.at[1]))
    for i in range(nb):
        vct_ref[i] = vct[:, i * ns:(i + 1) * ns]


def _compress(xk, xv, wk, wv, bd, kg):
    b, ns, wd = xk.shape
    const2 = lambda i: (0, 0)
    whole = lambda i: (0, 0, 0)
    wspecs = [pl.BlockSpec(w.shape, const2) for w in wk] + [pl.BlockSpec(w.shape, const2) for w in wv]
    return pl.pallas_call(
        _compress_kernel,
        grid=(1,),
        in_specs=[pl.BlockSpec((b, ns, wd), whole), pl.BlockSpec((b, ns, wd), whole)] + wspecs + [
            pl.BlockSpec((LANE, LANE), const2), pl.BlockSpec((1, LANE), const2)],
        out_specs=[pl.BlockSpec((b, ns, LANE), whole), pl.BlockSpec((b, 2 * V_AUG, ns), whole)],
        out_shape=[jax.ShapeDtypeStruct((b, ns, LANE), BF16), jax.ShapeDtypeStruct((b, 2 * V_AUG, ns), BF16)],
        scratch_shapes=[pltpu.VMEM((2, CMP_LEN // CMP_STRIDE, wd, NSA_GROUPS * CMP_HIDDEN), BF16)],
        compiler_params=pltpu.CompilerParams(
            dimension_semantics=("arbitrary",), vmem_limit_bytes=VMEM_LIMIT),
        name="compress",
    )(xk, xv, *wk, *wv, bd, kg)


def _topk_bias_t(imp_ts, s0):
    nj, nq = imp_ts[0].shape
    jidx = lax.broadcasted_iota(jnp.int32, (nj, nq), 0)
    tq = s0 + lax.broadcasted_iota(jnp.int32, (nj, nq), 1)
    jcur = lax.shift_right_logical(tq, 6)
    forced = (jidx == 0) | (jidx == jcur) | (jidx == jcur - 1)
    future = jidx > jcur
    vs = [jnp.where(forced, -2.0, jnp.where(future, -1.0, imp_t)) for imp_t in imp_ts]
    jf = jidx.astype(F32)
    for _ in range(N_SELECT - 3):
        for g in range(len(vs)):
            m = jnp.max(vs[g], axis=0, keepdims=True)
            jm = jnp.min(jnp.where(vs[g] == m, jf, float(nj)), axis=0, keepdims=True)
            vs[g] = jnp.where(jf == jm, -2.0, vs[g])
    return [jnp.where(future, NEG, jnp.where(v < -1.5, 0.0, NEG)) for v in vs]


def _nsa_kernel(qt_ref, qtn_ref, ksa_ref, vst_ref, kw_ref, vwt_ref, kc_ref, vct_ref, gt_ref, gtn_ref, ovt_ref, o_ref,
                w_ref, wn_ref, bias_ref, cmp_ref, sc_ref, sw_ref, ss_ref, mx_ref, e_ref, ew_ref,
                m_ref, acc_ref, out_ref):
    qb = pl.program_id(1)
    s0 = qb * Q_BLOCK
    nc = kc_ref.shape[1]
    two = 2 * Q_BLOCK
    lane_q = lax.broadcasted_iota(jnp.int32, (1, two), 1) & (Q_BLOCK - 1)
    t_lane = s0 + lane_q
    gt = gt_ref[0]

    def stationary_q(q_ref, dst):
        zero = jnp.zeros((HEAD_DIM, Q_BLOCK), BF16)
        for p in range(N_PAIRS):
            g = (2 * p) // NSA_REP
            halves = []
            for h in (2 * p, 2 * p + 1):
                qh = q_ref[0, h * HEAD_DIM:(h + 1) * HEAD_DIM, :]
                halves.append(jnp.concatenate([qh, zero] if g == 0 else [zero, qh], axis=0))
            dst[p, 0:LANE, :] = jnp.concatenate(halves, axis=1)

    def store_scores(dst, s, valid):
        if valid is not None:
            s = jnp.where(valid, s, NEG)
        dst[...] = s
        return jnp.max(s, axis=0, keepdims=True)

    def exp_tiles(load, n_tiles, m, e_dst):
        for c in range(n_tiles):
            rows = slice(c * KEY_TILE, (c + 1) * KEY_TILE)
            e_dst[rows, :] = jnp.exp2(load(rows) - m).astype(BF16)

    def values_t(ref, g, cols):
        return ref[0, g * V_AUG:(g + 1) * V_AUG, cols]

    def emit(dst, gates, p, branch, o_aug, first, guard=None):
        inv = 1.0 / o_aug[HEAD_DIM:HEAD_DIM + 1, :]
        if guard is not None:
            inv = jnp.where(guard, inv, 0.0)
        for hh in range(2):
            h = 2 * p + hh
            ls = slice(hh * Q_BLOCK, (hh + 1) * Q_BLOCK)
            o = o_aug[0:HEAD_DIM, ls] * (inv[:, ls] * gates[3 * h + branch:3 * h + branch + 1, :])
            rs = slice(h * HEAD_DIM, (h + 1) * HEAD_DIM)
            if first:
                dst[rs, :] = o
            else:
                dst[rs, :] += o
        return inv

    def key_rows(n):
        return lax.broadcasted_iota(jnp.int32, (n, two), 0)

    def cmp_scores(q_ref, s0x):
        stationary_q(q_ref, wn_ref)
        kc = kc_ref[0]
        valid = key_rows(nc) <= lax.shift_right_arithmetic(s0x + lane_q - (CMP_LEN - 1), 4)
        return [store_scores(sc_ref.at[p], _dot(kc, wn_ref[p]), valid) for p in range(N_PAIRS)]

    def cmp_finish(gates, m_cmp):
        imp_t = [jnp.zeros((LANE, Q_BLOCK), F32) for _ in range(NSA_GROUPS)]
        for p in range(N_PAIRS):
            exp_tiles(lambda rows: sc_ref[p, rows, :], nc // KEY_TILE, m_cmp[p], e_ref.at[p])
        for p in range(N_PAIRS):
            g = (2 * p) // NSA_REP
            inv = emit(cmp_ref, gates, p, 0, _dot(values_t(vct_ref, g, slice(None)), e_ref[p, 0:nc, :]), True,
                       guard=m_cmp[p] > 0.5 * NEG)
            imp_p = _dot(ovt_ref[...], e_ref[p, 0:nc, :]) * inv
            imp_t[g] = imp_t[g] + imp_p[:, 0:Q_BLOCK] + imp_p[:, Q_BLOCK:two]
        return imp_t

    def select(imp_t, s0x):
        for g, bias_t in enumerate(_topk_bias_t(imp_t, s0x)):
            bias_ref[g] = bias_t.astype(BF16)

    @pl.when(qb == 0)
    def _():
        select(cmp_finish(gt, cmp_scores(qt_ref, s0)), s0)

    w0 = pl.multiple_of(jnp.maximum(s0 - WINDOW, 0), KEY_TILE)
    wlen = WINDOW + Q_BLOCK
    kwin = kw_ref[0, pl.ds(w0, wlen), :]
    vwin = [values_t(vwt_ref, g, pl.ds(w0, wlen)) for g in range(NSA_GROUPS)]

    stationary_q(qt_ref, w_ref)
    for p in range(N_PAIRS):
        bias_t = bias_ref[(2 * p) // NSA_REP]
        w_ref[p, LANE:2 * LANE, :] = jnp.concatenate([bias_t, bias_t], axis=1)
    out_ref[...] = cmp_ref[...]

    imp_t = cmp_finish(gtn_ref[0], cmp_scores(qtn_ref, s0 + Q_BLOCK))
    diff = (t_lane - w0) - key_rows(wlen)
    win_valid = lax.shift_right_arithmetic(diff, WINDOW.bit_length() - 1) == 0
    m_win = [store_scores(sw_ref.at[p], _dot(kwin, w_ref[p, 0:LANE, :]), win_valid) for p in range(N_PAIRS)]
    for p in range(N_PAIRS):
        exp_tiles(lambda rows: sw_ref[p, rows, :], wlen // KEY_TILE, m_win[p], ew_ref.at[p])
    for p in range(N_PAIRS):
        emit(out_ref, gt, p, 2, _dot(vwin[(2 * p) // NSA_REP], ew_ref[p]), False)

    m_ref[...] = jnp.full(m_ref.shape, NEG, F32)
    acc_ref[...] = jnp.zeros(acc_ref.shape, F32)

    def sel_scores(u, buf, p, causal):
        k0 = pl.multiple_of(u * KEY_UNIT, KEY_UNIT)
        ku = ksa_ref[0, pl.ds(k0, KEY_UNIT), :]
        valid = (k0 + key_rows(KEY_UNIT)) <= t_lane if causal else None
        mx_ref[buf, p] = store_scores(ss_ref.at[buf, p], _dot(ku, w_ref[p]), valid)

    def sel_update(u, buf, p):
        k0 = pl.multiple_of(u * KEY_UNIT, KEY_UNIT)
        g = (2 * p) // NSA_REP
        m_old = m_ref[p]
        m = jnp.maximum(m_old, mx_ref[buf, p])
        exp_tiles(lambda rows: ss_ref[buf, p, rows, :], KEY_UNIT // KEY_TILE, m, e_ref.at[p])
        m_ref[p] = m
        acc_ref[p] = jnp.exp2(m_old - m) * acc_ref[p] + _dot(
            values_t(vst_ref, g, pl.ds(k0, KEY_UNIT)), e_ref[p, 0:KEY_UNIT, :])

    def sel_step(u, buf, next_causal):
        for p in range(N_PAIRS):
            sel_scores(u + 1, 1 - buf, p, next_causal)
            sel_update(u, buf, p)

    def sel_last(u, buf):
        for p in range(N_PAIRS):
            sel_update(u, buf, p)

    n_before = qb // (KEY_UNIT // Q_BLOCK)
    n_trips = jnp.maximum(n_before - 1, 0) // 2
    rest = 2 * n_trips

    for p in range(N_PAIRS):
        sel_scores(0, 0, p, False)
    select(imp_t, s0 + Q_BLOCK)

    def body(i, carry):
        sel_step(2 * i, 0, False)
        sel_step(2 * i + 1, 1, False)
        return carry

    def body2(i, carry):
        body(2 * i, carry)
        return body(2 * i + 1, carry)

    lax.fori_loop(0, n_trips // 2, body2, 0)
    lax.fori_loop(2 * (n_trips // 2), n_trips, body, 0)

    @pl.when(n_before == 0)
    def _():
        for p in range(N_PAIRS):
            sel_scores(0, 0, p, True)
        sel_last(0, 0)

    @pl.when(n_before == rest + 1)
    def _():
        sel_step(rest, 0, True)
        sel_last(rest + 1, 1)

    @pl.when(n_before == rest + 2)
    def _():
        sel_step(rest, 0, False)
        sel_step(rest + 1, 1, True)
        sel_last(rest + 2, 0)

    for p in range(N_PAIRS):
        emit(out_ref, gt, p, 1, acc_ref[p], False)

    o_ref[0] = out_ref[...].T.astype(BF16)


def _nsa(qt, ksa, vst, kw, vwt, kc, vct, gt, ovt):
    b, _, t = qt.shape
    nc = kc.shape[1]
    n_qb = t // Q_BLOCK
    qcol = lambda i, j: (i, 0, j)
    qnext = lambda i, j: (i, 0, jnp.minimum(j + 1, n_qb - 1))
    full = lambda i, j: (i, 0, 0)
    const = lambda i, j: (0, 0)
    return pl.pallas_call(
        _nsa_kernel,
        grid=(b, n_qb),
        in_specs=[
            pl.BlockSpec((1, NSA_WIDTH, Q_BLOCK), qcol),
            pl.BlockSpec((1, NSA_WIDTH, Q_BLOCK), qnext),
            pl.BlockSpec((1, t, 2 * LANE), full),
            pl.BlockSpec((1, 2 * V_AUG, t), full),
            pl.BlockSpec((1, t, LANE), full),
            pl.BlockSpec((1, 2 * V_AUG, t), full),
            pl.BlockSpec((1, nc, LANE), full),
            pl.BlockSpec((1, 2 * V_AUG, nc), full),
            pl.BlockSpec((1, LANE, Q_BLOCK), qcol),
            pl.BlockSpec((1, LANE, Q_BLOCK), qnext),
            pl.BlockSpec(ovt.shape, const),
        ],
        out_specs=pl.BlockSpec((1, Q_BLOCK, NSA_WIDTH), lambda i, j: (i, j, 0)),
        out_shape=jax.ShapeDtypeStruct((b, t, NSA_WIDTH), BF16),
        scratch_shapes=[
            pltpu.VMEM((N_PAIRS, 2 * LANE, 2 * Q_BLOCK), BF16),
            pltpu.VMEM((N_PAIRS, LANE, 2 * Q_BLOCK), BF16),
            pltpu.VMEM((NSA_GROUPS, LANE, Q_BLOCK), BF16),
            pltpu.VMEM((NSA_WIDTH, Q_BLOCK), F32),
            pltpu.VMEM((N_PAIRS, nc, 2 * Q_BLOCK), F32),
            pltpu.VMEM((N_PAIRS, WINDOW + Q_BLOCK, 2 * Q_BLOCK), F32),
            pltpu.VMEM((2, N_PAIRS, KEY_UNIT, 2 * Q_BLOCK), F32),
            pltpu.VMEM((2, N_PAIRS, 1, 2 * Q_BLOCK), F32),
            pltpu.VMEM((N_PAIRS, max(nc, KEY_UNIT), 2 * Q_BLOCK), BF16),
            pltpu.VMEM((N_PAIRS, WINDOW + Q_BLOCK, 2 * Q_BLOCK), BF16),
            pltpu.VMEM((N_PAIRS, 1, 2 * Q_BLOCK), F32),
            pltpu.VMEM((N_PAIRS, V_AUG, 2 * Q_BLOCK), F32),
            pltpu.VMEM((NSA_WIDTH, Q_BLOCK), F32),
        ],
        compiler_params=pltpu.CompilerParams(
            dimension_semantics=("arbitrary", "arbitrary"), vmem_limit_bytes=VMEM_LIMIT),
        name="nsa",
    )(qt, qt, ksa, vst, kw, vwt, kc, vct, gt, gt, ovt)


def _ff_chunks(d_ff, width):
    return tuple((s, min(width, d_ff - s)) for s in range(0, d_ff, width))


def _out_ffn_kernel(x_ref, on_ref, oh_ref, wo_ref, g_ref, wgu_ref, wd_ref, o_ref, *, chunks):
    d_ff = wd_ref.shape[0]
    x1 = (x_ref[...] + _dot(on_ref[...], wo_ref[0:NSA_WIDTH, :])
          + _dot(oh_ref[...], wo_ref[NSA_WIDTH:NSA_WIDTH + HG_WIDTH, :]))
    ms = jnp.mean(x1 * x1, axis=-1, keepdims=True)
    h = (x1 * lax.rsqrt(ms + RMS_EPS) * g_ref[...]).astype(BF16)
    o_ref[...] = x1
    for s, n in chunks:
        gate = _dot(h, wgu_ref[:, s:s + n])
        up = _dot(h, wgu_ref[:, d_ff + s:d_ff + s + n])
        act = (jax.nn.silu(gate) * up).astype(BF16)
        o_ref[...] += _dot(act, wd_ref[s:s + n, :])


def _out_ffn(x2, o_nsa, o_hg, wo, g, wgu, wd, tm=1024, ff_width=4 * MXU_WIDTH):
    n, d = x2.shape
    d_ff = wd.shape[0]
    row = lambda i: (i, 0)
    const = lambda i: (0, 0)
    resident = lambda a: pl.BlockSpec(a.shape, const, pipeline_mode=pl.Buffered(1))
    return pl.pallas_call(
        functools.partial(_out_ffn_kernel, chunks=_ff_chunks(d_ff, ff_width)),
        grid=(n // tm,),
        in_specs=[
            pl.BlockSpec((tm, d), row),
            pl.BlockSpec((tm, NSA_WIDTH), row),
            pl.BlockSpec((tm, HG_WIDTH), row),
            resident(wo),
            pl.BlockSpec((1, d), const),
            resident(wgu),
            resident(wd),
        ],
        out_specs=pl.BlockSpec((tm, d), row),
        out_shape=jax.ShapeDtypeStruct((n, d), F32),
        compiler_params=pltpu.CompilerParams(
            dimension_semantics=("arbitrary",), vmem_limit_bytes=VMEM_LIMIT),
        name="out_ffn",
    )(x2, o_nsa, o_hg, wo, g, wgu, wd)


def _expand_cmp_weights(pos, w1, w2):
    eye = jnp.eye(NSA_GROUPS, dtype=F32)
    w2x = jnp.einsum('cd,gk->gckd', w2, eye).reshape(NSA_GROUPS * CMP_HIDDEN, NSA_GROUPS * HEAD_DIM).astype(BF16)
    pbias = jnp.tile(pos.reshape(1, CMP_LEN * HEAD_DIM) @ w1, (1, NSA_GROUPS))
    return w1, w2x, pbias


def _mixers(x, norm_mix, w_in, q_norm, k_norm, cmp_pos_k, cmp_pos_v, cmp_k_w1, cmp_k_w2, cmp_v_w1, cmp_v_w2,
            hgrn_lb_logits, hgrn_o_norm, to_cast=()):
    b, t, d = x.shape
    depth = norm_mix.shape[0]
    assert depth == 1 and hgrn_lb_logits.shape[0] == 2
    assert t % KEY_UNIT == 0 and t >= WINDOW + Q_BLOCK and t // SLC_LEN <= LANE
    assert (t // CMP_STRIDE) % KEY_TILE == 0
    l = 0

    bd = jnp.asarray(np.kron(np.eye(LANE // HEAD_DIM), np.ones((HEAD_DIM, HEAD_DIM))), BF16)
    tile2 = lambda v: jnp.tile(v.reshape(1, HEAD_DIM), (1, LANE // HEAD_DIM)).astype(F32)
    qg = tile2(q_norm[l]) * (HEAD_DIM ** -0.5 * math.log2(math.e))

    qt, kc_raw, vc_raw, ksa, vst, kw, vwt, gt, o_hg, *casted = _in_proj(
        x, norm_mix[l].reshape(1, d), w_in[l].T, bd, qg, tile2(k_norm[l, 1]), tile2(k_norm[l, 2]),
        hgrn_lb_logits, hgrn_o_norm[l].reshape(1, HGRN_DIM), list(to_cast))

    ns = t // CMP_STRIDE
    kc, vct = _compress(
        kc_raw, vc_raw,
        _expand_cmp_weights(cmp_pos_k[l], cmp_k_w1[l], cmp_k_w2[l]),
        _expand_cmp_weights(cmp_pos_v[l], cmp_v_w1[l], cmp_v_w2[l]),
        bd, tile2(k_norm[l, 0]))

    cs = np.arange(ns)[None, :] * CMP_STRIDE
    ss = np.arange(LANE)[:, None] * SLC_LEN
    ovt = jnp.asarray(((cs < ss + SLC_LEN) & (cs + CMP_LEN > ss)).astype(np.float32), BF16)

    o_nsa = _nsa(qt, ksa, vst, kw, vwt, kc, vct, gt, ovt)
    return o_nsa, o_hg, casted


def kernel(x, norm_mix, w_in, q_norm, k_norm, cmp_pos_k, cmp_pos_v, cmp_k_w1, cmp_k_w2, cmp_v_w1, cmp_v_w2,
           hgrn_lb_logits, hgrn_o_norm, w_out, norm_ffn, w_gate_up, w_down):
    b, t, d = x.shape
    l = 0
    o_nsa, o_hg, (wo, wgu, wd) = _mixers(
        x, norm_mix, w_in, q_norm, k_norm, cmp_pos_k, cmp_pos_v, cmp_k_w1, cmp_k_w2, cmp_v_w1, cmp_v_w2,
        hgrn_lb_logits, hgrn_o_norm, to_cast=(w_out[l], w_gate_up[l], w_down[l]))
    out = _out_ffn(
        x.reshape(b * t, d), o_nsa.reshape(b * t, NSA_WIDTH), o_hg.reshape(b * t, HG_WIDTH),
        wo, norm_ffn[l].reshape(1, d), wgu, wd)
    return out.reshape(b, t, d)
```
